```python
import jax, jax.numpy as jnp
from jax import lax
import numpy as np

D_MODEL = 2048
BATCH = 2
SEQ = 16384
DEPTH = 2

N_MIXERS = 2
N_RET_LAYERS = (DEPTH + 1) // 2
N_NSA_LAYERS = DEPTH // 2
EPS = 1e-6
NEG_INF = -1e30
FORCE_SCORE = 1e4

RET_HEADS = 8
RET_DK = D_MODEL // RET_HEADS
RET_DV = 2 * RET_DK
RET_QK_W = RET_HEADS * RET_DK
RET_V_W = RET_HEADS * RET_DV
RET_IN = 2 * RET_QK_W + 2 * RET_V_W
RET_CHUNK = 128

NSA_HEADS = 16
NSA_GROUPS = 4
NSA_REP = NSA_HEADS // NSA_GROUPS
NSA_DH = D_MODEL // NSA_HEADS
NSA_WIDTH = NSA_HEADS * NSA_DH
NSA_KV = NSA_GROUPS * NSA_DH
NSA_BRANCHES = 3
NSA_IN = NSA_WIDTH + 6 * NSA_KV + NSA_HEADS * NSA_BRANCHES + NSA_WIDTH
CMP_LEN = 32
CMP_STRIDE = 16
SLC_BLOCK = 64
TOP_N = 16
WINDOW = 512
Q_BLOCK = 128

kernel_name = "interleaved_retention_nsa_gated_hybrid"


def rmsnorm(x, g):
    xf = x.astype(jnp.float32)
    y = xf * lax.rsqrt(jnp.mean(xf * xf, axis=-1, keepdims=True) + EPS)
    return (y * g.astype(jnp.float32)).astype(x.dtype)


def masked_softmax(s, mask):
    p = jax.nn.softmax(jnp.where(mask, s, NEG_INF), axis=-1)
    return jnp.where(mask, p, 0.0)


def alibi_slopes(n_heads):
    return jnp.exp2(-8.0 * jnp.arange(1, n_heads + 1, dtype=jnp.float32) / n_heads)


def retention_mixer(h, w_in, w_out):
    B, S, _ = h.shape
    H, C = RET_HEADS, RET_CHUNK
    n = S // C
    proj = h @ w_in
    q, k, v, gate = jnp.split(proj, [RET_QK_W, 2 * RET_QK_W, 2 * RET_QK_W + RET_V_W], axis=-1)

    def heads(t, d):
        return t.reshape(B, n, C, H, d).transpose(1, 0, 3, 2, 4).astype(jnp.float32)

    q = heads(q, RET_DK)
    k = heads(k, RET_DK) * (RET_DK ** -0.5)
    v = heads(v, RET_DV)
    log_g = jnp.log1p(-jnp.exp2(-5.0 - jnp.arange(H, dtype=jnp.float32)))
    idx = jnp.arange(C, dtype=jnp.float32)
    diff = idx[:, None] - idx[None, :]
    decay_intra = jnp.where(diff >= 0, jnp.exp(diff[None] * log_g[:, None, None]), 0.0)
    q_decay = jnp.exp((idx[None, :] + 1.0) * log_g[:, None])[None, :, :, None]
    k_decay = jnp.exp((C - 1.0 - idx[None, :]) * log_g[:, None])[None, :, :, None]
    chunk_decay = jnp.exp(C * log_g)[None, :, None, None]

    def step(state, qkv):
        qc, kc, vc = qkv
        scores = jnp.einsum('bhnd,bhmd->bhnm', qc, kc) * decay_intra
        o = (jnp.einsum('bhnm,bhmv->bhnv', scores, vc)
             + jnp.einsum('bhnd,bhdv->bhnv', qc, state) * q_decay)
        state = state * chunk_decay + jnp.einsum('bhmd,bhmv->bhdv', kc * k_decay, vc)
        return state, o

    state0 = jnp.zeros((B, H, RET_DK, RET_DV), jnp.float32)
    _, o = lax.scan(step, state0, (q, k, v))
    mu = jnp.mean(o, axis=-1, keepdims=True)
    var = jnp.mean(jnp.square(o - mu), axis=-1, keepdims=True)
    o = (o - mu) * lax.rsqrt(var + EPS)
    o = o.transpose(1, 0, 3, 2, 4).reshape(B, S, RET_V_W)
    y = jax.nn.silu(gate.astype(jnp.float32)) * o
    return y.astype(h.dtype) @ w_out


def compress_blocks(raw, pos, w1, w2, blk_idx):
    B = raw.shape[0]
    n_cmp = blk_idx.shape[0]
    blocks = raw[:, blk_idx] + pos[None, None, :, None, :]
    flat = blocks.transpose(0, 1, 3, 2, 4).reshape(B, n_cmp, NSA_GROUPS, CMP_LEN * NSA_DH)
    return jax.nn.silu(flat @ w1) @ w2


def nsa_mixer(h, w_in, cmp_pos_k, cmp_w1_k, cmp_w2_k, cmp_pos_v, cmp_w1_v, cmp_w2_v, w_out):
    B, S, _ = h.shape
    G, R, dh, QB = NSA_GROUPS, NSA_REP, NSA_DH, Q_BLOCK
    nq = S // QB
    sizes = [NSA_WIDTH] + [NSA_KV] * 6 + [NSA_HEADS * NSA_BRANCHES]
    offs = [int(o) for o in np.cumsum(sizes)]
    proj = h @ w_in
    q, kc_raw, vc_raw, ks, vs, kw, vw, g, z = jnp.split(proj, offs, axis=-1)
    q = q.reshape(B, S, G, R, dh) * (dh ** -0.5)
    kv = lambda t: t.reshape(B, S, G, dh)
    kc_raw, vc_raw, ks, vs, kw, vw = map(kv, (kc_raw, vc_raw, ks, vs, kw, vw))
    gates = jax.nn.sigmoid(g.astype(jnp.float32)).reshape(B, S, G, R, NSA_BRANCHES)
    slopes = alibi_slopes(NSA_HEADS).reshape(G, R)

    n_cmp = (S - CMP_LEN) // CMP_STRIDE + 1
    blk_idx = jnp.arange(n_cmp)[:, None] * CMP_STRIDE + jnp.arange(CMP_LEN)[None, :]
    k_cmp = compress_blocks(kc_raw, cmp_pos_k, cmp_w1_k, cmp_w2_k, blk_idx)
    v_cmp = compress_blocks(vc_raw, cmp_pos_v, cmp_w1_v, cmp_w2_v, blk_idx)
    cmp_end = blk_idx[:, -1]

    n_slc = S // SLC_BLOCK
    n_sel = min(TOP_N, n_slc)
    cstart = jnp.arange(n_cmp) * CMP_STRIDE
    sstart = jnp.arange(n_slc) * SLC_BLOCK
    overlap = ((cstart[:, None] < sstart[None, :] + SLC_BLOCK)
               & (cstart[:, None] + CMP_LEN > sstart[None, :])).astype(jnp.float32)
    k_blk = ks.reshape(B, n_slc, SLC_BLOCK, G, dh).transpose(0, 3, 1, 2, 4)
    v_blk = vs.reshape(B, n_slc, SLC_BLOCK, G, dh).transpose(0, 3, 1, 2, 4)
    b_i = jnp.arange(B)[:, None, None, None]
    g_i = jnp.arange(G)[None, None, :, None]
    jj = jnp.arange(n_slc)

    kw_p = jnp.pad(kw, ((0, 0), (WINDOW, 0), (0, 0), (0, 0)))
    vw_p = jnp.pad(vw, ((0, 0), (WINDOW, 0), (0, 0), (0, 0)))

    q_blocks = q.reshape(B, nq, QB, G, R, dh).transpose(1, 0, 2, 3, 4, 5)
    g_blocks = gates.reshape(B, nq, QB, G, R, NSA_BRANCHES).transpose(1, 0, 2, 3, 4, 5)

    def block_fn(args):
        qi, qb, gb = args
        q0 = qi * QB
        t = q0 + jnp.arange(QB)
        dc = t[:, None] - cmp_end[None, :]
        sc = (jnp.einsum('bqgrd,bcgd->bqgrc', qb, k_cmp).astype(jnp.float32)
              - slopes[None, None, :, :, None] * dc.astype(jnp.float32)[None, :, None, None, :])
        pc = masked_softmax(sc, (dc >= 0)[None, :, None, None, :])
        o_cmp = jnp.einsum('bqgrc,bcgd->bqgrd', pc, v_cmp)
        imp = jnp.einsum('bqgrc,cj->bqgj', pc, overlap)
        cur = (t // SLC_BLOCK)[:, None]
        forced = (jj[None, :] == 0) | (jj[None, :] == cur) | (jj[None, :] == cur - 1)
        blk_ok = sstart[None, :] <= t[:, None]
        score = jnp.where(forced[None, :, None, :], FORCE_SCORE,
                          jnp.where(blk_ok[None, :, None, :], imp, -1.0))
        _, sel = lax.top_k(score, n_sel)
        kb = k_blk[b_i, g_i, sel]
        vb = v_blk[b_i, g_i, sel]
        tok = sel[..., None] * SLC_BLOCK + jnp.arange(SLC_BLOCK)
        ds = (t[None, :, None, None, None] - tok)[:, :, :, None]
        ss = (jnp.einsum('bqgrd,bqgnkd->bqgrnk', qb, kb).astype(jnp.float32)
              - slopes[None, None, :, :, None, None] * ds.astype(jnp.float32))
        flat_shape = ss.shape[:4] + (n_sel * SLC_BLOCK,)
        ps = masked_softmax(ss.reshape(flat_shape),
                            jnp.broadcast_to(ds >= 0, ss.shape).reshape(flat_shape)).reshape(ss.shape)
        o_slc = jnp.einsum('bqgrnk,bqgnkd->bqgrd', ps, vb)
        kwb = lax.dynamic_slice_in_dim(kw_p, q0, QB + WINDOW, axis=1)
        vwb = lax.dynamic_slice_in_dim(vw_p, q0, QB + WINDOW, axis=1)
        kpos = q0 - WINDOW + jnp.arange(QB + WINDOW)
        dw = t[:, None] - kpos[None, :]
        mw = (dw >= 0) & (dw < WINDOW) & (kpos[None, :] >= 0)
        sw = (jnp.einsum('bqgrd,bkgd->bqgrk', qb, kwb).astype(jnp.float32)
              - slopes[None, None, :, :, None] * dw.astype(jnp.float32)[None, :, None, None, :])
        pw = masked_softmax(sw, mw[None, :, None, None, :])
        o_win = jnp.einsum('bqgrk,bkgd->bqgrd', pw, vwb)
        return gb[..., 0:1] * o_cmp + gb[..., 1:2] * o_slc + gb[..., 2:3] * o_win

    o = lax.map(block_fn, (jnp.arange(nq), q_blocks, g_blocks))
    o = o.transpose(1, 0, 2, 3, 4, 5).reshape(B, S, NSA_WIDTH)
    y = jax.nn.silu(z.astype(jnp.float32)) * o.astype(jnp.float32)
    return y.astype(h.dtype) @ w_out


def setup_inputs(seed: int = 0) -> dict:
    key = jax.random.key(seed)
    ks = jax.random.split(key, 14)
    nrm = lambda k, shape, scale: jax.random.normal(k, shape, jnp.float32) * scale
    return {
        "x": nrm(ks[0], (BATCH, SEQ, D_MODEL), 1.0),
        "norm_g": 1.0 + nrm(ks[1], (DEPTH, D_MODEL), 0.01),
        "ret_w_in": nrm(ks[2], (N_RET_LAYERS, D_MODEL, RET_IN), D_MODEL ** -0.5),
        "ret_w_out": nrm(ks[3], (N_RET_LAYERS, RET_V_W, D_MODEL), RET_V_W ** -0.5),
        "nsa_w_in": nrm(ks[4], (N_NSA_LAYERS, D_MODEL, NSA_IN), D_MODEL ** -0.5),
        "nsa_cmp_pos_k": nrm(ks[5], (N_NSA_LAYERS, CMP_LEN, NSA_DH), 0.1),
        "nsa_cmp_w1_k": nrm(ks[6], (N_NSA_LAYERS, CMP_LEN * NSA_DH, NSA_DH), (CMP_LEN * NSA_DH) ** -0.5),
        "nsa_cmp_w2_k": nrm(ks[7], (N_NSA_LAYERS, NSA_DH, NSA_DH), NSA_DH ** -0.5),
        "nsa_cmp_pos_v": nrm(ks[8], (N_NSA_LAYERS, CMP_LEN, NSA_DH), 0.1),
        "nsa_cmp_w1_v": nrm(ks[9], (N_NSA_LAYERS, CMP_LEN * NSA_DH, NSA_DH), (CMP_LEN * NSA_DH) ** -0.5),
        "nsa_cmp_w2_v": nrm(ks[10], (N_NSA_LAYERS, NSA_DH, NSA_DH), NSA_DH ** -0.5),
        "nsa_w_out": nrm(ks[11], (N_NSA_LAYERS, NSA_WIDTH, D_MODEL), NSA_WIDTH ** -0.5),
        "final_g": 1.0 + nrm(ks[12], (D_MODEL,), 0.01),
    }


def reference(x, norm_g, ret_w_in, ret_w_out, nsa_w_in, nsa_cmp_pos_k, nsa_cmp_w1_k, nsa_cmp_w2_k,
              nsa_cmp_pos_v, nsa_cmp_w1_v, nsa_cmp_w2_v, nsa_w_out, final_g):
    h = x
    for i in range(DEPTH):
        hn = rmsnorm(h, norm_g[i])
        j = i // N_MIXERS
        if i % N_MIXERS == 0:
            h = h + retention_mixer(hn, ret_w_in[j], ret_w_out[j])
        else:
            h = h + nsa_mixer(hn, nsa_w_in[j], nsa_cmp_pos_k[j], nsa_cmp_w1_k[j], nsa_cmp_w2_k[j],
                              nsa_cmp_pos_v[j], nsa_cmp_w1_v[j], nsa_cmp_w2_v[j], nsa_w_out[j])
    return rmsnorm(h, final_g)
```

```python
import functools

import jax
import jax.numpy as jnp
import numpy as np
from jax import lax
from jax.experimental import pallas as pl
from jax.experimental.pallas import tpu as pltpu

F32 = jnp.float32
BF16 = jnp.bfloat16

EPS = 1e-6
MASKED = -2e30
M_FLOOR = -1e30
FORCE_SCORE = 1e4

RET_HEADS = 8
RET_DK = 256
RET_DV = 512
RET_CHUNK = 256

NSA_HEADS = 16
NSA_GROUPS = 4
NSA_REP = 4
NSA_DH = 128
CMP_LEN = 32
CMP_STRIDE = 16
SLC_BLOCK = 64
TOP_N = 16
WINDOW = 512
Q_BLOCK = 128
SEL_CHUNK = 256

VMEM_LIMIT = 56 * 1024 * 1024

_NT = (((1,), (1,)), ((), ()))
_TN = (((0,), (0,)), ((), ()))


def _sigmoid(x):
    return 1.0 / (1.0 + jnp.exp(-x))


def _iota(shape, dim):
    return lax.broadcasted_iota(jnp.int32, shape, dim)


def _norm_proj_kernel(x_ref, g_ref, w_ref, s_ref, o_ref, xn_ref):
    @pl.when(pl.program_id(1) == 0)
    def _():
        x = x_ref[...]
        ms = jnp.mean(x * x, axis=-1, keepdims=True)
        xn_ref[...] = (x * lax.rsqrt(ms + EPS) * g_ref[...]).astype(xn_ref.dtype)

    acc = jnp.dot(xn_ref[...], w_ref[...], preferred_element_type=F32)
    o_ref[...] = (acc * s_ref[...]).astype(o_ref.dtype)


def _norm_proj(x2, g, w, col_scale, out_dtype, tm, tn):
    m, k = x2.shape
    n = w.shape[1]
    return pl.pallas_call(
        _norm_proj_kernel,
        grid=(m // tm, n // tn),
        in_specs=[
            pl.BlockSpec((tm, k), lambda i, j: (i, 0)),
            pl.BlockSpec((1, k), lambda i, j: (0, 0)),
            pl.BlockSpec((k, tn), lambda i, j: (0, j)),
            pl.BlockSpec((1, tn), lambda i, j: (0, j)),
        ],
        out_specs=pl.BlockSpec((tm, tn), lambda i, j: (i, j)),
        out_shape=jax.ShapeDtypeStruct((m, n), out_dtype),
        scratch_shapes=[pltpu.VMEM((tm, k), BF16)],
        compiler_params=pltpu.CompilerParams(
            dimension_semantics=("arbitrary", "arbitrary"), vmem_limit_bytes=VMEM_LIMIT),
    )(x2, g.reshape(1, k), w, col_scale.reshape(1, n))


def _out_proj_kernel(y_ref, w_ref, res_ref, g_ref, o_ref, acc_ref, *, final_norm):
    kk = pl.program_id(1)

    @pl.when(kk == 0)
    def _():
        acc_ref[...] = jnp.zeros_like(acc_ref)

    acc_ref[...] += jnp.dot(y_ref[...], w_ref[...], preferred_element_type=F32)

    @pl.when(kk == pl.num_programs(1) - 1)
    def _():
        h = res_ref[...] + acc_ref[...]
        if final_norm:
            ms = jnp.mean(h * h, axis=-1, keepdims=True)
            h = h * lax.rsqrt(ms + EPS) * g_ref[...]
        o_ref[...] = h


def _out_proj(y2, w, res2, g, final_norm, tm=512, tk=1024):
    m, kd = y2.shape
    n = w.shape[1]
    return pl.pallas_call(
        functools.partial(_out_proj_kernel, final_norm=final_norm),
        grid=(m // tm, kd // tk),
        in_specs=[
            pl.BlockSpec((tm, tk), lambda i, k: (i, k)),
            pl.BlockSpec((tk, n), lambda i, k: (k, 0)),
            pl.BlockSpec((tm, n), lambda i, k: (i, 0)),
            pl.BlockSpec((1, n), lambda i, k: (0, 0)),
        ],
        out_specs=pl.BlockSpec((tm, n), lambda i, k: (i, 0)),
        out_shape=jax.ShapeDtypeStruct((m, n), F32),
        scratch_shapes=[pltpu.VMEM((tm, n), F32)],
        compiler_params=pltpu.CompilerParams(
            dimension_semantics=("arbitrary", "arbitrary"), vmem_limit_bytes=VMEM_LIMIT),
    )(y2, w, res2, g.reshape(1, n))


def _retention_kernel(q_ref, k_ref, v_ref, gate_ref, di_ref, qd_ref, kd_ref, cd_ref, y_ref,
                      state_ref, *, chunk, n_sub):
    @pl.when(pl.program_id(2) == 0)
    def _():
        state_ref[...] = jnp.zeros_like(state_ref)

    di = di_ref[0]
    qd = qd_ref[0]
    kd = kd_ref[0]
    cd = cd_ref[0]
    for i in range(n_sub):
        rows = pl.ds(i * chunk, chunk)
        q = q_ref[0, rows, :]
        k = k_ref[0, rows, :]
        v = v_ref[0, rows, :]
        state = state_ref[...]
        s = lax.dot_general(q, k, _NT, preferred_element_type=F32) * di
        o = jnp.dot(s.astype(BF16), v, preferred_element_type=F32)
        o = o + jnp.dot(q, state.astype(BF16), preferred_element_type=F32) * qd
        k_dec = (k.astype(F32) * kd).astype(BF16)
        state_ref[...] = state * cd + lax.dot_general(k_dec, v, _TN, preferred_element_type=F32)
        mu = jnp.mean(o, axis=-1, keepdims=True)
        oc = o - mu
        var = jnp.mean(oc * oc, axis=-1, keepdims=True)
        on = oc * lax.rsqrt(var + EPS)
        gt = gate_ref[0, rows, :]
        y_ref[0, rows, :] = (gt * _sigmoid(gt) * on).astype(y_ref.dtype)


def _retention(qkv, gate, tokens_per_step=1024):
    b, s, _ = qkv.shape
    h, c = RET_HEADS, RET_CHUNK
    t = min(tokens_per_step, s)
    log_g = jnp.log1p(-jnp.exp2(-5.0 - jnp.arange(h, dtype=F32)))
    idx = jnp.arange(c, dtype=F32)
    diff = idx[:, None] - idx[None, :]
    decay_intra = jnp.where(diff >= 0, jnp.exp(diff[None] * log_g[:, None, None]), 0.0)
    q_decay = jnp.exp((idx[None, :] + 1.0) * log_g[:, None])[:, :, None]
    k_decay = jnp.exp((c - 1.0 - idx[None, :]) * log_g[:, None])[:, :, None]
    chunk_decay = jnp.exp(c * log_g)[:, None, None]
    nk = RET_HEADS * RET_DK // RET_DK
    nv = 2 * RET_HEADS * RET_DK // RET_DV
    return pl.pallas_call(
        functools.partial(_retention_kernel, chunk=c, n_sub=t // c),
        grid=(b, h, s // t),
        in_specs=[
            pl.BlockSpec((1, t, RET_DK), lambda bi, hi, ti: (bi, ti, hi)),
            pl.BlockSpec((1, t, RET_DK), lambda bi, hi, ti: (bi, ti, nk + hi)),
            pl.BlockSpec((1, t, RET_DV), lambda bi, hi, ti: (bi, ti, nv + hi)),
            pl.BlockSpec((1, t, RET_DV), lambda bi, hi, ti: (bi, ti, hi)),
            pl.BlockSpec((1, c, c), lambda bi, hi, ti: (hi, 0, 0)),
            pl.BlockSpec((1, c, 1), lambda bi, hi, ti: (hi, 0, 0)),
            pl.BlockSpec((1, c, 1), lambda bi, hi, ti: (hi, 0, 0)),
            pl.BlockSpec((1, 1, 1), lambda bi, hi, ti: (hi, 0, 0)),
        ],
        out_specs=pl.BlockSpec((1, t, RET_DV), lambda bi, hi, ti: (bi, ti, hi)),
        out_shape=jax.ShapeDtypeStruct((b, s, h * RET_DV), BF16),
        scratch_shapes=[pltpu.VMEM((RET_DK, RET_DV), F32)],
        compiler_params=pltpu.CompilerParams(
            dimension_semantics=("arbitrary", "arbitrary", "arbitrary"),
            vmem_limit_bytes=VMEM_LIMIT),
    )(qkv, qkv, qkv, gate, decay_intra, q_decay, k_decay, chunk_decay)


def _compress_kernel(x_ref, xn_ref, pos_ref, w1_ref, w2_ref, o_ref):
    half = CMP_LEN // 2
    width = NSA_GROUPS * NSA_DH
    for g in range(NSA_GROUPS):
        acc = jnp.zeros((x_ref.shape[1], NSA_DH), F32)
        for l in range(half):
            cols = slice(l * width + g * NSA_DH, l * width + (g + 1) * NSA_DH)
            xa = (x_ref[0, :, cols].astype(F32) + pos_ref[l:l + 1, :]).astype(BF16)
            acc += jnp.dot(xa, w1_ref[l * NSA_DH:(l + 1) * NSA_DH, :], preferred_element_type=F32)
            lb = half + l
            xb = (xn_ref[0, :, cols].astype(F32) + pos_ref[lb:lb + 1, :]).astype(BF16)
            acc += jnp.dot(xb, w1_ref[lb * NSA_DH:(lb + 1) * NSA_DH, :], preferred_element_type=F32)
        hid = (acc * _sigmoid(acc)).astype(BF16)
        out = jnp.dot(hid, w2_ref[...], preferred_element_type=F32)
        o_ref[0, :, g * NSA_DH:(g + 1) * NSA_DH] = out.astype(o_ref.dtype)


def _compress(raw, pos, w1, w2):
    b, s, width = raw.shape
    rows = s // CMP_STRIDE
    x = raw.reshape(b, rows, CMP_STRIDE * width)
    xn = jnp.concatenate([x[:, 1:], jnp.zeros_like(x[:, :1])], axis=1)
    tr = min(256, rows)
    blk = pl.BlockSpec((1, tr, CMP_STRIDE * width), lambda bi, ri: (bi, ri, 0))
    return pl.pallas_call(
        _compress_kernel,
        grid=(b, rows // tr),
        in_specs=[
            blk, blk,
            pl.BlockSpec((CMP_LEN, NSA_DH), lambda bi, ri: (0, 0)),
            pl.BlockSpec((CMP_LEN * NSA_DH, NSA_DH), lambda bi, ri: (0, 0)),
            pl.BlockSpec((NSA_DH, NSA_DH), lambda bi, ri: (0, 0)),
        ],
        out_specs=pl.BlockSpec((1, tr, width), lambda bi, ri: (bi, ri, 0)),
        out_shape=jax.ShapeDtypeStruct((b, rows, width), BF16),
        compiler_params=pltpu.CompilerParams(
            dimension_semantics=("arbitrary", "arbitrary"), vmem_limit_bytes=VMEM_LIMIT),
    )(x, xn, pos, w1.astype(BF16), w2.astype(BF16))


def _stack_heads(qblk):
    return jnp.concatenate(
        [qblk[:, r * NSA_DH:(r + 1) * NSA_DH] for r in range(NSA_REP)], axis=0)


def _cmp_select_kernel(slopes_ref, q_ref, kc_ref, vc_ref, ovt_ref, ocmp_ref, sel_ref, flag_ref):
    g = pl.program_id(1)
    q0 = pl.program_id(2) * Q_BLOCK
    qb = Q_BLOCK
    n_cmp = kc_ref.shape[1]
    n_slc = ovt_ref.shape[0]

    q = _stack_heads(q_ref[0])
    s_all = lax.dot_general(q, kc_ref[0], _NT, preferred_element_type=F32)
    vc = vc_ref[0]

    t = q0 + _iota((qb, n_cmp), 0)
    dc = t - (_iota((qb, n_cmp), 1) * CMP_STRIDE + (CMP_LEN - 1))
    valid = dc >= 0
    dist = dc.astype(F32)

    psum = jnp.zeros((qb, n_cmp), F32)
    for r in range(NSA_REP):
        slope = slopes_ref[g * NSA_REP + r]
        s = jnp.where(valid, s_all[r * qb:(r + 1) * qb] - slope * dist, M_FLOOR)
        m = jnp.max(s, axis=-1, keepdims=True)
        p = jnp.where(valid, jnp.exp(s - m), 0.0)
        l = jnp.sum(p, axis=-1, keepdims=True)
        p = p * jnp.where(l > 0.0, 1.0 / l, 0.0)
        ocmp_ref[0, :, r * NSA_DH:(r + 1) * NSA_DH] = jnp.dot(
            p.astype(BF16), vc, preferred_element_type=F32)
        psum = psum + p

    imp_t = lax.dot_general(ovt_ref[...], psum.astype(BF16), _NT, preferred_element_type=F32)
    jrow = _iota((n_slc, qb), 0)
    cur = lax.shift_right_logical(q0 + _iota((n_slc, qb), 1), int(np.log2(SLC_BLOCK)))
    forced = (jrow == 0) | (jrow == cur) | (jrow == cur - 1)
    score = jnp.where(forced, FORCE_SCORE, jnp.where(jrow <= cur, imp_t, -1.0))
    chosen = jnp.zeros((n_slc, qb), F32)
    for _ in range(min(TOP_N, n_slc)):
        best = jnp.max(score, axis=0, keepdims=True)
        first = jnp.min(jnp.where(score == best, jrow, n_slc), axis=0, keepdims=True)
        hit = jrow == first
        chosen = jnp.where(hit, 1.0, chosen)
        score = jnp.where(hit, -3e38, score)
    sel = chosen.T
    sel_ref[0, 0] = sel.astype(sel_ref.dtype)
    flag_ref[0, 0, 0] = jnp.max(sel, axis=0, keepdims=True)


def _cmp_select(pa, k_cmp, v_cmp, slopes):
    b, s, _ = pa.shape
    nq = s // Q_BLOCK
    n_cmp = k_cmp.shape[1]
    n_slc = s // SLC_BLOCK
    cstart = np.arange(n_cmp)[None, :] * CMP_STRIDE
    sstart = np.arange(n_slc)[:, None] * SLC_BLOCK
    real = np.arange(n_cmp)[None, :] < (s - CMP_LEN) // CMP_STRIDE + 1
    ovt = ((cstart < sstart + SLC_BLOCK) & (cstart + CMP_LEN > sstart) & real)
    ovt = jnp.asarray(ovt.astype(np.float32), BF16)
    gw = NSA_REP * NSA_DH
    return pl.pallas_call(
        _cmp_select_kernel,
        grid=(b, NSA_GROUPS, nq),
        in_specs=[
            pl.BlockSpec(memory_space=pltpu.SMEM),
            pl.BlockSpec((1, Q_BLOCK, gw), lambda bi, gi, qi: (bi, qi, gi)),
            pl.BlockSpec((1, n_cmp, NSA_DH), lambda bi, gi, qi: (bi, 0, gi)),
            pl.BlockSpec((1, n_cmp, NSA_DH), lambda bi, gi, qi: (bi, 0, gi)),
            pl.BlockSpec((n_slc, n_cmp), lambda bi, gi, qi: (0, 0)),
        ],
        out_specs=[
            pl.BlockSpec((1, Q_BLOCK, gw), lambda bi, gi, qi: (bi, qi, gi)),
            pl.BlockSpec((1, 1, Q_BLOCK, n_slc), lambda bi, gi, qi: (bi, gi, qi, 0)),
            pl.BlockSpec((1, 1, 1, 1, n_slc), lambda bi, gi, qi: (bi, gi, qi, 0, 0)),
        ],
        out_shape=[
            jax.ShapeDtypeStruct((b, s, NSA_GROUPS * gw), F32),
            jax.ShapeDtypeStruct((b, NSA_GROUPS, s, n_slc), BF16),
            jax.ShapeDtypeStruct((b, NSA_GROUPS, nq, 1, n_slc), F32),
        ],
        compiler_params=pltpu.CompilerParams(
            dimension_semantics=("arbitrary", "arbitrary", "arbitrary"),
            vmem_limit_bytes=VMEM_LIMIT),
    )(slopes, pa, k_cmp, v_cmp, ovt)


def _flash_reset(m_ref, l_ref, acc_ref):
    m_ref[...] = jnp.full_like(m_ref, M_FLOOR)
    l_ref[...] = jnp.zeros_like(l_ref)
    acc_ref[...] = jnp.zeros_like(acc_ref)


def _flash_update(r, s, mask, v, m_ref, l_ref, acc_ref):
    rows = pl.ds(r * Q_BLOCK, Q_BLOCK)
    s = jnp.where(mask, s, MASKED)
    m_old = m_ref[rows, :]
    m_new = jnp.maximum(m_old, jnp.max(s, axis=-1, keepdims=True))
    alpha = jnp.exp(m_old - m_new)
    p = jnp.exp(s - m_new)
    l_ref[rows, :] = alpha * l_ref[rows, :] + jnp.sum(p, axis=-1, keepdims=True)
    acc_ref[rows, :] = alpha * acc_ref[rows, :] + jnp.dot(
        p.astype(BF16), v, preferred_element_type=F32)
    m_ref[rows, :] = m_new


def _flash_result(r, l_ref, acc_ref):
    rows = pl.ds(r * Q_BLOCK, Q_BLOCK)
    l = l_ref[rows, :]
    return acc_ref[rows, :] * jnp.where(l > 0.0, 1.0 / l, 0.0)


def _sel_win_kernel(words_ref, slopes_ref, q_ref, ks_ref, vs_ref, kw_ref, vw_ref, sel_ref,
                    ocmp_ref, gate_ref, z_ref, y_ref, m_ref, l_ref, acc_ref, comb_ref, *,
                    words_per_q):
    b = pl.program_id(0)
    g = pl.program_id(1)
    qi = pl.program_id(2)
    nq = pl.num_programs(2)
    qb = Q_BLOCK
    q0 = qi * qb
    n_slc = sel_ref.shape[3]
    blocks_per_chunk = SEL_CHUNK // SLC_BLOCK

    q = _stack_heads(q_ref[0])
    sel = sel_ref[0, 0]
    slopes = [slopes_ref[g * NSA_REP + r] for r in range(NSA_REP)]
    gates = _sigmoid(gate_ref[0, 0])

    _flash_reset(m_ref, l_ref, acc_ref)
    word_base = ((b * NSA_GROUPS + g) * nq + qi) * words_per_q

    def chunk_step(c, carry):
        word = words_ref[word_base + lax.shift_right_logical(c, 5)]
        active = lax.shift_right_logical(word, c & 31) & 1

        @pl.when(active != 0)
        def _():
            off = pl.multiple_of(c * SEL_CHUNK, SEL_CHUNK)
            k = ks_ref[0, pl.ds(off, SEL_CHUNK), :]
            v = vs_ref[0, pl.ds(off, SEL_CHUNK), :]
            s_all = lax.dot_general(q, k, _NT, preferred_element_type=F32)
            blk_of_key = c * blocks_per_chunk + lax.shift_right_logical(
                _iota((n_slc, SEL_CHUNK), 1), int(np.log2(SLC_BLOCK)))
            expand = jnp.where(_iota((n_slc, SEL_CHUNK), 0) == blk_of_key, 1.0, 0.0).astype(BF16)
            chosen = jnp.dot(sel, expand, preferred_element_type=F32)
            d = (q0 + _iota((qb, SEL_CHUNK), 0)) - (off + _iota((qb, SEL_CHUNK), 1))
            mask = (chosen > 0.5) & (d >= 0)
            dist = d.astype(F32)
            for r in range(NSA_REP):
                _flash_update(r, s_all[r * qb:(r + 1) * qb] - slopes[r] * dist, mask, v,
                              m_ref, l_ref, acc_ref)

        return carry

    lax.fori_loop(0, lax.shift_right_logical(q0, int(np.log2(SEL_CHUNK))) + 1, chunk_step, 0)

    for r in range(NSA_REP):
        cols = slice(r * NSA_DH, (r + 1) * NSA_DH)
        g_cmp = gates[:, 3 * r:3 * r + 1]
        g_slc = gates[:, 3 * r + 1:3 * r + 2]
        comb_ref[pl.ds(r * qb, qb), :] = (g_cmp * ocmp_ref[0, :, cols]
                                          + g_slc * _flash_result(r, l_ref, acc_ref))

    _flash_reset(m_ref, l_ref, acc_ref)
    for i in range(WINDOW // qb + 1):
        kb = qi - WINDOW // qb + i

        @pl.when(kb >= 0)
        def _():
            off = pl.multiple_of(kb * qb, qb)
            k = kw_ref[0, pl.ds(off, qb), :]
            v = vw_ref[0, pl.ds(off, qb), :]
            s_all = lax.dot_general(q, k, _NT, preferred_element_type=F32)
            d = (q0 + _iota((qb, qb), 0)) - (off + _iota((qb, qb), 1))
            mask = (d >= 0) & (d < WINDOW)
            dist = d.astype(F32)
            for r in range(NSA_REP):
                _flash_update(r, s_all[r * qb:(r + 1) * qb] - slopes[r] * dist, mask, v,
                              m_ref, l_ref, acc_ref)

    for r in range(NSA_REP):
        cols = slice(r * NSA_DH, (r + 1) * NSA_DH)
        g_win = gates[:, 3 * r + 2:3 * r + 3]
        o = comb_ref[pl.ds(r * qb, qb), :] + g_win * _flash_result(r, l_ref, acc_ref)
        z = z_ref[0, :, cols]
        y_ref[0, :, cols] = (z * _sigmoid(z) * o).astype(y_ref.dtype)


def _sel_win(words, slopes, pa, sel, ocmp, gates, pb):
    b, s, _ = pa.shape
    nq = s // Q_BLOCK
    n_slc = sel.shape[3]
    gw = NSA_REP * NSA_DH
    q_cols = NSA_HEADS * NSA_DH // NSA_DH
    kv = lambda slot: pl.BlockSpec(
        (1, s, NSA_DH), lambda bi, gi, qi, *_: (bi, 0, q_cols + slot * NSA_GROUPS + gi))
    rows = NSA_REP * Q_BLOCK
    grid_spec = pltpu.PrefetchScalarGridSpec(
        num_scalar_prefetch=2,
        grid=(b, NSA_GROUPS, nq),
        in_specs=[
            pl.BlockSpec((1, Q_BLOCK, gw), lambda bi, gi, qi, *_: (bi, qi, gi)),
            kv(2), kv(3), kv(4), kv(5),
            pl.BlockSpec((1, 1, Q_BLOCK, n_slc), lambda bi, gi, qi, *_: (bi, gi, qi, 0)),
            pl.BlockSpec((1, Q_BLOCK, gw), lambda bi, gi, qi, *_: (bi, qi, gi)),
            pl.BlockSpec((1, 1, Q_BLOCK, NSA_REP * 3), lambda bi, gi, qi, *_: (bi, gi, qi, 0)),
            pl.BlockSpec((1, Q_BLOCK, gw), lambda bi, gi, qi, *_: (bi, qi, gi)),
        ],
        out_specs=pl.BlockSpec((1, Q_BLOCK, gw), lambda bi, gi, qi, *_: (bi, qi, gi)),
        scratch_shapes=[
            pltpu.VMEM((rows, 1), F32),
            pltpu.VMEM((rows, 1), F32),
            pltpu.VMEM((rows, NSA_DH), F32),
            pltpu.VMEM((rows, NSA_DH), F32),
        ],
    )
    return pl.pallas_call(
        functools.partial(_sel_win_kernel, words_per_q=words.shape[0] // (b * NSA_GROUPS * nq)),
        grid_spec=grid_spec,
        out_shape=jax.ShapeDtypeStruct((b, s, NSA_GROUPS * gw), BF16),
        compiler_params=pltpu.CompilerParams(
            dimension_semantics=("arbitrary", "arbitrary", "arbitrary"),
            vmem_limit_bytes=VMEM_LIMIT),
    )(words, slopes, pa, pa, pa, pa, pa, sel, ocmp, gates, pb)


def _pack_chunk_flags(flags):
    b, g, nq, _, n_slc = flags.shape
    per_chunk = SEL_CHUNK // SLC_BLOCK
    n_chunks = max(n_slc // per_chunk, 1)
    chunk_any = flags.reshape(b, g, nq, n_chunks, -1).max(axis=-1) > 0.5
    n_words = -(-n_chunks // 32)
    pad = n_words * 32 - n_chunks
    bits = jnp.pad(chunk_any, ((0, 0), (0, 0), (0, 0), (0, pad))).astype(jnp.uint32)
    bits = bits.reshape(b, g, nq, n_words, 32) << jnp.arange(32, dtype=jnp.uint32)
    words = bits.sum(axis=-1, dtype=jnp.uint32)
    return lax.bitcast_convert_type(words, jnp.int32).reshape(-1)


def kernel(x, norm_g, ret_w_in, ret_w_out, nsa_w_in, nsa_cmp_pos_k, nsa_cmp_w1_k, nsa_cmp_w2_k,
           nsa_cmp_pos_v, nsa_cmp_w1_v, nsa_cmp_w2_v, nsa_w_out, final_g):
    b, s, d = x.shape
    m = b * s
    x2 = x.reshape(m, d)

    qk_w = 2 * RET_HEADS * RET_DK
    v_w = RET_HEADS * RET_DV
    w_in = ret_w_in[0].astype(BF16)
    scale = jnp.concatenate([jnp.ones((qk_w // 2,), F32), jnp.full((qk_w // 2,), RET_DK ** -0.5, F32),
                             jnp.ones((v_w,), F32)])
    qkv = _norm_proj(x2, norm_g[0], w_in[:, :qk_w + v_w], scale, BF16, tm=1024, tn=1024)
    gate = _norm_proj(x2, norm_g[0], w_in[:, qk_w + v_w:], jnp.ones((v_w,), F32), F32,
                      tm=1024, tn=1024)
    y = _retention(qkv.reshape(b, s, qk_w + v_w), gate.reshape(b, s, v_w))
    h1 = _out_proj(y.reshape(m, v_w), ret_w_out[0].astype(BF16), x2, final_g, final_norm=False)

    width = NSA_HEADS * NSA_DH
    kvw = NSA_GROUPS * NSA_DH
    n_gate = NSA_HEADS * 3
    qkv_w = width + 6 * kvw
    w_n = nsa_w_in[0]
    w_a = w_n[:, :qkv_w].astype(BF16)
    scale_a = jnp.concatenate([jnp.full((width,), NSA_DH ** -0.5, F32), jnp.ones((6 * kvw,), F32)])
    zg_w = width + 128
    w_b = jnp.concatenate([w_n[:, qkv_w + n_gate:], w_n[:, qkv_w:qkv_w + n_gate],
                           jnp.zeros((d, 128 - n_gate), F32)], axis=1).astype(BF16)
    pa = _norm_proj(h1, norm_g[1], w_a, scale_a, BF16, tm=1024, tn=1024).reshape(b, s, qkv_w)
    pb = _norm_proj(h1, norm_g[1], w_b, jnp.ones((zg_w,), F32), F32, tm=512, tn=zg_w)
    pb = pb.reshape(b, s, zg_w)

    k_cmp = _compress(pa[:, :, width:width + kvw], nsa_cmp_pos_k[0], nsa_cmp_w1_k[0],
                      nsa_cmp_w2_k[0])
    v_cmp = _compress(pa[:, :, width + kvw:width + 2 * kvw], nsa_cmp_pos_v[0], nsa_cmp_w1_v[0],
                      nsa_cmp_w2_v[0])
    slopes = jnp.exp2(-8.0 * jnp.arange(1, NSA_HEADS + 1, dtype=F32) / NSA_HEADS)
    ocmp, sel, flags = _cmp_select(pa, k_cmp, v_cmp, slopes)
    words = _pack_chunk_flags(flags)
    gates = pb[:, :, width:width + n_gate].reshape(b, s, NSA_GROUPS, NSA_REP * 3)
    gates = gates.transpose(0, 2, 1, 3)
    y = _sel_win(words, slopes, pa, sel, ocmp, gates, pb)
    out = _out_proj(y.reshape(m, width), nsa_w_out[0].astype(BF16), h1, final_g, final_norm=True)
    return out.reshape(b, s, d)
```

```python
import functools

import jax
import jax.numpy as jnp
import numpy as np
from jax import lax
from jax.experimental import pallas as pl
from jax.experimental.pallas import tpu as pltpu

F32 = jnp.float32
BF16 = jnp.bfloat16

EPS = 1e-6
MASKED = -2e30
M_FLOOR = -1e30
FORCE_SCORE = 1e4

RET_HEADS = 8
RET_DK = 256
RET_DV = 512
RET_CHUNK = 256

NSA_HEADS = 16
NSA_GROUPS = 4
NSA_REP = 4
NSA_DH = 128
CMP_LEN = 32
CMP_STRIDE = 16
SLC_BLOCK = 64
TOP_N = 16
WINDOW = 512
Q_BLOCK = 128
SEL_CHUNK = 256

VMEM_LIMIT = 56 * 1024 * 1024

_NT = (((1,), (1,)), ((), ()))
_TN = (((0,), (0,)), ((), ()))


def _sigmoid(x):
    return 1.0 / (1.0 + jnp.exp(-x))


def _iota(shape, dim):
    return lax.broadcasted_iota(jnp.int32, shape, dim)


def _norm_proj_kernel(x_ref, g_ref, w_ref, s_ref, o_ref, xn_ref):
    @pl.when(pl.program_id(1) == 0)
    def _():
        x = x_ref[...]
        ms = jnp.mean(x * x, axis=-1, keepdims=True)
        xn_ref[...] = (x * lax.rsqrt(ms + EPS) * g_ref[...]).astype(xn_ref.dtype)

    acc = jnp.dot(xn_ref[...], w_ref[...], preferred_element_type=F32)
    o_ref[...] = (acc * s_ref[...]).astype(o_ref.dtype)


def _norm_proj(x2, g, w, col_scale, out_dtype, tm, tn):
    m, k = x2.shape
    n = w.shape[1]
    return pl.pallas_call(
        _norm_proj_kernel,
        grid=(m // tm, n // tn),
        in_specs=[
            pl.BlockSpec((tm, k), lambda i, j: (i, 0)),
            pl.BlockSpec((1, k), lambda i, j: (0, 0)),
            pl.BlockSpec((k, tn), lambda i, j: (0, j)),
            pl.BlockSpec((1, tn), lambda i, j: (0, j)),
        ],
        out_specs=pl.BlockSpec((tm, tn), lambda i, j: (i, j)),
        out_shape=jax.ShapeDtypeStruct((m, n), out_dtype),
        scratch_shapes=[pltpu.VMEM((tm, k), BF16)],
        compiler_params=pltpu.CompilerParams(
            dimension_semantics=("arbitrary", "arbitrary"), vmem_limit_bytes=VMEM_LIMIT),
    )(x2, g.reshape(1, k), w, col_scale.reshape(1, n))


def _out_proj_kernel(y_ref, w_ref, res_ref, g_ref, o_ref, acc_ref, *, final_norm):
    kk = pl.program_id(1)

    @pl.when(kk == 0)
    def _():
        acc_ref[...] = jnp.zeros_like(acc_ref)

    acc_ref[...] += jnp.dot(y_ref[...], w_ref[...], preferred_element_type=F32)

    @pl.when(kk == pl.num_programs(1) - 1)
    def _():
        h = res_ref[...] + acc_ref[...]
        if final_norm:
            ms = jnp.mean(h * h, axis=-1, keepdims=True)
            h = h * lax.rsqrt(ms + EPS) * g_ref[...]
        o_ref[...] = h


def _out_proj(y2, w, res2, g, final_norm, tm=512, tk=1024):
    m, kd = y2.shape
    n = w.shape[1]
    return pl.pallas_call(
        functools.partial(_out_proj_kernel, final_norm=final_norm),
        grid=(m // tm, kd // tk),
        in_specs=[
            pl.BlockSpec((tm, tk), lambda i, k: (i, k)),
            pl.BlockSpec((tk, n), lambda i, k: (k, 0)),
            pl.BlockSpec((tm, n), lambda i, k: (i, 0)),
            pl.BlockSpec((1, n), lambda i, k: (0, 0)),
        ],
        out_specs=pl.BlockSpec((tm, n), lambda i, k: (i, 0)),
        out_shape=jax.ShapeDtypeStruct((m, n), F32),
        scratch_shapes=[pltpu.VMEM((tm, n), F32)],
        compiler_params=pltpu.CompilerParams(
            dimension_semantics=("arbitrary", "arbitrary"), vmem_limit_bytes=VMEM_LIMIT),
    )(y2, w, res2, g.reshape(1, n))


def _retention_kernel(q_ref, k_ref, v_ref, gate_ref, di_ref, qd_ref, kd_ref, cd_ref, y_ref,
                      state_ref, *, chunk, n_sub):
    @pl.when(pl.program_id(2) == 0)
    def _():
        state_ref[...] = jnp.zeros_like(state_ref)

    di = di_ref[0]
    qd = qd_ref[0]
    kd = kd_ref[0]
    cd = cd_ref[0]
    for i in range(n_sub):
        rows = pl.ds(i * chunk, chunk)
        q = q_ref[0, rows, :]
        k = k_ref[0, rows, :]
        v = v_ref[0, rows, :]
        state = state_ref[...]
        s = lax.dot_general(q, k, _NT, preferred_element_type=F32) * di
        o = jnp.dot(s.astype(BF16), v, preferred_element_type=F32)
        o = o + jnp.dot(q, state.astype(BF16), preferred_element_type=F32) * qd
        k_dec = (k.astype(F32) * kd).astype(BF16)
        state_ref[...] = state * cd + lax.dot_general(k_dec, v, _TN, preferred_element_type=F32)
        mu = jnp.mean(o, axis=-1, keepdims=True)
        oc = o - mu
        var = jnp.mean(oc * oc, axis=-1, keepdims=True)
        on = oc * lax.rsqrt(var + EPS)
        gt = gate_ref[0, rows, :]
        y_ref[0, rows, :] = (gt * _sigmoid(gt) * on).astype(y_ref.dtype)


def _retention(qkv, gate, tokens_per_step=1024):
    b, s, _ = qkv.shape
    h, c = RET_HEADS, RET_CHUNK
    t = min(tokens_per_step, s)
    log_g = jnp.log1p(-jnp.exp2(-5.0 - jnp.arange(h, dtype=F32)))
    idx = jnp.arange(c, dtype=F32)
    diff = idx[:, None] - idx[None, :]
    decay_intra = jnp.where(diff >= 0, jnp.exp(diff[None] * log_g[:, None, None]), 0.0)
    q_decay = jnp.exp((idx[None, :] + 1.0) * log_g[:, None])[:, :, None]
    k_decay = jnp.exp((c - 1.0 - idx[None, :]) * log_g[:, None])[:, :, None]
    chunk_decay = jnp.exp(c * log_g)[:, None, None]
    nk = RET_HEADS * RET_DK // RET_DK
    nv = 2 * RET_HEADS * RET_DK // RET_DV
    return pl.pallas_call(
        functools.partial(_retention_kernel, chunk=c, n_sub=t // c),
        grid=(b, h, s // t),
        in_specs=[
            pl.BlockSpec((1, t, RET_DK), lambda bi, hi, ti: (bi, ti, hi)),
            pl.BlockSpec((1, t, RET_DK), lambda bi, hi, ti: (bi, ti, nk + hi)),
            pl.BlockSpec((1, t, RET_DV), lambda bi, hi, ti: (bi, ti, nv + hi)),
            pl.BlockSpec((1, t, RET_DV), lambda bi, hi, ti: (bi, ti, hi)),
            pl.BlockSpec((1, c, c), lambda bi, hi, ti: (hi, 0, 0)),
            pl.BlockSpec((1, c, 1), lambda bi, hi, ti: (hi, 0, 0)),
            pl.BlockSpec((1, c, 1), lambda bi, hi, ti: (hi, 0, 0)),
            pl.BlockSpec((1, 1, 1), lambda bi, hi, ti: (hi, 0, 0)),
        ],
        out_specs=pl.BlockSpec((1, t, RET_DV), lambda bi, hi, ti: (bi, ti, hi)),
        out_shape=jax.ShapeDtypeStruct((b, s, h * RET_DV), BF16),
        scratch_shapes=[pltpu.VMEM((RET_DK, RET_DV), F32)],
        compiler_params=pltpu.CompilerParams(
            dimension_semantics=("arbitrary", "arbitrary", "arbitrary"),
            vmem_limit_bytes=VMEM_LIMIT),
    )(qkv, qkv, qkv, gate, decay_intra, q_decay, k_decay, chunk_decay)


def _compress_kernel(x_ref, xn_ref, pos_ref, w1_ref, w2_ref, o_ref):
    half = CMP_LEN // 2
    width = NSA_GROUPS * NSA_DH
    for g in range(NSA_GROUPS):
        acc = jnp.zeros((x_ref.shape[1], NSA_DH), F32)
        for l in range(half):
            cols = slice(l * width + g * NSA_DH, l * width + (g + 1) * NSA_DH)
            xa = (x_ref[0, :, cols].astype(F32) + pos_ref[l:l + 1, :]).astype(BF16)
            acc += jnp.dot(xa, w1_ref[l * NSA_DH:(l + 1) * NSA_DH, :], preferred_element_type=F32)
            lb = half + l
            xb = (xn_ref[0, :, cols].astype(F32) + pos_ref[lb:lb + 1, :]).astype(BF16)
            acc += jnp.dot(xb, w1_ref[lb * NSA_DH:(lb + 1) * NSA_DH, :], preferred_element_type=F32)
        hid = (acc * _sigmoid(acc)).astype(BF16)
        out = jnp.dot(hid, w2_ref[...], preferred_element_type=F32)
        o_ref[0, :, g * NSA_DH:(g + 1) * NSA_DH] = out.astype(o_ref.dtype)


def _compress(raw, pos, w1, w2):
    b, s, width = raw.shape
    rows = s // CMP_STRIDE
    x = raw.reshape(b, rows, CMP_STRIDE * width)
    xn = jnp.concatenate([x[:, 1:], jnp.zeros_like(x[:, :1])], axis=1)
    tr = min(256, rows)
    blk = pl.BlockSpec((1, tr, CMP_STRIDE * width), lambda bi, ri: (bi, ri, 0))
    return pl.pallas_call(
        _compress_kernel,
        grid=(b, rows // tr),
        in_specs=[
            blk, blk,
            pl.BlockSpec((CMP_LEN, NSA_DH), lambda bi, ri: (0, 0)),
            pl.BlockSpec((CMP_LEN * NSA_DH, NSA_DH), lambda bi, ri: (0, 0)),
            pl.BlockSpec((NSA_DH, NSA_DH), lambda bi, ri: (0, 0)),
        ],
        out_specs=pl.BlockSpec((1, tr, width), lambda bi, ri: (bi, ri, 0)),
        out_shape=jax.ShapeDtypeStruct((b, rows, width), BF16),
        compiler_params=pltpu.CompilerParams(
            dimension_semantics=("arbitrary", "arbitrary"), vmem_limit_bytes=VMEM_LIMIT),
    )(x, xn, pos, w1.astype(BF16), w2.astype(BF16))


def _stack_heads(qblk):
    return jnp.concatenate(
        [qblk[:, r * NSA_DH:(r + 1) * NSA_DH] for r in range(NSA_REP)], axis=0)


def _head_cols(r):
    return slice(r * Q_BLOCK, (r + 1) * Q_BLOCK)


def _cmp_select_kernel(slopes_ref, q_ref, kc_ref, vct_ref, ovt_ref, spread_ref, ocmp_ref, sel_ref,
                       cnt_ref, s_scr, o_scr, imp_scr, *, ck):
    g = pl.program_id(1)
    q0 = pl.program_id(2) * Q_BLOCK
    qb = Q_BLOCK
    n_slc = ovt_ref.shape[1]
    log_ck = int(np.log2(ck))

    q = _stack_heads(q_ref[0])
    slopes = [slopes_ref[g * NSA_REP + r] for r in range(NSA_REP)]
    last_c = lax.shift_right_logical(q0, int(np.log2(CMP_STRIDE))) + (qb - CMP_LEN) // CMP_STRIDE
    n_act = lax.shift_right_logical(last_c, log_ck) + 1
    rel = _iota((ck, qb), 1) - CMP_STRIDE * _iota((ck, qb), 0)

    def score_step(c, ms):
        off = pl.multiple_of(c * ck, ck)
        st = lax.dot_general(kc_ref[0, pl.ds(off, ck), :], q, _NT, preferred_element_type=F32)
        d = rel + (q0 - CMP_STRIDE * off - (CMP_LEN - 1))
        valid = d >= 0
        dist = d.astype(F32)
        new = []
        for r in range(NSA_REP):
            s = jnp.where(valid, st[:, _head_cols(r)] - slopes[r] * dist, MASKED)
            s_scr[pl.ds(off, ck), _head_cols(r)] = s
            new.append(jnp.maximum(ms[r], jnp.max(s, axis=0, keepdims=True)))
        return tuple(new)

    ms = lax.fori_loop(0, n_act, score_step,
                       tuple(jnp.full((1, qb), M_FLOOR, F32) for _ in range(NSA_REP)))
    m_all = jnp.concatenate(ms, axis=1)

    o_scr[...] = jnp.zeros_like(o_scr)
    imp_scr[...] = jnp.zeros_like(imp_scr)

    def prob_step(c, l):
        off = pl.multiple_of(c * ck, ck)
        p = jnp.exp(s_scr[pl.ds(off, ck), :] - m_all)
        pb = p.astype(BF16)
        o_scr[...] += jnp.dot(vct_ref[0, 0, c], pb, preferred_element_type=F32)
        imp_scr[...] += jnp.dot(ovt_ref[c], pb, preferred_element_type=F32)
        return l + jnp.sum(p, axis=0, keepdims=True)

    l = lax.fori_loop(0, n_act, prob_step, jnp.zeros((1, NSA_REP * qb), F32))
    inv = jnp.where(l > 0.0, 1.0 / l, 0.0)
    ocmp_ref[0, 0, 0] = o_scr[...] * inv
    imp_w = imp_scr[...] * inv
    imp_t = imp_w[:, _head_cols(0)]
    for r in range(1, NSA_REP):
        imp_t = imp_t + imp_w[:, _head_cols(r)]

    jrow = _iota((n_slc, qb), 0)
    cur = lax.shift_right_logical(q0 + _iota((n_slc, qb), 1), int(np.log2(SLC_BLOCK)))
    forced = (jrow == 0) | (jrow == cur) | (jrow == cur - 1)
    score = jnp.where(forced, FORCE_SCORE, jnp.where(jrow <= cur, imp_t, -1.0))
    chosen = jnp.zeros((n_slc, qb), F32)
    for _ in range(min(TOP_N, n_slc)):
        best = jnp.max(score, axis=0, keepdims=True)
        first = jnp.min(jnp.where(score == best, jrow, n_slc), axis=0, keepdims=True)
        hit = jrow == first
        chosen = jnp.where(hit, 1.0, chosen)
        score = jnp.where(hit, -3e38, score)
    chosen = chosen.astype(BF16)
    sel_ref[0, 0, 0] = jnp.dot(spread_ref[...], chosen, preferred_element_type=F32)
    cnt_ref[0, 0, 0] = lax.dot_general(jnp.ones((8, qb), BF16), chosen, _NT,
                                       preferred_element_type=F32)


def _cmp_select(pa, k_cmp, v_cmp, slopes):
    b, s, _ = pa.shape
    nq = s // Q_BLOCK
    n_cmp = k_cmp.shape[1]
    n_slc = s // SLC_BLOCK
    ck = min(256, n_cmp)
    n_ck = n_cmp // ck
    gw = NSA_REP * NSA_DH
    cstart = np.arange(n_cmp)[None, :] * CMP_STRIDE
    sstart = np.arange(n_slc)[:, None] * SLC_BLOCK
    real = np.arange(n_cmp)[None, :] < (s - CMP_LEN) // CMP_STRIDE + 1
    ovt = ((cstart < sstart + SLC_BLOCK) & (cstart + CMP_LEN > sstart) & real).astype(np.float32)
    ovt = jnp.asarray(ovt.reshape(n_slc, n_ck, ck).transpose(1, 0, 2), BF16)
    per_chunk = SEL_CHUNK // SLC_BLOCK
    n_sel_chunks = s // SEL_CHUNK
    spread = np.zeros((8 * n_sel_chunks, n_slc), np.float32)
    for j in range(n_slc):
        spread[8 * (j // per_chunk) + j % per_chunk, j] = 1.0
    spread = jnp.asarray(spread, BF16)
    vct = v_cmp.reshape(b, n_ck, ck, NSA_GROUPS, NSA_DH).transpose(0, 3, 1, 4, 2)
    return pl.pallas_call(
        functools.partial(_cmp_select_kernel, ck=ck),
        grid=(b, NSA_GROUPS, nq),
        in_specs=[
            pl.BlockSpec(memory_space=pltpu.SMEM),
            pl.BlockSpec((1, Q_BLOCK, gw), lambda bi, gi, qi: (bi, qi, gi)),
            pl.BlockSpec((1, n_cmp, NSA_DH), lambda bi, gi, qi: (bi, 0, gi)),
            pl.BlockSpec((1, 1, n_ck, NSA_DH, ck), lambda bi, gi, qi: (bi, gi, 0, 0, 0)),
            pl.BlockSpec((n_ck, n_slc, ck), lambda bi, gi, qi: (0, 0, 0)),
            pl.BlockSpec((8 * n_sel_chunks, n_slc), lambda bi, gi, qi: (0, 0)),
        ],
        out_specs=[
            pl.BlockSpec((1, 1, 1, NSA_DH, gw), lambda bi, gi, qi: (bi, gi, qi, 0, 0)),
            pl.BlockSpec((1, 1, 1, 8 * n_sel_chunks, Q_BLOCK), lambda bi, gi, qi: (bi, gi, qi, 0, 0)),
            pl.BlockSpec((1, 1, 1, 8, n_slc), lambda bi, gi, qi: (bi, gi, qi, 0, 0)),
        ],
        out_shape=[
            jax.ShapeDtypeStruct((b, NSA_GROUPS, nq, NSA_DH, gw), F32),
            jax.ShapeDtypeStruct((b, NSA_GROUPS, nq, 8 * n_sel_chunks, Q_BLOCK), F32),
            jax.ShapeDtypeStruct((b, NSA_GROUPS, nq, 8, n_slc), F32),
        ],
        scratch_shapes=[
            pltpu.VMEM((n_cmp, gw), F32),
            pltpu.VMEM((NSA_DH, gw), F32),
            pltpu.VMEM((n_slc, gw), F32),
        ],
        compiler_params=pltpu.CompilerParams(
            dimension_semantics=("arbitrary", "arbitrary", "arbitrary"),
            vmem_limit_bytes=VMEM_LIMIT),
    )(slopes, pa, k_cmp, vct, ovt, spread)


def _masked_scores(st, mask, dist, slopes):
    return [jnp.where(mask, st[:, _head_cols(r)] - slopes[r] * dist, MASKED)
            for r in range(NSA_REP)]


def _sel_win_kernel(words_ref, slopes_ref, q_ref, ks_ref, vst_ref, kw_ref, vwt_ref, sel_ref,
                    ocmp_ref, gate_ref, z_ref, y_ref, m_ref, l_ref, acc_ref, *, words_per_q):
    b = pl.program_id(0)
    g = pl.program_id(1)
    qi = pl.program_id(2)
    nq = pl.num_programs(2)
    qb = Q_BLOCK
    q0 = qi * qb
    per_chunk = SEL_CHUNK // SLC_BLOCK

    q = _stack_heads(q_ref[0])
    slopes = [slopes_ref[g * NSA_REP + r] for r in range(NSA_REP)]

    m_ref[...] = jnp.full_like(m_ref, M_FLOOR)
    l_ref[...] = jnp.zeros_like(l_ref)
    acc_ref[...] = jnp.zeros_like(acc_ref)
    word_base = ((b * NSA_GROUPS + g) * nq + qi) * words_per_q

    def chunk_step(c, carry):
        word = words_ref[word_base + lax.shift_right_logical(c, 5)]
        active = lax.shift_right_logical(word, c & 31) & 1

        @pl.when(active != 0)
        def _():
            off = pl.multiple_of(c * SEL_CHUNK, SEL_CHUNK)
            st = lax.dot_general(ks_ref[0, pl.ds(off, SEL_CHUNK), :], q, _NT,
                                 preferred_element_type=F32)
            rows8 = sel_ref[0, 0, 0, pl.ds(pl.multiple_of(c * 8, 8), 8), :]
            chosen = jnp.concatenate(
                [jnp.broadcast_to(rows8[i:i + 1], (SLC_BLOCK, qb)) for i in range(per_chunk)],
                axis=0)
            d = (_iota((SEL_CHUNK, qb), 1) - _iota((SEL_CHUNK, qb), 0)) + (q0 - off)
            mask = (chosen > 0.5) & (d >= 0)
            ss = _masked_scores(st, mask, d.astype(F32), slopes)
            m_old = m_ref[...]
            m_new = jnp.concatenate(
                [jnp.maximum(m_old[:, _head_cols(r)], jnp.max(ss[r], axis=0, keepdims=True))
                 for r in range(NSA_REP)], axis=1)
            ps = [jnp.exp(ss[r] - m_new[:, _head_cols(r)]) for r in range(NSA_REP)]
            alpha = jnp.exp(m_old - m_new)
            l_ref[...] = alpha * l_ref[...] + jnp.concatenate(
                [jnp.sum(p, axis=0, keepdims=True) for p in ps], axis=1)
            pt = jnp.concatenate([p.astype(BF16) for p in ps], axis=1)
            acc_ref[...] = alpha * acc_ref[...] + jnp.dot(vst_ref[0, 0, c], pt,
                                                          preferred_element_type=F32)
            m_ref[...] = m_new

        return carry

    lax.fori_loop(0, lax.shift_right_logical(q0, int(np.log2(SEL_CHUNK))) + 1, chunk_step, 0)

    wk = WINDOW + qb
    start = pl.multiple_of(jnp.maximum(q0 - WINDOW, 0), qb)
    st = lax.dot_general(kw_ref[0, pl.ds(start, wk), :], q, _NT, preferred_element_type=F32)
    d = (_iota((wk, qb), 1) - _iota((wk, qb), 0)) + (q0 - start)
    ss = _masked_scores(st, (d >= 0) & (d < WINDOW), d.astype(F32), slopes)
    ps = [jnp.exp(s - jnp.max(s, axis=0, keepdims=True)) for s in ss]
    l_win = jnp.concatenate([jnp.sum(p, axis=0, keepdims=True) for p in ps], axis=1)
    pt = jnp.concatenate([p.astype(BF16) for p in ps], axis=1)
    first_blk = lax.shift_right_logical(start, int(np.log2(qb)))
    o_win = jnp.zeros((NSA_DH, NSA_REP * qb), F32)
    for i in range(wk // qb):
        o_win = o_win + jnp.dot(vwt_ref[0, 0, first_blk + i], pt[i * qb:(i + 1) * qb],
                                preferred_element_type=F32)

    gates = _sigmoid(gate_ref[0, 0])
    gate_row = lambda br: jnp.concatenate(
        [gates[3 * r + br:3 * r + br + 1] for r in range(NSA_REP)], axis=1)
    l_sel = l_ref[...]
    comb = (gate_row(0) * ocmp_ref[0, 0, 0]
            + gate_row(1) * (acc_ref[...] * jnp.where(l_sel > 0.0, 1.0 / l_sel, 0.0))
            + gate_row(2) * (o_win * (1.0 / l_win)))
    for r in range(NSA_REP):
        cols = slice(r * NSA_DH, (r + 1) * NSA_DH)
        z = z_ref[0, :, cols]
        y_ref[0, :, cols] = (z * _sigmoid(z) * comb[:, _head_cols(r)].T).astype(y_ref.dtype)


def _sel_win(words, slopes, pa, vst, vwt, sel, ocmp, gates_t, pb):
    b, s, _ = pa.shape
    nq = s // Q_BLOCK
    gw = NSA_REP * NSA_DH
    q_cols = NSA_HEADS * NSA_DH // NSA_DH
    kv = lambda slot: pl.BlockSpec(
        (1, s, NSA_DH), lambda bi, gi, qi, *_: (bi, 0, q_cols + slot * NSA_GROUPS + gi))
    whole = lambda arr: pl.BlockSpec((1, 1) + arr.shape[2:], lambda bi, gi, qi, *_: (bi, gi, 0, 0, 0))
    per_q = lambda arr: pl.BlockSpec((1, 1, 1) + arr.shape[3:],
                                     lambda bi, gi, qi, *_: (bi, gi, qi, 0, 0))
    grid_spec = pltpu.PrefetchScalarGridSpec(
        num_scalar_prefetch=2,
        grid=(b, NSA_GROUPS, nq),
        in_specs=[
            pl.BlockSpec((1, Q_BLOCK, gw), lambda bi, gi, qi, *_: (bi, qi, gi)),
            kv(2), whole(vst), kv(4), whole(vwt),
            per_q(sel), per_q(ocmp),
            pl.BlockSpec((1, 1, NSA_REP * 3, Q_BLOCK), lambda bi, gi, qi, *_: (bi, gi, 0, qi)),
            pl.BlockSpec((1, Q_BLOCK, gw), lambda bi, gi, qi, *_: (bi, qi, gi)),
        ],
        out_specs=pl.BlockSpec((1, Q_BLOCK, gw), lambda bi, gi, qi, *_: (bi, qi, gi)),
        scratch_shapes=[
            pltpu.VMEM((1, gw), F32),
            pltpu.VMEM((1, gw), F32),
            pltpu.VMEM((NSA_DH, gw), F32),
        ],
    )
    return pl.pallas_call(
        functools.partial(_sel_win_kernel, words_per_q=words.shape[0] // (b * NSA_GROUPS * nq)),
        grid_spec=grid_spec,
        out_shape=jax.ShapeDtypeStruct((b, s, NSA_GROUPS * gw), BF16),
        compiler_params=pltpu.CompilerParams(
            dimension_semantics=("arbitrary", "arbitrary", "arbitrary"),
            vmem_limit_bytes=VMEM_LIMIT),
    )(words, slopes, pa, pa, vst, pa, vwt, sel, ocmp, gates_t, pb)


def _pack_chunk_flags(counts):
    b, g, nq, _, n_slc = counts.shape
    per_chunk = SEL_CHUNK // SLC_BLOCK
    n_chunks = n_slc // per_chunk
    chunk_any = counts[:, :, :, 0].reshape(b, g, nq, n_chunks, per_chunk).max(axis=-1) > 0.5
    n_words = -(-n_chunks // 32)
    pad = n_words * 32 - n_chunks
    bits = jnp.pad(chunk_any, ((0, 0), (0, 0), (0, 0), (0, pad))).astype(jnp.uint32)
    bits = bits.reshape(b, g, nq, n_words, 32) << jnp.arange(32, dtype=jnp.uint32)
    words = bits.sum(axis=-1, dtype=jnp.uint32)
    return lax.bitcast_convert_type(words, jnp.int32).reshape(-1)


def _keys_on_lanes(v, chunk):
    b, s, _ = v.shape
    return v.reshape(b, s // chunk, chunk, NSA_GROUPS, NSA_DH).transpose(0, 3, 1, 4, 2)


def kernel(x, norm_g, ret_w_in, ret_w_out, nsa_w_in, nsa_cmp_pos_k, nsa_cmp_w1_k, nsa_cmp_w2_k,
           nsa_cmp_pos_v, nsa_cmp_w1_v, nsa_cmp_w2_v, nsa_w_out, final_g):
    b, s, d = x.shape
    m = b * s
    x2 = x.reshape(m, d)

    qk_w = 2 * RET_HEADS * RET_DK
    v_w = RET_HEADS * RET_DV
    w_in = ret_w_in[0].astype(BF16)
    scale = jnp.concatenate([jnp.ones((qk_w // 2,), F32), jnp.full((qk_w // 2,), RET_DK ** -0.5, F32),
                             jnp.ones((v_w,), F32)])
    qkv = _norm_proj(x2, norm_g[0], w_in[:, :qk_w + v_w], scale, BF16, tm=1024, tn=1024)
    gate = _norm_proj(x2, norm_g[0], w_in[:, qk_w + v_w:], jnp.ones((v_w,), F32), F32,
                      tm=1024, tn=1024)
    y = _retention(qkv.reshape(b, s, qk_w + v_w), gate.reshape(b, s, v_w))
    h1 = _out_proj(y.reshape(m, v_w), ret_w_out[0].astype(BF16), x2, final_g, final_norm=False)

    width = NSA_HEADS * NSA_DH
    kvw = NSA_GROUPS * NSA_DH
    n_gate = NSA_HEADS * 3
    qkv_w = width + 6 * kvw
    w_n = nsa_w_in[0]
    w_a = w_n[:, :qkv_w].astype(BF16)
    scale_a = jnp.concatenate([jnp.full((width,), NSA_DH ** -0.5, F32), jnp.ones((6 * kvw,), F32)])
    zg_w = width + 128
    w_b = jnp.concatenate([w_n[:, qkv_w + n_gate:], w_n[:, qkv_w:qkv_w + n_gate],
                           jnp.zeros((d, 128 - n_gate), F32)], axis=1).astype(BF16)
    pa = _norm_proj(h1, norm_g[1], w_a, scale_a, BF16, tm=1024, tn=1024).reshape(b, s, qkv_w)
    pb = _norm_proj(h1, norm_g[1], w_b, jnp.ones((zg_w,), F32), F32, tm=512, tn=zg_w)
    pb = pb.reshape(b, s, zg_w)

    kv_cols = lambda slot: slice(width + slot * kvw, width + (slot + 1) * kvw)
    k_cmp = _compress(pa[:, :, kv_cols(0)], nsa_cmp_pos_k[0], nsa_cmp_w1_k[0], nsa_cmp_w2_k[0])
    v_cmp = _compress(pa[:, :, kv_cols(1)], nsa_cmp_pos_v[0], nsa_cmp_w1_v[0], nsa_cmp_w2_v[0])
    slopes = jnp.exp2(-8.0 * jnp.arange(1, NSA_HEADS + 1, dtype=F32) / NSA_HEADS)
    ocmp, sel, counts = _cmp_select(pa, k_cmp, v_cmp, slopes)
    words = _pack_chunk_flags(counts)
    gates_t = pb[:, :, width:width + n_gate].reshape(b, s, NSA_GROUPS, NSA_REP * 3)
    gates_t = gates_t.transpose(0, 2, 3, 1)
    vst = _keys_on_lanes(pa[:, :, kv_cols(3)], SEL_CHUNK)
    vwt = _keys_on_lanes(pa[:, :, kv_cols(5)], Q_BLOCK)
    y = _sel_win(words, slopes, pa, vst, vwt, sel, ocmp, gates_t, pb)
    out = _out_proj(y.reshape(m, width), nsa_w_out[0].astype(BF16), h1, final_g, final_norm=True)
    return out.reshape(b, s, d)
```

```python
import functools

import jax
import jax.numpy as jnp
import numpy as np
from jax import lax
from jax.experimental import pallas as pl
from jax.experimental.pallas import tpu as pltpu

F32 = jnp.float32
BF16 = jnp.bfloat16

EPS = 1e-6
LOG2E = 1.4426950408889634
MASKED = -2e30
M_FLOOR = -1e30
TAKEN = -3e38
N_FORCED = 3

RET_HEADS = 8
RET_DK = 256
RET_DV = 512
RET_CHUNK = 256

NSA_HEADS = 16
NSA_GROUPS = 4
NSA_REP = 4
NSA_DH = 128
CMP_LEN = 32
CMP_STRIDE = 16
SLC_BLOCK = 64
TOP_N = 16
WINDOW = 512
Q_BLOCK = 128
SEL_CHUNK = 256

VMEM_LIMIT = 56 * 1024 * 1024

_NT = (((1,), (1,)), ((), ()))
_TN = (((0,), (0,)), ((), ()))


def _sigmoid(x):
    return 1.0 / (1.0 + jnp.exp(-x))


def _iota(shape, dim):
    return lax.broadcasted_iota(jnp.int32, shape, dim)


def _norm_proj_kernel(x_ref, g_ref, w_ref, s_ref, o_ref, xn_ref):
    @pl.when(pl.program_id(1) == 0)
    def _():
        x = x_ref[...]
        ms = jnp.mean(x * x, axis=-1, keepdims=True)
        xn_ref[...] = (x * lax.rsqrt(ms + EPS) * g_ref[...]).astype(xn_ref.dtype)

    acc = jnp.dot(xn_ref[...], w_ref[...], preferred_element_type=F32)
    o_ref[...] = (acc * s_ref[...]).astype(o_ref.dtype)


def _norm_proj(x2, g, w, col_scale, out_dtype, tm, tn):
    m, k = x2.shape
    n = w.shape[1]
    return pl.pallas_call(
        _norm_proj_kernel,
        grid=(m // tm, n // tn),
        in_specs=[
            pl.BlockSpec((tm, k), lambda i, j: (i, 0)),
            pl.BlockSpec((1, k), lambda i, j: (0, 0)),
            pl.BlockSpec((k, tn), lambda i, j: (0, j)),
            pl.BlockSpec((1, tn), lambda i, j: (0, j)),
        ],
        out_specs=pl.BlockSpec((tm, tn), lambda i, j: (i, j)),
        out_shape=jax.ShapeDtypeStruct((m, n), out_dtype),
        scratch_shapes=[pltpu.VMEM((tm, k), BF16)],
        compiler_params=pltpu.CompilerParams(
            dimension_semantics=("arbitrary", "arbitrary"), vmem_limit_bytes=VMEM_LIMIT),
    )(x2, g.reshape(1, k), w, col_scale.reshape(1, n))


def _proj_kernel(x_ref, w_ref, s_ref, o_ref):
    acc = jnp.dot(x_ref[...], w_ref[...], preferred_element_type=F32)
    o_ref[...] = (acc * s_ref[...]).astype(o_ref.dtype)


def _proj(x2, w, col_scale, out_dtype, tm, tn):
    m, k = x2.shape
    n = w.shape[1]
    return pl.pallas_call(
        _proj_kernel,
        grid=(m // tm, n // tn),
        in_specs=[
            pl.BlockSpec((tm, k), lambda i, j: (i, 0)),
            pl.BlockSpec((k, tn), lambda i, j: (0, j)),
            pl.BlockSpec((1, tn), lambda i, j: (0, j)),
        ],
        out_specs=pl.BlockSpec((tm, tn), lambda i, j: (i, j)),
        out_shape=jax.ShapeDtypeStruct((m, n), out_dtype),
        compiler_params=pltpu.CompilerParams(
            dimension_semantics=("arbitrary", "arbitrary"), vmem_limit_bytes=VMEM_LIMIT),
    )(x2, w, col_scale.reshape(1, n))


def _out_proj_kernel(y_ref, w_ref, res_ref, g_ref, *refs, last_layer):
    o_refs, acc_ref = refs[:-1], refs[-1]
    kk = pl.program_id(1)

    @pl.when(kk == 0)
    def _():
        acc_ref[...] = jnp.zeros_like(acc_ref)

    acc_ref[...] += jnp.dot(y_ref[...], w_ref[...], preferred_element_type=F32)

    @pl.when(kk == pl.num_programs(1) - 1)
    def _():
        h = res_ref[...] + acc_ref[...]
        ms = jnp.mean(h * h, axis=-1, keepdims=True)
        hn = h * lax.rsqrt(ms + EPS) * g_ref[...]
        if last_layer:
            o_refs[0][...] = hn
        else:
            o_refs[0][...] = h
            o_refs[1][...] = hn.astype(o_refs[1].dtype)


def _out_proj(y2, w, res2, g, last_layer, tm=512, tk=1024):
    m, kd = y2.shape
    n = w.shape[1]
    row_blk = pl.BlockSpec((tm, n), lambda i, k: (i, 0))
    out_shape = [jax.ShapeDtypeStruct((m, n), F32)]
    if not last_layer:
        out_shape.append(jax.ShapeDtypeStruct((m, n), BF16))
    return pl.pallas_call(
        functools.partial(_out_proj_kernel, last_layer=last_layer),
        grid=(m // tm, kd // tk),
        in_specs=[
            pl.BlockSpec((tm, tk), lambda i, k: (i, k)),
            pl.BlockSpec((tk, n), lambda i, k: (k, 0)),
            row_blk,
            pl.BlockSpec((1, n), lambda i, k: (0, 0)),
        ],
        out_specs=[row_blk] * len(out_shape),
        out_shape=out_shape,
        scratch_shapes=[pltpu.VMEM((tm, n), F32)],
        compiler_params=pltpu.CompilerParams(
            dimension_semantics=("arbitrary", "arbitrary"), vmem_limit_bytes=VMEM_LIMIT),
    )(y2, w, res2, g.reshape(1, n))


def _retention_kernel(q_ref, k_ref, v_ref, gate_ref, di_ref, qd_ref, kd_ref, cd_ref, y_ref,
                      state_ref, *, chunk, n_sub):
    @pl.when(pl.program_id(2) == 0)
    def _():
        state_ref[...] = jnp.zeros_like(state_ref)

    di = di_ref[0]
    qd = qd_ref[0]
    kd = kd_ref[0]
    cd = cd_ref[0]
    for i in range(n_sub):
        rows = pl.ds(i * chunk, chunk)
        q = q_ref[0, rows, :]
        k = k_ref[0, rows, :]
        v = v_ref[0, rows, :]
        state = state_ref[...]
        s = lax.dot_general(q, k, _NT, preferred_element_type=F32) * di
        o = jnp.dot(s.astype(BF16), v, preferred_element_type=F32)
        o = o + jnp.dot(q, state.astype(BF16), preferred_element_type=F32) * qd
        k_dec = (k.astype(F32) * kd).astype(BF16)
        state_ref[...] = state * cd + lax.dot_general(k_dec, v, _TN, preferred_element_type=F32)
        mu = jnp.mean(o, axis=-1, keepdims=True)
        oc = o - mu
        var = jnp.mean(oc * oc, axis=-1, keepdims=True)
        on = oc * lax.rsqrt(var + EPS)
        gt = gate_ref[0, rows, :].astype(F32)
        y_ref[0, rows, :] = (gt * _sigmoid(gt) * on).astype(y_ref.dtype)


def _retention(proj, tokens_per_step=1024):
    b, s, _ = proj.shape
    h, c = RET_HEADS, RET_CHUNK
    t = min(tokens_per_step, s)
    log_g = jnp.log1p(-jnp.exp2(-5.0 - jnp.arange(h, dtype=F32)))
    idx = jnp.arange(c, dtype=F32)
    diff = idx[:, None] - idx[None, :]
    decay_intra = jnp.where(diff >= 0, jnp.exp(diff[None] * log_g[:, None, None]), 0.0)
    q_decay = jnp.exp((idx[None, :] + 1.0) * log_g[:, None])[:, :, None]
    k_decay = jnp.exp((c - 1.0 - idx[None, :]) * log_g[:, None])[:, :, None]
    chunk_decay = jnp.exp(c * log_g)[:, None, None]
    nk = RET_HEADS * RET_DK // RET_DK
    nv = 2 * RET_HEADS * RET_DK // RET_DV
    return pl.pallas_call(
        functools.partial(_retention_kernel, chunk=c, n_sub=t // c),
        grid=(b, h, s // t),
        in_specs=[
            pl.BlockSpec((1, t, RET_DK), lambda bi, hi, ti: (bi, ti, hi)),
            pl.BlockSpec((1, t, RET_DK), lambda bi, hi, ti: (bi, ti, nk + hi)),
            pl.BlockSpec((1, t, RET_DV), lambda bi, hi, ti: (bi, ti, nv + hi)),
            pl.BlockSpec((1, t, RET_DV), lambda bi, hi, ti: (bi, ti, nv + h + hi)),
            pl.BlockSpec((1, c, c), lambda bi, hi, ti: (hi, 0, 0)),
            pl.BlockSpec((1, c, 1), lambda bi, hi, ti: (hi, 0, 0)),
            pl.BlockSpec((1, c, 1), lambda bi, hi, ti: (hi, 0, 0)),
            pl.BlockSpec((1, 1, 1), lambda bi, hi, ti: (hi, 0, 0)),
        ],
        out_specs=pl.BlockSpec((1, t, RET_DV), lambda bi, hi, ti: (bi, ti, hi)),
        out_shape=jax.ShapeDtypeStruct((b, s, h * RET_DV), BF16),
        scratch_shapes=[pltpu.VMEM((RET_DK, RET_DV), F32)],
        compiler_params=pltpu.CompilerParams(
            dimension_semantics=("arbitrary", "arbitrary", "arbitrary"),
            vmem_limit_bytes=VMEM_LIMIT),
    )(proj, proj, proj, proj, decay_intra, q_decay, k_decay, chunk_decay)


def _compress_kernel(x_ref, xn_ref, pos_ref, w1_ref, w2_ref, o_ref):
    half = CMP_LEN // 2
    width = NSA_GROUPS * NSA_DH
    for g in range(NSA_GROUPS):
        acc = jnp.zeros((x_ref.shape[1], NSA_DH), F32)
        for l in range(half):
            cols = slice(l * width + g * NSA_DH, l * width + (g + 1) * NSA_DH)
            xa = (x_ref[0, :, cols].astype(F32) + pos_ref[l:l + 1, :]).astype(BF16)
            acc += jnp.dot(xa, w1_ref[l * NSA_DH:(l + 1) * NSA_DH, :], preferred_element_type=F32)
            lb = half + l
            xb = (xn_ref[0, :, cols].astype(F32) + pos_ref[lb:lb + 1, :]).astype(BF16)
            acc += jnp.dot(xb, w1_ref[lb * NSA_DH:(lb + 1) * NSA_DH, :], preferred_element_type=F32)
        hid = (acc * _sigmoid(acc)).astype(BF16)
        out = jnp.dot(hid, w2_ref[...], preferred_element_type=F32)
        o_ref[0, :, g * NSA_DH:(g + 1) * NSA_DH] = out.astype(o_ref.dtype)


def _compress(raw, pos, w1, w2):
    b, s, width = raw.shape
    rows = s // CMP_STRIDE
    x = raw.reshape(b, rows, CMP_STRIDE * width)
    xn = jnp.concatenate([x[:, 1:], jnp.zeros_like(x[:, :1])], axis=1)
    tr = min(256, rows)
    blk = pl.BlockSpec((1, tr, CMP_STRIDE * width), lambda bi, ri: (bi, ri, 0))
    return pl.pallas_call(
        _compress_kernel,
        grid=(b, rows // tr),
        in_specs=[
            blk, blk,
            pl.BlockSpec((CMP_LEN, NSA_DH), lambda bi, ri: (0, 0)),
            pl.BlockSpec((CMP_LEN * NSA_DH, NSA_DH), lambda bi, ri: (0, 0)),
            pl.BlockSpec((NSA_DH, NSA_DH), lambda bi, ri: (0, 0)),
        ],
        out_specs=pl.BlockSpec((1, tr, width), lambda bi, ri: (bi, ri, 0)),
        out_shape=jax.ShapeDtypeStruct((b, rows, width), BF16),
        compiler_params=pltpu.CompilerParams(
            dimension_semantics=("arbitrary", "arbitrary"), vmem_limit_bytes=VMEM_LIMIT),
    )(x, xn, pos, w1.astype(BF16), w2.astype(BF16))


def _stack_heads(qblk):
    return jnp.concatenate(
        [qblk[:, r * NSA_DH:(r + 1) * NSA_DH] for r in range(NSA_REP)], axis=0)


def _head_cols(r):
    return slice(r * Q_BLOCK, (r + 1) * Q_BLOCK)


def _alibi_table(slopes, n_keys, key_stride):
    j = jnp.arange(n_keys, dtype=F32)[:, None] * key_stride
    i = jnp.arange(Q_BLOCK, dtype=F32)[None, :]
    table = slopes.reshape(NSA_GROUPS, 1, NSA_REP, 1) * (j - i)[None, :, None, :]
    return table.reshape(NSA_GROUPS, n_keys, NSA_REP * Q_BLOCK)


def _chunk_shift(slopes, delta):
    row = jnp.full((1, Q_BLOCK), delta, jnp.int32).astype(F32)
    return jnp.concatenate([s * row for s in slopes], axis=1)


def _cmp_select_kernel(slopes_ref, q_ref, kc_ref, vct_ref, ovt_ref, spread_ref, bias_ref, ocmp_ref,
                       sel_ref, cnt_ref, s_scr, o_scr, imp_scr, *, ck):
    g = pl.program_id(1)
    q0 = pl.program_id(2) * Q_BLOCK
    qb = Q_BLOCK
    n_slc = ovt_ref.shape[1]
    log_ck = int(np.log2(ck))

    q = _stack_heads(q_ref[0])
    slopes = [slopes_ref[g * NSA_REP + r] for r in range(NSA_REP)]
    last_c = lax.shift_right_logical(q0, int(np.log2(CMP_STRIDE))) + (qb - CMP_LEN) // CMP_STRIDE
    n_act = lax.shift_right_logical(last_c, log_ck) + 1
    rel = _iota((ck, qb), 1) - CMP_STRIDE * _iota((ck, qb), 0)
    chunk_delta = lambda off: CMP_STRIDE * off + (CMP_LEN - 1) - q0

    def score_step(c, m):
        off = pl.multiple_of(c * ck, ck)
        st = lax.dot_general(kc_ref[0, pl.ds(off, ck), :], q, _NT, preferred_element_type=F32)
        valid = rel >= chunk_delta(off)
        tops = []
        for r in range(NSA_REP):
            s = jnp.where(valid, st[:, _head_cols(r)] + bias_ref[0, :, _head_cols(r)], MASKED)
            s_scr[pl.ds(off, ck), _head_cols(r)] = s
            tops.append(jnp.max(s, axis=0, keepdims=True))
        return jnp.maximum(m, jnp.concatenate(tops, axis=1) + _chunk_shift(slopes, chunk_delta(off)))

    m_all = lax.fori_loop(0, n_act, score_step, jnp.full((1, NSA_REP * qb), M_FLOOR, F32))

    o_scr[...] = jnp.zeros_like(o_scr)
    imp_scr[...] = jnp.zeros_like(imp_scr)

    def prob_step(c, l):
        off = pl.multiple_of(c * ck, ck)
        shifted_m = m_all - _chunk_shift(slopes, chunk_delta(off))
        p = jnp.exp2(s_scr[pl.ds(off, ck), :] - shifted_m)
        pb = p.astype(BF16)
        o_scr[...] += jnp.dot(vct_ref[0, 0, c], pb, preferred_element_type=F32)
        imp_scr[...] += jnp.dot(ovt_ref[c], pb, preferred_element_type=F32)
        return l + jnp.sum(p, axis=0, keepdims=True)

    l = lax.fori_loop(0, n_act, prob_step, jnp.zeros((1, NSA_REP * qb), F32))
    inv = jnp.where(l > 0.0, 1.0 / l, 0.0)
    ocmp_ref[0, 0, 0] = o_scr[...] * inv
    imp_w = imp_scr[...] * inv
    imp_t = imp_w[:, _head_cols(0)]
    for r in range(1, NSA_REP):
        imp_t = imp_t + imp_w[:, _head_cols(r)]

    jrow = _iota((n_slc, qb), 0)
    cur = lax.shift_right_logical(q0 + _iota((n_slc, qb), 1), int(np.log2(SLC_BLOCK)))
    forced = (jrow == 0) | (jrow == cur) | (jrow == cur - 1)
    score = jnp.where(forced, TAKEN, jnp.where(jrow <= cur, imp_t, -1.0))
    for _ in range(min(TOP_N, n_slc) - N_FORCED):
        best = jnp.max(score, axis=0, keepdims=True)
        first = jnp.min(jnp.where(score == best, jrow, n_slc), axis=0, keepdims=True)
        score = jnp.where(jrow == first, TAKEN, score)
    chosen = jnp.where(score == TAKEN, 1.0, 0.0).astype(BF16)
    sel_ref[0, 0, 0] = jnp.dot(spread_ref[...], chosen, preferred_element_type=F32)
    cnt_ref[0, 0, 0] = lax.dot_general(jnp.ones((8, qb), BF16), chosen, _NT,
                                       preferred_element_type=F32)


def _cmp_select(pa, k_cmp, v_cmp, slopes):
    b, s, _ = pa.shape
    nq = s // Q_BLOCK
    n_cmp = k_cmp.shape[1]
    n_slc = s // SLC_BLOCK
    ck = min(256, n_cmp)
    n_ck = n_cmp // ck
    gw = NSA_REP * NSA_DH
    cstart = np.arange(n_cmp)[None, :] * CMP_STRIDE
    sstart = np.arange(n_slc)[:, None] * SLC_BLOCK
    real = np.arange(n_cmp)[None, :] < (s - CMP_LEN) // CMP_STRIDE + 1
    ovt = ((cstart < sstart + SLC_BLOCK) & (cstart + CMP_LEN > sstart) & real).astype(np.float32)
    ovt = jnp.asarray(ovt.reshape(n_slc, n_ck, ck).transpose(1, 0, 2), BF16)
    per_chunk = SEL_CHUNK // SLC_BLOCK
    n_sel_chunks = s // SEL_CHUNK
    spread = np.zeros((8 * n_sel_chunks, n_slc), np.float32)
    for j in range(n_slc):
        spread[8 * (j // per_chunk) + j % per_chunk, j] = 1.0
    spread = jnp.asarray(spread, BF16)
    vct = v_cmp.reshape(b, n_ck, ck, NSA_GROUPS, NSA_DH).transpose(0, 3, 1, 4, 2)
    bias = _alibi_table(slopes, ck, CMP_STRIDE)
    return pl.pallas_call(
        functools.partial(_cmp_select_kernel, ck=ck),
        grid=(b, NSA_GROUPS, nq),
        in_specs=[
            pl.BlockSpec(memory_space=pltpu.SMEM),
            pl.BlockSpec((1, Q_BLOCK, gw), lambda bi, gi, qi: (bi, qi, gi)),
            pl.BlockSpec((1, n_cmp, NSA_DH), lambda bi, gi, qi: (bi, 0, gi)),
            pl.BlockSpec((1, 1, n_ck, NSA_DH, ck), lambda bi, gi, qi: (bi, gi, 0, 0, 0)),
            pl.BlockSpec((n_ck, n_slc, ck), lambda bi, gi, qi: (0, 0, 0)),
            pl.BlockSpec((8 * n_sel_chunks, n_slc), lambda bi, gi, qi: (0, 0)),
            pl.BlockSpec((1, ck, gw), lambda bi, gi, qi: (gi, 0, 0)),
        ],
        out_specs=[
            pl.BlockSpec((1, 1, 1, NSA_DH, gw), lambda bi, gi, qi: (bi, gi, qi, 0, 0)),
            pl.BlockSpec((1, 1, 1, 8 * n_sel_chunks, Q_BLOCK), lambda bi, gi, qi: (bi, gi, qi, 0, 0)),
            pl.BlockSpec((1, 1, 1, 8, n_slc), lambda bi, gi, qi: (bi, gi, qi, 0, 0)),
        ],
        out_shape=[
            jax.ShapeDtypeStruct((b, NSA_GROUPS, nq, NSA_DH, gw), F32),
            jax.ShapeDtypeStruct((b, NSA_GROUPS, nq, 8 * n_sel_chunks, Q_BLOCK), F32),
            jax.ShapeDtypeStruct((b, NSA_GROUPS, nq, 8, n_slc), F32),
        ],
        scratch_shapes=[
            pltpu.VMEM((n_cmp, gw), F32),
            pltpu.VMEM((NSA_DH, gw), F32),
            pltpu.VMEM((n_slc, gw), F32),
        ],
        compiler_params=pltpu.CompilerParams(
            dimension_semantics=("arbitrary", "arbitrary", "arbitrary"),
            vmem_limit_bytes=VMEM_LIMIT),
    )(slopes, pa, k_cmp, vct, ovt, spread, bias)


def _masked_scores(st, mask, bias_ref):
    n = st.shape[0]
    return [jnp.where(mask, st[:, _head_cols(r)] + bias_ref[0, :n, _head_cols(r)], MASKED)
            for r in range(NSA_REP)]


def _sel_win_kernel(ids_ref, cnt_ref, slopes_ref, q_ref, ks_ref, vst_ref, kw_ref, vwt_ref, sel_ref,
                    ocmp_ref, gate_ref, z_ref, bias_ref, y_ref, m_ref, l_ref, acc_ref, *, n_chunks):
    b = pl.program_id(0)
    g = pl.program_id(1)
    qi = pl.program_id(2)
    nq = pl.num_programs(2)
    qb = Q_BLOCK
    q0 = qi * qb
    per_chunk = SEL_CHUNK // SLC_BLOCK

    q = _stack_heads(q_ref[0])
    slopes = [slopes_ref[g * NSA_REP + r] for r in range(NSA_REP)]

    m_ref[...] = jnp.full_like(m_ref, M_FLOOR)
    l_ref[...] = jnp.zeros_like(l_ref)
    acc_ref[...] = jnp.zeros_like(acc_ref)
    step = (b * NSA_GROUPS + g) * nq + qi
    n_active = cnt_ref[step]
    id_base = step * n_chunks

    def chunk_scores(c, threshold):
        off = pl.multiple_of(c * SEL_CHUNK, SEL_CHUNK)
        st = lax.dot_general(ks_ref[0, pl.ds(off, SEL_CHUNK), :], q, _NT,
                             preferred_element_type=F32)
        rows8 = sel_ref[0, 0, 0, pl.ds(pl.multiple_of(c * 8, 8), 8), :]
        chosen = jnp.concatenate(
            [jnp.broadcast_to(rows8[i:i + 1], (SLC_BLOCK, qb)) for i in range(per_chunk)],
            axis=0)
        rel = _iota((SEL_CHUNK, qb), 1) - _iota((SEL_CHUNK, qb), 0)
        causal = rel >= off - q0
        return (_masked_scores(st, (chosen > threshold) & causal, bias_ref),
                _chunk_shift(slopes, off - q0))

    def pair_update(state, ca, cb, scores_a, scores_b):
        m_old, l_old, acc_old = state
        (sa, shift_a), (sb, shift_b) = scores_a, scores_b
        top = lambda ss: jnp.concatenate([jnp.max(s, axis=0, keepdims=True) for s in ss], axis=1)
        m_new = jnp.maximum(m_old, jnp.maximum(top(sa) + shift_a, top(sb) + shift_b))
        ma = m_new - shift_a
        mb = m_new - shift_b
        pa = [jnp.exp2(sa[r] - ma[:, _head_cols(r)]) for r in range(NSA_REP)]
        pb = [jnp.exp2(sb[r] - mb[:, _head_cols(r)]) for r in range(NSA_REP)]
        alpha = jnp.exp2(m_old - m_new)
        l_new = alpha * l_old + jnp.concatenate(
            [jnp.sum(pa[r], axis=0, keepdims=True) + jnp.sum(pb[r], axis=0, keepdims=True)
             for r in range(NSA_REP)], axis=1)
        pta = jnp.concatenate([p.astype(BF16) for p in pa], axis=1)
        ptb = jnp.concatenate([p.astype(BF16) for p in pb], axis=1)
        acc_new = (alpha * acc_old
                   + jnp.dot(vst_ref[0, 0, ca], pta, preferred_element_type=F32)
                   + jnp.dot(vst_ref[0, 0, cb], ptb, preferred_element_type=F32))
        return m_new, l_new, acc_new

    def load_state():
        return m_ref[...], l_ref[...], acc_ref[...]

    def store_state(state):
        m_ref[...], l_ref[...], acc_ref[...] = state

    def quad_step(i, carry):
        cs = [ids_ref[id_base + 4 * i + k] for k in range(4)]
        scores = [chunk_scores(c, 0.5) for c in cs]
        state = pair_update(load_state(), cs[0], cs[1], scores[0], scores[1])
        store_state(pair_update(state, cs[2], cs[3], scores[2], scores[3]))
        return carry

    def pair_step(i, carry):
        ca = ids_ref[id_base + 2 * i]
        cb = ids_ref[id_base + 2 * i + 1]
        thr_b = jnp.where(2 * i + 1 < n_active, 0.5, 2.0)
        store_state(pair_update(load_state(), ca, cb, chunk_scores(ca, 0.5),
                                chunk_scores(cb, thr_b)))
        return carry

    n_quads = lax.shift_right_logical(n_active, 2)
    lax.fori_loop(0, n_quads, quad_step, 0)
    lax.fori_loop(2 * n_quads, lax.shift_right_logical(n_active + 1, 1), pair_step, 0)

    wk = WINDOW + qb
    start = pl.multiple_of(jnp.maximum(q0 - WINDOW, 0), qb)
    st = lax.dot_general(kw_ref[0, pl.ds(start, wk), :], q, _NT, preferred_element_type=F32)
    rel = _iota((wk, qb), 1) - _iota((wk, qb), 0)
    ss = _masked_scores(st, (rel >= start - q0) & (rel < WINDOW + start - q0), bias_ref)
    ps = [jnp.exp2(s - jnp.max(s, axis=0, keepdims=True)) for s in ss]
    l_win = jnp.concatenate([jnp.sum(p, axis=0, keepdims=True) for p in ps], axis=1)
    pt = jnp.concatenate([p.astype(BF16) for p in ps], axis=1)
    first_blk = lax.shift_right_logical(start, int(np.log2(qb)))
    o_win = jnp.zeros((NSA_DH, NSA_REP * qb), F32)
    for i in range(wk // qb):
        o_win = o_win + jnp.dot(vwt_ref[0, 0, first_blk + i], pt[i * qb:(i + 1) * qb],
                                preferred_element_type=F32)

    gates = _sigmoid(gate_ref[0, 0])
    gate_row = lambda br: jnp.concatenate(
        [gates[3 * r + br:3 * r + br + 1] for r in range(NSA_REP)], axis=1)
    l_sel = l_ref[...]
    comb = (gate_row(0) * ocmp_ref[0, 0, 0]
            + gate_row(1) * (acc_ref[...] * jnp.where(l_sel > 0.0, 1.0 / l_sel, 0.0))
            + gate_row(2) * (o_win * (1.0 / l_win)))
    for r in range(NSA_REP):
        cols = slice(r * NSA_DH, (r + 1) * NSA_DH)
        z = z_ref[0, :, cols].astype(F32)
        y_ref[0, :, cols] = (z * _sigmoid(z) * comb[:, _head_cols(r)].T).astype(y_ref.dtype)


def _sel_win(ids, cnt, slopes, pa, vst, vwt, sel, ocmp, gates_t):
    b, s, _ = pa.shape
    nq = s // Q_BLOCK
    gw = NSA_REP * NSA_DH
    q_cols = NSA_HEADS * NSA_DH // NSA_DH
    z_col = (NSA_HEADS * NSA_DH + 6 * NSA_GROUPS * NSA_DH) // gw
    kv = lambda slot: pl.BlockSpec(
        (1, s, NSA_DH), lambda bi, gi, qi, *_: (bi, 0, q_cols + slot * NSA_GROUPS + gi))
    whole = lambda arr: pl.BlockSpec((1, 1) + arr.shape[2:], lambda bi, gi, qi, *_: (bi, gi, 0, 0, 0))
    per_q = lambda arr: pl.BlockSpec((1, 1, 1) + arr.shape[3:],
                                     lambda bi, gi, qi, *_: (bi, gi, qi, 0, 0))
    grid_spec = pltpu.PrefetchScalarGridSpec(
        num_scalar_prefetch=3,
        grid=(b, NSA_GROUPS, nq),
        in_specs=[
            pl.BlockSpec((1, Q_BLOCK, gw), lambda bi, gi, qi, *_: (bi, qi, gi)),
            kv(2), whole(vst), kv(4), whole(vwt),
            per_q(sel), per_q(ocmp),
            pl.BlockSpec((1, 1, NSA_REP * 3, Q_BLOCK), lambda bi, gi, qi, *_: (bi, gi, 0, qi)),
            pl.BlockSpec((1, Q_BLOCK, gw), lambda bi, gi, qi, *_: (bi, qi, z_col + gi)),
            pl.BlockSpec((1, WINDOW + Q_BLOCK, gw), lambda bi, gi, qi, *_: (gi, 0, 0)),
        ],
        out_specs=pl.BlockSpec((1, Q_BLOCK, gw), lambda bi, gi, qi, *_: (bi, qi, gi)),
        scratch_shapes=[
            pltpu.VMEM((1, gw), F32),
            pltpu.VMEM((1, gw), F32),
            pltpu.VMEM((NSA_DH, gw), F32),
        ],
    )
    return pl.pallas_call(
        functools.partial(_sel_win_kernel, n_chunks=s // SEL_CHUNK),
        grid_spec=grid_spec,
        out_shape=jax.ShapeDtypeStruct((b, s, NSA_GROUPS * gw), BF16),
        compiler_params=pltpu.CompilerParams(
            dimension_semantics=("arbitrary", "arbitrary", "arbitrary"),
            vmem_limit_bytes=VMEM_LIMIT),
    )(ids, cnt, slopes, pa, pa, vst, pa, vwt, sel, ocmp, gates_t, pa,
      _alibi_table(slopes, WINDOW + Q_BLOCK, 1))


def _active_chunk_lists(counts):
    b, g, nq, _, n_slc = counts.shape
    per_chunk = SEL_CHUNK // SLC_BLOCK
    n_chunks = n_slc // per_chunk
    chunk = jnp.arange(n_chunks, dtype=jnp.int32)
    causal = chunk[None, :] <= (jnp.arange(nq, dtype=jnp.int32)[:, None] * Q_BLOCK) // SEL_CHUNK
    active = counts[:, :, :, 0].reshape(b, g, nq, n_chunks, per_chunk).max(axis=-1) > 0.5
    active = active & causal
    ids = jnp.argsort(jnp.where(active, chunk, chunk + n_chunks), axis=-1).astype(jnp.int32)
    return ids.reshape(-1), active.sum(axis=-1, dtype=jnp.int32).reshape(-1)


def _keys_on_lanes(v, chunk):
    b, s, _ = v.shape
    return v.reshape(b, s // chunk, chunk, NSA_GROUPS, NSA_DH).transpose(0, 3, 1, 4, 2)


def kernel(x, norm_g, ret_w_in, ret_w_out, nsa_w_in, nsa_cmp_pos_k, nsa_cmp_w1_k, nsa_cmp_w2_k,
           nsa_cmp_pos_v, nsa_cmp_w1_v, nsa_cmp_w2_v, nsa_w_out, final_g):
    b, s, d = x.shape
    m = b * s
    x2 = x.reshape(m, d)

    qk_w = 2 * RET_HEADS * RET_DK
    v_w = RET_HEADS * RET_DV
    w_in = ret_w_in[0].astype(BF16)
    scale = jnp.concatenate([jnp.ones((qk_w // 2,), F32), jnp.full((qk_w // 2,), RET_DK ** -0.5, F32),
                             jnp.ones((2 * v_w,), F32)])
    proj = _norm_proj(x2, norm_g[0], w_in, scale, BF16, tm=1024, tn=1024)
    y = _retention(proj.reshape(b, s, qk_w + 2 * v_w))
    h1, hn1 = _out_proj(y.reshape(m, v_w), ret_w_out[0].astype(BF16), x2, norm_g[1],
                        last_layer=False)

    width = NSA_HEADS * NSA_DH
    kvw = NSA_GROUPS * NSA_DH
    n_gate = NSA_HEADS * 3
    qkv_w = width + 6 * kvw
    w_n = nsa_w_in[0]
    w_a = jnp.concatenate([w_n[:, :qkv_w], w_n[:, qkv_w + n_gate:]], axis=1).astype(BF16)
    scale_a = jnp.concatenate([jnp.full((width,), NSA_DH ** -0.5 * LOG2E, F32),
                               jnp.ones((6 * kvw + width,), F32)])
    w_g = jnp.concatenate([w_n[:, qkv_w:qkv_w + n_gate], jnp.zeros((d, 128 - n_gate), F32)],
                          axis=1).astype(BF16)
    pa = _proj(hn1, w_a, scale_a, BF16, tm=min(2048, m), tn=1024)
    pa = pa.reshape(b, s, qkv_w + width)
    pg = _proj(hn1, w_g, jnp.ones((128,), F32), F32, tm=min(2048, m), tn=128)

    kv_cols = lambda slot: slice(width + slot * kvw, width + (slot + 1) * kvw)
    k_cmp = _compress(pa[:, :, kv_cols(0)], nsa_cmp_pos_k[0], nsa_cmp_w1_k[0], nsa_cmp_w2_k[0])
    v_cmp = _compress(pa[:, :, kv_cols(1)], nsa_cmp_pos_v[0], nsa_cmp_w1_v[0], nsa_cmp_w2_v[0])
    slopes = jnp.exp2(-8.0 * jnp.arange(1, NSA_HEADS + 1, dtype=F32) / NSA_HEADS) * LOG2E
    ocmp, sel, counts = _cmp_select(pa, k_cmp, v_cmp, slopes)
    ids, cnt = _active_chunk_lists(counts)
    gates_t = pg[:, :n_gate].reshape(b, s, NSA_GROUPS, NSA_REP * 3)
    gates_t = gates_t.transpose(0, 2, 3, 1)
    vst = _keys_on_lanes(pa[:, :, kv_cols(3)], SEL_CHUNK)
    vwt = _keys_on_lanes(pa[:, :, kv_cols(5)], Q_BLOCK)
    y = _sel_win(ids, cnt, slopes, pa, vst, vwt, sel, ocmp, gates_t)
    out, = _out_proj(y.reshape(m, width), nsa_w_out[0].astype(BF16), h1, final_g, last_layer=True)
    return out.reshape(b, s, d)
```

```python
import functools

import jax
import jax.numpy as jnp
import numpy as np
from jax import lax
from jax.experimental import pallas as pl
from jax.experimental.pallas import tpu as pltpu

F32 = jnp.float32
BF16 = jnp.bfloat16

EPS = 1e-6
LOG2E = 1.4426950408889634
MASKED = -2e30
M_FLOOR = -1e30
TAKEN = -3e38
N_FORCED = 3

RET_HEADS = 8
RET_DK = 256
RET_DV = 512
RET_CHUNK = 256

NSA_HEADS = 16
NSA_GROUPS = 4
NSA_REP = 4
NSA_DH = 128
CMP_LEN = 32
CMP_STRIDE = 16
SLC_BLOCK = 64
TOP_N = 16
WINDOW = 512
Q_BLOCK = 128
SEL_CHUNK = 256

VMEM_LIMIT = 56 * 1024 * 1024

_NT = (((1,), (1,)), ((), ()))
_TN = (((0,), (0,)), ((), ()))


def _sigmoid(x):
    return 1.0 / (1.0 + jnp.exp(-x))


def _iota(shape, dim):
    return lax.broadcasted_iota(jnp.int32, shape, dim)


def _norm_proj_kernel(x_ref, g_ref, w_ref, s_ref, o_ref, xn_ref):
    @pl.when(pl.program_id(1) == 0)
    def _():
        x = x_ref[...]
        ms = jnp.mean(x * x, axis=-1, keepdims=True)
        xn_ref[...] = (x * lax.rsqrt(ms + EPS) * g_ref[...]).astype(xn_ref.dtype)

    acc = jnp.dot(xn_ref[...], w_ref[...], preferred_element_type=F32)
    o_ref[...] = (acc * s_ref[...]).astype(o_ref.dtype)


def _norm_proj(x2, g, w, col_scale, out_dtype, tm, tn):
    m, k = x2.shape
    n = w.shape[1]
    return pl.pallas_call(
        _norm_proj_kernel,
        grid=(m // tm, n // tn),
        in_specs=[
            pl.BlockSpec((tm, k), lambda i, j: (i, 0)),
            pl.BlockSpec((1, k), lambda i, j: (0, 0)),
            pl.BlockSpec((k, tn), lambda i, j: (0, j)),
            pl.BlockSpec((1, tn), lambda i, j: (0, j)),
        ],
        out_specs=pl.BlockSpec((tm, tn), lambda i, j: (i, j)),
        out_shape=jax.ShapeDtypeStruct((m, n), out_dtype),
        scratch_shapes=[pltpu.VMEM((tm, k), BF16)],
        compiler_params=pltpu.CompilerParams(
            dimension_semantics=("arbitrary", "arbitrary"), vmem_limit_bytes=VMEM_LIMIT),
    )(x2, g.reshape(1, k), w, col_scale.reshape(1, n))


def _proj_kernel(x_ref, w_ref, s_ref, o_ref):
    acc = jnp.dot(x_ref[...], w_ref[...], preferred_element_type=F32)
    o_ref[...] = (acc * s_ref[...]).astype(o_ref.dtype)


def _proj(x2, w, col_scale, out_dtype, tm, tn):
    m, k = x2.shape
    n = w.shape[1]
    return pl.pallas_call(
        _proj_kernel,
        grid=(m // tm, n // tn),
        in_specs=[
            pl.BlockSpec((tm, k), lambda i, j: (i, 0)),
            pl.BlockSpec((k, tn), lambda i, j: (0, j)),
            pl.BlockSpec((1, tn), lambda i, j: (0, j)),
        ],
        out_specs=pl.BlockSpec((tm, tn), lambda i, j: (i, j)),
        out_shape=jax.ShapeDtypeStruct((m, n), out_dtype),
        compiler_params=pltpu.CompilerParams(
            dimension_semantics=("arbitrary", "arbitrary"), vmem_limit_bytes=VMEM_LIMIT),
    )(x2, w, col_scale.reshape(1, n))


def _out_proj_kernel(y_ref, w_ref, res_ref, g_ref, *o_refs, last_layer):
    h = res_ref[...] + jnp.dot(y_ref[...], w_ref[...], preferred_element_type=F32)
    ms = jnp.mean(h * h, axis=-1, keepdims=True)
    hn = h * lax.rsqrt(ms + EPS) * g_ref[...]
    if last_layer:
        o_refs[0][...] = hn
    else:
        o_refs[0][...] = h
        o_refs[1][...] = hn.astype(o_refs[1].dtype)


def _out_proj(y2, w, res2, g, last_layer, tm=512):
    m, kd = y2.shape
    n = w.shape[1]
    row_blk = pl.BlockSpec((tm, n), lambda i: (i, 0))
    out_shape = [jax.ShapeDtypeStruct((m, n), F32)]
    if not last_layer:
        out_shape.append(jax.ShapeDtypeStruct((m, n), BF16))
    return pl.pallas_call(
        functools.partial(_out_proj_kernel, last_layer=last_layer),
        grid=(m // tm,),
        in_specs=[
            pl.BlockSpec((tm, kd), lambda i: (i, 0)),
            pl.BlockSpec((kd, n), lambda i: (0, 0), pipeline_mode=pl.Buffered(1)),
            row_blk,
            pl.BlockSpec((1, n), lambda i: (0, 0)),
        ],
        out_specs=[row_blk] * len(out_shape),
        out_shape=out_shape,
        compiler_params=pltpu.CompilerParams(
            dimension_semantics=("arbitrary",), vmem_limit_bytes=VMEM_LIMIT),
    )(y2, w, res2, g.reshape(1, n))


def _retention_kernel(q_ref, k_ref, v_ref, gate_ref, di_ref, qd_ref, kd_ref, cd_ref, y_ref,
                      state_ref, *, chunk, n_sub):
    @pl.when(pl.program_id(2) == 0)
    def _():
        state_ref[...] = jnp.zeros_like(state_ref)

    di = di_ref[0]
    qd = qd_ref[0]
    kd = kd_ref[0]
    cd = cd_ref[0]
    for i in range(n_sub):
        rows = pl.ds(i * chunk, chunk)
        q = q_ref[0, rows, :]
        k = k_ref[0, rows, :]
        v = v_ref[0, rows, :]
        state = state_ref[...]
        s = lax.dot_general(q, k, _NT, preferred_element_type=F32) * di
        o = jnp.dot(s.astype(BF16), v, preferred_element_type=F32)
        o = o + jnp.dot(q, state.astype(BF16), preferred_element_type=F32) * qd
        k_dec = (k.astype(F32) * kd).astype(BF16)
        state_ref[...] = state * cd + lax.dot_general(k_dec, v, _TN, preferred_element_type=F32)
        mu = jnp.mean(o, axis=-1, keepdims=True)
        oc = o - mu
        var = jnp.mean(oc * oc, axis=-1, keepdims=True)
        on = oc * lax.rsqrt(var + EPS)
        gt = gate_ref[0, rows, :].astype(F32)
        y_ref[0, rows, :] = (gt * _sigmoid(gt) * on).astype(y_ref.dtype)


def _retention(proj, tokens_per_step=1024):
    b, s, _ = proj.shape
    h, c = RET_HEADS, RET_CHUNK
    t = min(tokens_per_step, s)
    log_g = jnp.log1p(-jnp.exp2(-5.0 - jnp.arange(h, dtype=F32)))
    idx = jnp.arange(c, dtype=F32)
    diff = idx[:, None] - idx[None, :]
    decay_intra = jnp.where(diff >= 0, jnp.exp(diff[None] * log_g[:, None, None]), 0.0)
    q_decay = jnp.exp((idx[None, :] + 1.0) * log_g[:, None])[:, :, None]
    k_decay = jnp.exp((c - 1.0 - idx[None, :]) * log_g[:, None])[:, :, None]
    chunk_decay = jnp.exp(c * log_g)[:, None, None]
    nk = RET_HEADS * RET_DK // RET_DK
    nv = 2 * RET_HEADS * RET_DK // RET_DV
    return pl.pallas_call(
        functools.partial(_retention_kernel, chunk=c, n_sub=t // c),
        grid=(b, h, s // t),
        in_specs=[
            pl.BlockSpec((1, t, RET_DK), lambda bi, hi, ti: (bi, ti, hi)),
            pl.BlockSpec((1, t, RET_DK), lambda bi, hi, ti: (bi, ti, nk + hi)),
            pl.BlockSpec((1, t, RET_DV), lambda bi, hi, ti: (bi, ti, nv + hi)),
            pl.BlockSpec((1, t, RET_DV), lambda bi, hi, ti: (bi, ti, nv + h + hi)),
            pl.BlockSpec((1, c, c), lambda bi, hi, ti: (hi, 0, 0)),
            pl.BlockSpec((1, c, 1), lambda bi, hi, ti: (hi, 0, 0)),
            pl.BlockSpec((1, c, 1), lambda bi, hi, ti: (hi, 0, 0)),
            pl.BlockSpec((1, 1, 1), lambda bi, hi, ti: (hi, 0, 0)),
        ],
        out_specs=pl.BlockSpec((1, t, RET_DV), lambda bi, hi, ti: (bi, ti, hi)),
        out_shape=jax.ShapeDtypeStruct((b, s, h * RET_DV), BF16),
        scratch_shapes=[pltpu.VMEM((RET_DK, RET_DV), F32)],
        compiler_params=pltpu.CompilerParams(
            dimension_semantics=("arbitrary", "arbitrary", "arbitrary"),
            vmem_limit_bytes=VMEM_LIMIT),
    )(proj, proj, proj, proj, decay_intra, q_decay, k_decay, chunk_decay)


def _compress_kernel(x_ref, xn_ref, pos_ref, w1_ref, w2_ref, o_ref):
    half = CMP_LEN // 2
    width = NSA_GROUPS * NSA_DH
    for g in range(NSA_GROUPS):
        acc = jnp.zeros((x_ref.shape[1], NSA_DH), F32)
        for l in range(half):
            cols = slice(l * width + g * NSA_DH, l * width + (g + 1) * NSA_DH)
            xa = (x_ref[0, :, cols].astype(F32) + pos_ref[l:l + 1, :]).astype(BF16)
            acc += jnp.dot(xa, w1_ref[l * NSA_DH:(l + 1) * NSA_DH, :], preferred_element_type=F32)
            lb = half + l
            xb = (xn_ref[0, :, cols].astype(F32) + pos_ref[lb:lb + 1, :]).astype(BF16)
            acc += jnp.dot(xb, w1_ref[lb * NSA_DH:(lb + 1) * NSA_DH, :], preferred_element_type=F32)
        hid = (acc * _sigmoid(acc)).astype(BF16)
        out = jnp.dot(hid, w2_ref[...], preferred_element_type=F32)
        o_ref[0, :, g * NSA_DH:(g + 1) * NSA_DH] = out.astype(o_ref.dtype)


def _compress(raw, pos, w1, w2):
    b, s, width = raw.shape
    rows = s // CMP_STRIDE
    x = raw.reshape(b, rows, CMP_STRIDE * width)
    xn = jnp.concatenate([x[:, 1:], jnp.zeros_like(x[:, :1])], axis=1)
    tr = min(256, rows)
    blk = pl.BlockSpec((1, tr, CMP_STRIDE * width), lambda bi, ri: (bi, ri, 0))
    return pl.pallas_call(
        _compress_kernel,
        grid=(b, rows // tr),
        in_specs=[
            blk, blk,
            pl.BlockSpec((CMP_LEN, NSA_DH), lambda bi, ri: (0, 0)),
            pl.BlockSpec((CMP_LEN * NSA_DH, NSA_DH), lambda bi, ri: (0, 0)),
            pl.BlockSpec((NSA_DH, NSA_DH), lambda bi, ri: (0, 0)),
        ],
        out_specs=pl.BlockSpec((1, tr, width), lambda bi, ri: (bi, ri, 0)),
        out_shape=jax.ShapeDtypeStruct((b, rows, width), BF16),
        compiler_params=pltpu.CompilerParams(
            dimension_semantics=("arbitrary", "arbitrary"), vmem_limit_bytes=VMEM_LIMIT),
    )(x, xn, pos, w1.astype(BF16), w2.astype(BF16))


def _stack_heads(qblk):
    return jnp.concatenate(
        [qblk[:, r * NSA_DH:(r + 1) * NSA_DH] for r in range(NSA_REP)], axis=0)


def _head_cols(r):
    return slice(r * Q_BLOCK, (r + 1) * Q_BLOCK)


def _alibi_table(slopes, n_keys, key_stride):
    j = jnp.arange(n_keys, dtype=F32)[:, None] * key_stride
    i = jnp.arange(Q_BLOCK, dtype=F32)[None, :]
    table = slopes.reshape(NSA_GROUPS, 1, NSA_REP, 1) * (j - i)[None, :, None, :]
    return table.reshape(NSA_GROUPS, n_keys, NSA_REP * Q_BLOCK)


def _chunk_shift(slopes, delta):
    row = jnp.full((1, Q_BLOCK), delta, jnp.int32).astype(F32)
    return jnp.concatenate([s * row for s in slopes], axis=1)


def _cmp_select_kernel(slopes_ref, q_ref, kc_ref, vct_ref, ovt_ref, spread_ref, bias_ref, ocmp_ref,
                       sel_ref, cnt_ref, s_scr, *, ck, nq):
    g = pl.program_id(1)
    qi = pl.program_id(2)
    q0 = qi * Q_BLOCK
    qb = Q_BLOCK
    n_ck, n_slc = ovt_ref.shape[0], ovt_ref.shape[1]

    q = _stack_heads(q_ref[0])
    slopes = [slopes_ref[g * NSA_REP + r] for r in range(NSA_REP)]
    rel = _iota((ck, qb), 1) - CMP_STRIDE * _iota((ck, qb), 0)
    chunk_delta = lambda off: CMP_STRIDE * off + (CMP_LEN - 1) - q0

    def attend_and_select(chunks, rows):
        tops = []
        for c in range(chunks):
            off = c * ck
            st = lax.dot_general(kc_ref[0, off:off + ck, :], q, _NT, preferred_element_type=F32)
            valid = rel >= chunk_delta(off)
            top_c = []
            for r in range(NSA_REP):
                s = jnp.where(valid, st[:, _head_cols(r)] + bias_ref[0, :, _head_cols(r)], MASKED)
                s_scr[off:off + ck, _head_cols(r)] = s
                top_c.append(jnp.max(s, axis=0, keepdims=True))
            tops.append(jnp.concatenate(top_c, axis=1) + _chunk_shift(slopes, chunk_delta(off)))
        m_all = functools.reduce(jnp.maximum, tops, jnp.full((1, NSA_REP * qb), M_FLOOR, F32))

        l = jnp.zeros((1, NSA_REP * qb), F32)
        o = jnp.zeros((NSA_DH, NSA_REP * qb), F32)
        imp = jnp.zeros((rows, NSA_REP * qb), F32)
        for c in range(chunks):
            off = c * ck
            shifted_m = m_all - _chunk_shift(slopes, chunk_delta(off))
            p = jnp.exp2(s_scr[off:off + ck, :] - shifted_m)
            pb = p.astype(BF16)
            l = l + jnp.sum(p, axis=0, keepdims=True)
            o = o + jnp.dot(vct_ref[0, 0, c], pb, preferred_element_type=F32)
            imp = imp + jnp.dot(ovt_ref[c, :rows, :], pb, preferred_element_type=F32)
        inv = jnp.where(l > 0.0, 1.0 / l, 0.0)
        ocmp_ref[0, 0, 0] = o * inv
        imp_w = imp * inv
        imp_t = imp_w[:, _head_cols(0)]
        for r in range(1, NSA_REP):
            imp_t = imp_t + imp_w[:, _head_cols(r)]

        jrow = _iota((rows, qb), 0)
        cur = lax.shift_right_logical(q0 + _iota((rows, qb), 1), int(np.log2(SLC_BLOCK)))
        forced = (jrow == 0) | (jrow == cur) | (jrow == cur - 1)
        score = jnp.where(forced, TAKEN, jnp.where(jrow <= cur, imp_t, -1.0))
        for _ in range(min(TOP_N, rows) - N_FORCED):
            best = jnp.max(score, axis=0, keepdims=True)
            first = jnp.min(jnp.where(score == best, jrow, rows), axis=0, keepdims=True)
            score = jnp.where(jrow == first, TAKEN, score)
        chosen = jnp.where(score == TAKEN, 1.0, 0.0).astype(BF16)
        sel_ref[0, 0, 0] = jnp.dot(spread_ref[:, :rows], chosen, preferred_element_type=F32)
        cnt_ref[0, 0, 0, :, :rows] = lax.dot_general(jnp.ones((8, qb), BF16), chosen, _NT,
                                                     preferred_element_type=F32)
        if rows < n_slc:
            cnt_ref[0, 0, 0, :, rows:] = jnp.zeros((8, n_slc - rows), F32)

    if n_ck % 2 == 0 and n_slc % 2 == 0:
        pl.when(qi < nq // 2)(lambda: attend_and_select(n_ck // 2, n_slc // 2))
        pl.when(qi >= nq // 2)(lambda: attend_and_select(n_ck, n_slc))
    else:
        attend_and_select(n_ck, n_slc)


def _cmp_select(pa, k_cmp, v_cmp, slopes):
    b, s, _ = pa.shape
    nq = s // Q_BLOCK
    n_cmp = k_cmp.shape[1]
    n_slc = s // SLC_BLOCK
    ck = min(256, n_cmp)
    n_ck = n_cmp // ck
    gw = NSA_REP * NSA_DH
    cstart = np.arange(n_cmp)[None, :] * CMP_STRIDE
    sstart = np.arange(n_slc)[:, None] * SLC_BLOCK
    real = np.arange(n_cmp)[None, :] < (s - CMP_LEN) // CMP_STRIDE + 1
    ovt = ((cstart < sstart + SLC_BLOCK) & (cstart + CMP_LEN > sstart) & real).astype(np.float32)
    ovt = jnp.asarray(ovt.reshape(n_slc, n_ck, ck).transpose(1, 0, 2), BF16)
    per_chunk = SEL_CHUNK // SLC_BLOCK
    n_sel_chunks = s // SEL_CHUNK
    spread = np.zeros((8 * n_sel_chunks, n_slc), np.float32)
    for j in range(n_slc):
        spread[8 * (j // per_chunk) + j % per_chunk, j] = 1.0
    spread = jnp.asarray(spread, BF16)
    vct = v_cmp.reshape(b, n_ck, ck, NSA_GROUPS, NSA_DH).transpose(0, 3, 1, 4, 2)
    bias = _alibi_table(slopes, ck, CMP_STRIDE)
    return pl.pallas_call(
        functools.partial(_cmp_select_kernel, ck=ck, nq=nq),
        grid=(b, NSA_GROUPS, nq),
        in_specs=[
            pl.BlockSpec(memory_space=pltpu.SMEM),
            pl.BlockSpec((1, Q_BLOCK, gw), lambda bi, gi, qi: (bi, qi, gi)),
            pl.BlockSpec((1, n_cmp, NSA_DH), lambda bi, gi, qi: (bi, 0, gi)),
            pl.BlockSpec((1, 1, n_ck, NSA_DH, ck), lambda bi, gi, qi: (bi, gi, 0, 0, 0)),
            pl.BlockSpec((n_ck, n_slc, ck), lambda bi, gi, qi: (0, 0, 0)),
            pl.BlockSpec((8 * n_sel_chunks, n_slc), lambda bi, gi, qi: (0, 0)),
            pl.BlockSpec((1, ck, gw), lambda bi, gi, qi: (gi, 0, 0)),
        ],
        out_specs=[
            pl.BlockSpec((1, 1, 1, NSA_DH, gw), lambda bi, gi, qi: (bi, gi, qi, 0, 0)),
            pl.BlockSpec((1, 1, 1, 8 * n_sel_chunks, Q_BLOCK), lambda bi, gi, qi: (bi, gi, qi, 0, 0)),
            pl.BlockSpec((1, 1, 1, 8, n_slc), lambda bi, gi, qi: (bi, gi, qi, 0, 0)),
        ],
        out_shape=[
            jax.ShapeDtypeStruct((b, NSA_GROUPS, nq, NSA_DH, gw), F32),
            jax.ShapeDtypeStruct((b, NSA_GROUPS, nq, 8 * n_sel_chunks, Q_BLOCK), F32),
            jax.ShapeDtypeStruct((b, NSA_GROUPS, nq, 8, n_slc), F32),
        ],
        scratch_shapes=[pltpu.VMEM((n_cmp, gw), F32)],
        compiler_params=pltpu.CompilerParams(
            dimension_semantics=("arbitrary", "arbitrary", "arbitrary"),
            vmem_limit_bytes=VMEM_LIMIT),
    )(slopes, pa, k_cmp, vct, ovt, spread, bias)


def _masked_scores(st, mask, bias_ref):
    n = st.shape[0]
    return [jnp.where(mask, st[:, _head_cols(r)] + bias_ref[0, :n, _head_cols(r)], MASKED)
            for r in range(NSA_REP)]


def _sel_win_kernel(ids_ref, cnt_ref, slopes_ref, q_ref, ks_ref, vs_ref, kw_ref, vw_ref, sel_ref,
                    ocmp_ref, gate_ref, z_ref, bias_ref, y_ref, m_ref, l_ref, acc_ref, *, n_chunks):
    b = pl.program_id(0)
    g = pl.program_id(1)
    qi = pl.program_id(2)
    nq = pl.num_programs(2)
    qb = Q_BLOCK
    q0 = qi * qb
    per_chunk = SEL_CHUNK // SLC_BLOCK

    q = _stack_heads(q_ref[0])
    slopes = [slopes_ref[g * NSA_REP + r] for r in range(NSA_REP)]

    m_ref[...] = jnp.full_like(m_ref, M_FLOOR)
    l_ref[...] = jnp.zeros_like(l_ref)
    acc_ref[...] = jnp.zeros_like(acc_ref)
    step = (b * NSA_GROUPS + g) * nq + qi
    n_active = cnt_ref[step]
    id_base = step * n_chunks

    def chunk_scores(c, threshold):
        off = pl.multiple_of(c * SEL_CHUNK, SEL_CHUNK)
        st = lax.dot_general(ks_ref[0, pl.ds(off, SEL_CHUNK), :], q, _NT,
                             preferred_element_type=F32)
        rows8 = sel_ref[0, 0, 0, pl.ds(pl.multiple_of(c * 8, 8), 8), :]
        chosen = jnp.concatenate(
            [jnp.broadcast_to(rows8[i:i + 1], (SLC_BLOCK, qb)) for i in range(per_chunk)],
            axis=0)
        rel = _iota((SEL_CHUNK, qb), 1) - _iota((SEL_CHUNK, qb), 0)
        causal = rel >= off - q0
        return (_masked_scores(st, (chosen > threshold) & causal, bias_ref),
                _chunk_shift(slopes, off - q0))

    def pair_update(state, ca, cb, scores_a, scores_b):
        m_old, l_old, acc_old = state
        (sa, shift_a), (sb, shift_b) = scores_a, scores_b
        top = lambda ss: jnp.concatenate([jnp.max(s, axis=0, keepdims=True) for s in ss], axis=1)
        m_new = jnp.maximum(m_old, jnp.maximum(top(sa) + shift_a, top(sb) + shift_b))
        ma = m_new - shift_a
        mb = m_new - shift_b
        pa = [jnp.exp2(sa[r] - ma[:, _head_cols(r)]) for r in range(NSA_REP)]
        pb = [jnp.exp2(sb[r] - mb[:, _head_cols(r)]) for r in range(NSA_REP)]
        alpha = jnp.exp2(m_old - m_new)
        l_new = alpha * l_old + jnp.concatenate(
            [jnp.sum(pa[r], axis=0, keepdims=True) + jnp.sum(pb[r], axis=0, keepdims=True)
             for r in range(NSA_REP)], axis=1)
        pta = jnp.concatenate([p.astype(BF16) for p in pa], axis=1)
        ptb = jnp.concatenate([p.astype(BF16) for p in pb], axis=1)
        values = lambda c: vs_ref[0, pl.ds(pl.multiple_of(c * SEL_CHUNK, SEL_CHUNK), SEL_CHUNK), :]
        acc_new = (alpha * acc_old
                   + lax.dot_general(values(ca), pta, _TN, preferred_element_type=F32)
                   + lax.dot_general(values(cb), ptb, _TN, preferred_element_type=F32))
        return m_new, l_new, acc_new

    def load_state():
        return m_ref[...], l_ref[...], acc_ref[...]

    def store_state(state):
        m_ref[...], l_ref[...], acc_ref[...] = state

    def quad_step(i, carry):
        cs = [ids_ref[id_base + 4 * i + k] for k in range(4)]
        scores = [chunk_scores(c, 0.5) for c in cs]
        state = pair_update(load_state(), cs[0], cs[1], scores[0], scores[1])
        store_state(pair_update(state, cs[2], cs[3], scores[2], scores[3]))
        return carry

    def pair_step(i, carry):
        ca = ids_ref[id_base + 2 * i]
        cb = ids_ref[id_base + 2 * i + 1]
        thr_b = jnp.where(2 * i + 1 < n_active, 0.5, 2.0)
        store_state(pair_update(load_state(), ca, cb, chunk_scores(ca, 0.5),
                                chunk_scores(cb, thr_b)))
        return carry

    n_quads = lax.shift_right_logical(n_active, 2)
    lax.fori_loop(0, n_quads, quad_step, 0)
    lax.fori_loop(2 * n_quads, lax.shift_right_logical(n_active + 1, 1), pair_step, 0)

    wk = WINDOW + qb
    start = pl.multiple_of(jnp.maximum(q0 - WINDOW, 0), qb)
    st = lax.dot_general(kw_ref[0, pl.ds(start, wk), :], q, _NT, preferred_element_type=F32)
    rel = _iota((wk, qb), 1) - _iota((wk, qb), 0)
    ss = _masked_scores(st, (rel >= start - q0) & (rel < WINDOW + start - q0), bias_ref)
    ps = [jnp.exp2(s - jnp.max(s, axis=0, keepdims=True)) for s in ss]
    l_win = jnp.concatenate([jnp.sum(p, axis=0, keepdims=True) for p in ps], axis=1)
    pt = jnp.concatenate([p.astype(BF16) for p in ps], axis=1)
    o_win = lax.dot_general(vw_ref[0, pl.ds(start, wk), :], pt, _TN, preferred_element_type=F32)

    gates = _sigmoid(gate_ref[0, 0])
    gate_row = lambda br: jnp.concatenate(
        [gates[3 * r + br:3 * r + br + 1] for r in range(NSA_REP)], axis=1)
    l_sel = l_ref[...]
    comb = (gate_row(0) * ocmp_ref[0, 0, 0]
            + gate_row(1) * (acc_ref[...] * jnp.where(l_sel > 0.0, 1.0 / l_sel, 0.0))
            + gate_row(2) * (o_win * (1.0 / l_win)))
    for r in range(NSA_REP):
        cols = slice(r * NSA_DH, (r + 1) * NSA_DH)
        z = z_ref[0, :, cols].astype(F32)
        y_ref[0, :, cols] = (z * _sigmoid(z) * comb[:, _head_cols(r)].T).astype(y_ref.dtype)


def _sel_win(ids, cnt, slopes, pa, sel, ocmp, gates_t):
    b, s, _ = pa.shape
    nq = s // Q_BLOCK
    gw = NSA_REP * NSA_DH
    q_cols = NSA_HEADS * NSA_DH // NSA_DH
    z_col = (NSA_HEADS * NSA_DH + 6 * NSA_GROUPS * NSA_DH) // gw
    kv = lambda slot: pl.BlockSpec(
        (1, s, NSA_DH), lambda bi, gi, qi, *_: (bi, 0, q_cols + slot * NSA_GROUPS + gi))
    per_q =lambda arr: pl.BlockSpec((1, 1, 1) + arr.shape[3:],
                                     lambda bi, gi, qi, *_: (bi, gi, qi, 0, 0))
    grid_spec = pltpu.PrefetchScalarGridSpec(
        num_scalar_prefetch=3,
        grid=(b, NSA_GROUPS, nq),
        in_specs=[
            pl.BlockSpec((1, Q_BLOCK, gw), lambda bi, gi, qi, *_: (bi, qi, gi)),
            kv(2), kv(3), kv(4), kv(5),
            per_q(sel), per_q(ocmp),
            pl.BlockSpec((1, 1, NSA_REP * 3, Q_BLOCK), lambda bi, gi, qi, *_: (bi, gi, 0, qi)),
            pl.BlockSpec((1, Q_BLOCK, gw), lambda bi, gi, qi, *_: (bi, qi, z_col + gi)),
            pl.BlockSpec((1, WINDOW + Q_BLOCK, gw), lambda bi, gi, qi, *_: (gi, 0, 0)),
        ],
        out_specs=pl.BlockSpec((1, Q_BLOCK, gw), lambda bi, gi, qi, *_: (bi, qi, gi)),
        scratch_shapes=[
            pltpu.VMEM((1, gw), F32),
            pltpu.VMEM((1, gw), F32),
            pltpu.VMEM((NSA_DH, gw), F32),
        ],
    )
    return pl.pallas_call(
        functools.partial(_sel_win_kernel, n_chunks=s // SEL_CHUNK),
        grid_spec=grid_spec,
        out_shape=jax.ShapeDtypeStruct((b, s, NSA_GROUPS * gw), BF16),
        compiler_params=pltpu.CompilerParams(
            dimension_semantics=("arbitrary", "arbitrary", "arbitrary"),
            vmem_limit_bytes=VMEM_LIMIT),
    )(ids, cnt, slopes, pa, pa, pa, pa, pa, sel, ocmp, gates_t, pa,
      _alibi_table(slopes, WINDOW + Q_BLOCK, 1))


def _active_chunk_lists(counts):
    b, g, nq, _, n_slc = counts.shape
    per_chunk = SEL_CHUNK // SLC_BLOCK
    n_chunks = n_slc // per_chunk
    chunk = jnp.arange(n_chunks, dtype=jnp.int32)
    causal = chunk[None, :] <= (jnp.arange(nq, dtype=jnp.int32)[:, None] * Q_BLOCK) // SEL_CHUNK
    active = counts[:, :, :, 0].reshape(b, g, nq, n_chunks, per_chunk).max(axis=-1) > 0.5
    active = active & causal
    ids = jnp.argsort(jnp.where(active, chunk, chunk + n_chunks), axis=-1).astype(jnp.int32)
    return ids.reshape(-1), active.sum(axis=-1, dtype=jnp.int32).reshape(-1)


def kernel(x, norm_g, ret_w_in, ret_w_out, nsa_w_in, nsa_cmp_pos_k, nsa_cmp_w1_k, nsa_cmp_w2_k,
           nsa_cmp_pos_v, nsa_cmp_w1_v, nsa_cmp_w2_v, nsa_w_out, final_g):
    b, s, d = x.shape
    m = b * s
    x2 = x.reshape(m, d)

    qk_w = 2 * RET_HEADS * RET_DK
    v_w = RET_HEADS * RET_DV
    w_in = ret_w_in[0].astype(BF16)
    scale = jnp.concatenate([jnp.ones((qk_w // 2,), F32), jnp.full((qk_w // 2,), RET_DK ** -0.5, F32),
                             jnp.ones((2 * v_w,), F32)])
    proj = _norm_proj(x2, norm_g[0], w_in, scale, BF16, tm=1024, tn=1024)
    y = _retention(proj.reshape(b, s, qk_w + 2 * v_w))
    h1, hn1 = _out_proj(y.reshape(m, v_w), ret_w_out[0].astype(BF16), x2, norm_g[1],
                        last_layer=False)

    width = NSA_HEADS * NSA_DH
    kvw = NSA_GROUPS * NSA_DH
    n_gate = NSA_HEADS * 3
    qkv_w = width + 6 * kvw
    w_n = nsa_w_in[0]
    w_a = jnp.concatenate([w_n[:, :qkv_w], w_n[:, qkv_w + n_gate:]], axis=1).astype(BF16)
    scale_a = jnp.concatenate([jnp.full((width,), NSA_DH ** -0.5 * LOG2E, F32),
                               jnp.ones((6 * kvw + width,), F32)])
    w_g = jnp.concatenate([w_n[:, qkv_w:qkv_w + n_gate], jnp.zeros((d, 128 - n_gate), F32)],
                          axis=1).astype(BF16)
    pa = _proj(hn1, w_a, scale_a, BF16, tm=min(2048, m), tn=1024)
    pa = pa.reshape(b, s, qkv_w + width)
    pg = _proj(hn1, w_g, jnp.ones((128,), F32), F32, tm=min(2048, m), tn=128)

    kv_cols = lambda slot: slice(width + slot * kvw, width + (slot + 1) * kvw)
    k_cmp = _compress(pa[:, :, kv_cols(0)], nsa_cmp_pos_k[0], nsa_cmp_w1_k[0], nsa_cmp_w2_k[0])
    v_cmp = _compress(pa[:, :, kv_cols(1)], nsa_cmp_pos_v[0], nsa_cmp_w1_v[0], nsa_cmp_w2_v[0])
    slopes = jnp.exp2(-8.0 * jnp.arange(1, NSA_HEADS + 1, dtype=F32) / NSA_HEADS) * LOG2E
    ocmp, sel, counts = _cmp_select(pa, k_cmp, v_cmp, slopes)
    ids, cnt = _active_chunk_lists(counts)
    gates_t = pg[:, :n_gate].reshape(b, s, NSA_GROUPS, NSA_REP * 3)
    gates_t = gates_t.transpose(0, 2, 3, 1)
    y = _sel_win(ids, cnt, slopes, pa, sel, ocmp, gates_t)
    out, = _out_proj(y.reshape(m, width), nsa_w_out[0].astype(BF16), h1, final_g, last_layer=True)
    return out.reshape(b, s, d)
```

```python
import functools

import jax
import jax.numpy as jnp
import numpy as np
from jax import lax
from jax.experimental import pallas as pl
from jax.experimental.pallas import tpu as pltpu

F32 = jnp.float32
BF16 = jnp.bfloat16

EPS = 1e-6
LOG2E = 1.4426950408889634
MASKED = -2e30
M_FLOOR = -1e30
TAKEN = -3e38
N_FORCED = 3

RET_HEADS = 8
RET_DK = 256
RET_DV = 512
RET_CHUNK = 256

NSA_HEADS = 16
NSA_GROUPS = 4
NSA_REP = 4
NSA_DH = 128
CMP_LEN = 32
CMP_STRIDE = 16
SLC_BLOCK = 64
TOP_N = 16
WINDOW = 512
Q_BLOCK = 128
SEL_CHUNK = 256

VMEM_LIMIT = 56 * 1024 * 1024

_NT = (((1,), (1,)), ((), ()))
_TN = (((0,), (0,)), ((), ()))


def _sigmoid(x):
    return 1.0 / (1.0 + jnp.exp(-x))


def _iota(shape, dim):
    return lax.broadcasted_iota(jnp.int32, shape, dim)


def _norm_proj_kernel(x_ref, g_ref, w_ref, s_ref, o_ref, xn_ref):
    @pl.when(pl.program_id(1) == 0)
    def _():
        x = x_ref[...]
        ms = jnp.mean(x * x, axis=-1, keepdims=True)
        xn_ref[...] = (x * lax.rsqrt(ms + EPS) * g_ref[...]).astype(xn_ref.dtype)

    acc = jnp.dot(xn_ref[...], w_ref[...], preferred_element_type=F32)
    o_ref[...] = (acc * s_ref[...]).astype(o_ref.dtype)


def _norm_proj(x2, g, w, col_scale, out_dtype, tm, tn):
    m, k = x2.shape
    n = w.shape[1]
    return pl.pallas_call(
        _norm_proj_kernel,
        grid=(m // tm, n // tn),
        in_specs=[
            pl.BlockSpec((tm, k), lambda i, j: (i, 0)),
            pl.BlockSpec((1, k), lambda i, j: (0, 0)),
            pl.BlockSpec((k, tn), lambda i, j: (0, j)),
            pl.BlockSpec((1, tn), lambda i, j: (0, j)),
        ],
        out_specs=pl.BlockSpec((tm, tn), lambda i, j: (i, j)),
        out_shape=jax.ShapeDtypeStruct((m, n), out_dtype),
        scratch_shapes=[pltpu.VMEM((tm, k), BF16)],
        compiler_params=pltpu.CompilerParams(
            dimension_semantics=("arbitrary", "arbitrary"), vmem_limit_bytes=VMEM_LIMIT),
    )(x2, g.reshape(1, k), w, col_scale.reshape(1, n))


def _proj_kernel(x_ref, w_ref, s_ref, o_ref):
    acc = jnp.dot(x_ref[...], w_ref[...], preferred_element_type=F32)
    o_ref[...] = (acc * s_ref[...]).astype(o_ref.dtype)


def _proj(x2, w, col_scale, out_dtype, tm, tn):
    m, k = x2.shape
    n = w.shape[1]
    return pl.pallas_call(
        _proj_kernel,
        grid=(m // tm, n // tn),
        in_specs=[
            pl.BlockSpec((tm, k), lambda i, j: (i, 0)),
            pl.BlockSpec((k, tn), lambda i, j: (0, j)),
            pl.BlockSpec((1, tn), lambda i, j: (0, j)),
        ],
        out_specs=pl.BlockSpec((tm, tn), lambda i, j: (i, j)),
        out_shape=jax.ShapeDtypeStruct((m, n), out_dtype),
        compiler_params=pltpu.CompilerParams(
            dimension_semantics=("arbitrary", "arbitrary"), vmem_limit_bytes=VMEM_LIMIT),
    )(x2, w, col_scale.reshape(1, n))


def _out_proj_kernel(y_ref, w_ref, res_ref, g_ref, *o_refs, last_layer):
    h = res_ref[...] + jnp.dot(y_ref[...], w_ref[...], preferred_element_type=F32)
    ms = jnp.mean(h * h, axis=-1, keepdims=True)
    hn = h * lax.rsqrt(ms + EPS) * g_ref[...]
    if last_layer:
        o_refs[0][...] = hn
    else:
        o_refs[0][...] = h
        o_refs[1][...] = hn.astype(o_refs[1].dtype)


def _out_proj(y2, w, res2, g, last_layer, tm=512):
    m, kd = y2.shape
    n = w.shape[1]
    row_blk = pl.BlockSpec((tm, n), lambda i: (i, 0))
    out_shape = [jax.ShapeDtypeStruct((m, n), F32)]
    if not last_layer:
        out_shape.append(jax.ShapeDtypeStruct((m, n), BF16))
    return pl.pallas_call(
        functools.partial(_out_proj_kernel, last_layer=last_layer),
        grid=(m // tm,),
        in_specs=[
            pl.BlockSpec((tm, kd), lambda i: (i, 0)),
            pl.BlockSpec((kd, n), lambda i: (0, 0), pipeline_mode=pl.Buffered(1)),
            row_blk,
            pl.BlockSpec((1, n), lambda i: (0, 0)),
        ],
        out_specs=[row_blk] * len(out_shape),
        out_shape=out_shape,
        compiler_params=pltpu.CompilerParams(
            dimension_semantics=("arbitrary",), vmem_limit_bytes=VMEM_LIMIT),
    )(y2, w, res2, g.reshape(1, n))


def _retention_kernel(q_ref, k_ref, v_ref, gate_ref, di_ref, qd_ref, kd_ref, cd_ref, y_ref,
                      state_ref, *, chunk, n_sub):
    @pl.when(pl.program_id(2) == 0)
    def _():
        state_ref[...] = jnp.zeros_like(state_ref)

    di = di_ref[0]
    qd = qd_ref[0]
    kd = kd_ref[0]
    cd = cd_ref[0]
    for i in range(n_sub):
        rows = pl.ds(i * chunk, chunk)
        q = q_ref[0, rows, :]
        k = k_ref[0, rows, :]
        v = v_ref[0, rows, :]
        state = state_ref[...]
        s = lax.dot_general(q, k, _NT, preferred_element_type=F32) * di
        o = jnp.dot(s.astype(BF16), v, preferred_element_type=F32)
        o = o + jnp.dot(q, state.astype(BF16), preferred_element_type=F32) * qd
        k_dec = (k.astype(F32) * kd).astype(BF16)
        state_ref[...] = state * cd + lax.dot_general(k_dec, v, _TN, preferred_element_type=F32)
        mu = jnp.mean(o, axis=-1, keepdims=True)
        oc = o - mu
        var = jnp.mean(oc * oc, axis=-1, keepdims=True)
        on = oc * lax.rsqrt(var + EPS)
        gt = gate_ref[0, rows, :].astype(F32)
        y_ref[0, rows, :] = (gt * _sigmoid(gt) * on).astype(y_ref.dtype)


def _retention(proj, tokens_per_step=1024):
    b, s, _ = proj.shape
    h, c = RET_HEADS, RET_CHUNK
    t = min(tokens_per_step, s)
    log_g = jnp.log1p(-jnp.exp2(-5.0 - jnp.arange(h, dtype=F32)))
    idx = jnp.arange(c, dtype=F32)
    diff = idx[:, None] - idx[None, :]
    decay_intra = jnp.where(diff >= 0, jnp.exp(diff[None] * log_g[:, None, None]), 0.0)
    q_decay = jnp.exp((idx[None, :] + 1.0) * log_g[:, None])[:, :, None]
    k_decay = jnp.exp((c - 1.0 - idx[None, :]) * log_g[:, None])[:, :, None]
    chunk_decay = jnp.exp(c * log_g)[:, None, None]
    nk = RET_HEADS * RET_DK // RET_DK
    nv = 2 * RET_HEADS * RET_DK // RET_DV
    return pl.pallas_call(
        functools.partial(_retention_kernel, chunk=c, n_sub=t // c),
        grid=(b, h, s // t),
        in_specs=[
            pl.BlockSpec((1, t, RET_DK), lambda bi, hi, ti: (bi, ti, hi)),
            pl.BlockSpec((1, t, RET_DK), lambda bi, hi, ti: (bi, ti, nk + hi)),
            pl.BlockSpec((1, t, RET_DV), lambda bi, hi, ti: (bi, ti, nv + hi)),
            pl.BlockSpec((1, t, RET_DV), lambda bi, hi, ti: (bi, ti, nv + h + hi)),
            pl.BlockSpec((1, c, c), lambda bi, hi, ti: (hi, 0, 0)),
            pl.BlockSpec((1, c, 1), lambda bi, hi, ti: (hi, 0, 0)),
            pl.BlockSpec((1, c, 1), lambda bi, hi, ti: (hi, 0, 0)),
            pl.BlockSpec((1, 1, 1), lambda bi, hi, ti: (hi, 0, 0)),
        ],
        out_specs=pl.BlockSpec((1, t, RET_DV), lambda bi, hi, ti: (bi, ti, hi)),
        out_shape=jax.ShapeDtypeStruct((b, s, h * RET_DV), BF16),
        scratch_shapes=[pltpu.VMEM((RET_DK, RET_DV), F32)],
        compiler_params=pltpu.CompilerParams(
            dimension_semantics=("arbitrary", "arbitrary", "arbitrary"),
            vmem_limit_bytes=VMEM_LIMIT),
    )(proj, proj, proj, proj, decay_intra, q_decay, k_decay, chunk_decay)


def _compress_kernel(x_ref, xn_ref, pos_ref, w1_ref, w2_ref, o_ref):
    half = CMP_LEN // 2
    width = NSA_GROUPS * NSA_DH
    for g in range(NSA_GROUPS):
        acc = jnp.zeros((x_ref.shape[1], NSA_DH), F32)
        for l in range(half):
            cols = slice(l * width + g * NSA_DH, l * width + (g + 1) * NSA_DH)
            xa = (x_ref[0, :, cols].astype(F32) + pos_ref[l:l + 1, :]).astype(BF16)
            acc += jnp.dot(xa, w1_ref[l * NSA_DH:(l + 1) * NSA_DH, :], preferred_element_type=F32)
            lb = half + l
            xb = (xn_ref[0, :, cols].astype(F32) + pos_ref[lb:lb + 1, :]).astype(BF16)
            acc += jnp.dot(xb, w1_ref[lb * NSA_DH:(lb + 1) * NSA_DH, :], preferred_element_type=F32)
        hid = (acc * _sigmoid(acc)).astype(BF16)
        out = jnp.dot(hid, w2_ref[...], preferred_element_type=F32)
        o_ref[0, :, g * NSA_DH:(g + 1) * NSA_DH] = out.astype(o_ref.dtype)


def _compress(raw, pos, w1, w2):
    b, s, width = raw.shape
    rows = s // CMP_STRIDE
    x = raw.reshape(b, rows, CMP_STRIDE * width)
    xn = jnp.concatenate([x[:, 1:], jnp.zeros_like(x[:, :1])], axis=1)
    tr = min(256, rows)
    blk = pl.BlockSpec((1, tr, CMP_STRIDE * width), lambda bi, ri: (bi, ri, 0))
    return pl.pallas_call(
        _compress_kernel,
        grid=(b, rows // tr),
        in_specs=[
            blk, blk,
            pl.BlockSpec((CMP_LEN, NSA_DH), lambda bi, ri: (0, 0)),
            pl.BlockSpec((CMP_LEN * NSA_DH, NSA_DH), lambda bi, ri: (0, 0)),
            pl.BlockSpec((NSA_DH, NSA_DH), lambda bi, ri: (0, 0)),
        ],
        out_specs=pl.BlockSpec((1, tr, width), lambda bi, ri: (bi, ri, 0)),
        out_shape=jax.ShapeDtypeStruct((b, rows, width), BF16),
        compiler_params=pltpu.CompilerParams(
            dimension_semantics=("arbitrary", "arbitrary"), vmem_limit_bytes=VMEM_LIMIT),
    )(x, xn, pos, w1.astype(BF16), w2.astype(BF16))


def _stack_heads(qblk):
    return jnp.concatenate(
        [qblk[:, r * NSA_DH:(r + 1) * NSA_DH] for r in range(NSA_REP)], axis=0)


def _head_cols(r):
    return slice(r * Q_BLOCK, (r + 1) * Q_BLOCK)


def _alibi_table(slopes, n_keys, key_stride):
    j = jnp.arange(n_keys, dtype=F32)[:, None] * key_stride
    i = jnp.arange(Q_BLOCK, dtype=F32)[None, :]
    table = slopes.reshape(NSA_GROUPS, 1, NSA_REP, 1) * (j - i)[None, :, None, :]
    return table.reshape(NSA_GROUPS, n_keys, NSA_REP * Q_BLOCK)


def _chunk_shift(slopes, delta):
    row = jnp.full((1, Q_BLOCK), delta, jnp.int32).astype(F32)
    return jnp.concatenate([s * row for s in slopes], axis=1)


def _cmp_select_kernel(slopes_ref, q_ref, kc_ref, vct_ref, ovt_ref, spread_ref, bias_ref, ocmp_ref,
                       sel_ref, cnt_ref, s_scr, *, ck, nq):
    g = pl.program_id(1)
    qi = pl.program_id(2)
    q0 = qi * Q_BLOCK
    qb = Q_BLOCK
    n_ck, n_slc = ovt_ref.shape[0], ovt_ref.shape[1]

    q = _stack_heads(q_ref[0])
    slopes = [slopes_ref[g * NSA_REP + r] for r in range(NSA_REP)]
    rel = _iota((ck, qb), 1) - CMP_STRIDE * _iota((ck, qb), 0)
    chunk_delta = lambda off: CMP_STRIDE * off + (CMP_LEN - 1) - q0

    def attend_and_select(chunks, rows):
        tops = []
        for c in range(chunks):
            off = c * ck
            st = lax.dot_general(kc_ref[0, off:off + ck, :], q, _NT, preferred_element_type=F32)
            valid = rel >= chunk_delta(off)
            top_c = []
            for r in range(NSA_REP):
                s = jnp.where(valid, st[:, _head_cols(r)] + bias_ref[0, :, _head_cols(r)], MASKED)
                s_scr[off:off + ck, _head_cols(r)] = s
                top_c.append(jnp.max(s, axis=0, keepdims=True))
            tops.append(jnp.concatenate(top_c, axis=1) + _chunk_shift(slopes, chunk_delta(off)))
        m_all = functools.reduce(jnp.maximum, tops, jnp.full((1, NSA_REP * qb), M_FLOOR, F32))

        l = jnp.zeros((1, NSA_REP * qb), F32)
        o = jnp.zeros((NSA_DH, NSA_REP * qb), F32)
        imp = jnp.zeros((rows, NSA_REP * qb), F32)
        for c in range(chunks):
            off = c * ck
            shifted_m = m_all - _chunk_shift(slopes, chunk_delta(off))
            p = jnp.exp2(s_scr[off:off + ck, :] - shifted_m)
            pb = p.astype(BF16)
            l = l + jnp.sum(p, axis=0, keepdims=True)
            o = o + jnp.dot(vct_ref[0, 0, c], pb, preferred_element_type=F32)
            imp = imp + jnp.dot(ovt_ref[c, :rows, :], pb, preferred_element_type=F32)
        inv = jnp.where(l > 0.0, 1.0 / l, 0.0)
        ocmp_ref[0, 0, 0] = o * inv
        imp_w = imp * inv
        imp_t = imp_w[:, _head_cols(0)]
        for r in range(1, NSA_REP):
            imp_t = imp_t + imp_w[:, _head_cols(r)]

        jrow = _iota((rows, qb), 0)
        cur = lax.shift_right_logical(q0 + _iota((rows, qb), 1), int(np.log2(SLC_BLOCK)))
        forced = (jrow == 0) | (jrow == cur) | (jrow == cur - 1)
        score = jnp.where(forced, TAKEN, jnp.where(jrow <= cur, imp_t, -1.0))
        for _ in range(min(TOP_N, rows) - N_FORCED):
            best = jnp.max(score, axis=0, keepdims=True)
            first = jnp.min(jnp.where(score == best, jrow, rows), axis=0, keepdims=True)
            score = jnp.where(jrow == first, TAKEN, score)
        chosen = jnp.where(score == TAKEN, 1.0, 0.0).astype(BF16)
        sel_ref[0, 0, 0] = jnp.dot(spread_ref[:, :rows], chosen, preferred_element_type=F32)
        cnt_ref[0, 0, 0, :, :rows] = lax.dot_general(jnp.ones((8, qb), BF16), chosen, _NT,
                                                     preferred_element_type=F32)
        if rows < n_slc:
            cnt_ref[0, 0, 0, :, rows:] = jnp.zeros((8, n_slc - rows), F32)

    if nq % n_ck == 0 and n_slc % n_ck == 0:
        for k in range(1, n_ck + 1):
            in_share = (qi >= (k - 1) * (nq // n_ck)) & (qi < k * (nq // n_ck))
            pl.when(in_share)(functools.partial(attend_and_select, k, k * (n_slc // n_ck)))
    else:
        attend_and_select(n_ck, n_slc)


def _cmp_select(pa, k_cmp, v_cmp, slopes):
    b, s, _ = pa.shape
    nq = s // Q_BLOCK
    n_cmp = k_cmp.shape[1]
    n_slc = s // SLC_BLOCK
    ck = min(256, n_cmp)
    n_ck = n_cmp // ck
    gw = NSA_REP * NSA_DH
    cstart = np.arange(n_cmp)[None, :] * CMP_STRIDE
    sstart = np.arange(n_slc)[:, None] * SLC_BLOCK
    real = np.arange(n_cmp)[None, :] < (s - CMP_LEN) // CMP_STRIDE + 1
    ovt = ((cstart < sstart + SLC_BLOCK) & (cstart + CMP_LEN > sstart) & real).astype(np.float32)
    ovt = jnp.asarray(ovt.reshape(n_slc, n_ck, ck).transpose(1, 0, 2), BF16)
    per_chunk = SEL_CHUNK // SLC_BLOCK
    n_sel_chunks = s // SEL_CHUNK
    spread = np.zeros((8 * n_sel_chunks, n_slc), np.float32)
    for j in range(n_slc):
        spread[8 * (j // per_chunk) + j % per_chunk, j] = 1.0
    spread = jnp.asarray(spread, BF16)
    vct = v_cmp.reshape(b, n_ck, ck, NSA_GROUPS, NSA_DH).transpose(0, 3, 1, 4, 2)
    bias = _alibi_table(slopes, ck, CMP_STRIDE)
    return pl.pallas_call(
        functools.partial(_cmp_select_kernel, ck=ck, nq=nq),
        grid=(b, NSA_GROUPS, nq),
        in_specs=[
            pl.BlockSpec(memory_space=pltpu.SMEM),
            pl.BlockSpec((1, Q_BLOCK, gw), lambda bi, gi, qi: (bi, qi, gi)),
            pl.BlockSpec((1, n_cmp, NSA_DH), lambda bi, gi, qi: (bi, 0, gi)),
            pl.BlockSpec((1, 1, n_ck, NSA_DH, ck), lambda bi, gi, qi: (bi, gi, 0, 0, 0)),
            pl.BlockSpec((n_ck, n_slc, ck), lambda bi, gi, qi: (0, 0, 0)),
            pl.BlockSpec((8 * n_sel_chunks, n_slc), lambda bi, gi, qi: (0, 0)),
            pl.BlockSpec((1, ck, gw), lambda bi, gi, qi: (gi, 0, 0)),
        ],
        out_specs=[
            pl.BlockSpec((1, 1, 1, NSA_DH, gw), lambda bi, gi, qi: (bi, gi, qi, 0, 0)),
            pl.BlockSpec((1, 1, 1, 8 * n_sel_chunks, Q_BLOCK), lambda bi, gi, qi: (bi, gi, qi, 0, 0)),
            pl.BlockSpec((1, 1, 1, 8, n_slc), lambda bi, gi, qi: (bi, gi, qi, 0, 0)),
        ],
        out_shape=[
            jax.ShapeDtypeStruct((b, NSA_GROUPS, nq, NSA_DH, gw), F32),
            jax.ShapeDtypeStruct((b, NSA_GROUPS, nq, 8 * n_sel_chunks, Q_BLOCK), F32),
            jax.ShapeDtypeStruct((b, NSA_GROUPS, nq, 8, n_slc), F32),
        ],
        scratch_shapes=[pltpu.VMEM((n_cmp, gw), F32)],
        compiler_params=pltpu.CompilerParams(
            dimension_semantics=("arbitrary", "arbitrary", "arbitrary"),
            vmem_limit_bytes=VMEM_LIMIT),
    )(slopes, pa, k_cmp, vct, ovt, spread, bias)


def _sel_win_kernel(ids_ref, cnt_ref, slopes_ref, q_ref, ks_ref, vs_ref, kw_ref, vw_ref, sel_ref,
                    ocmp_ref, gate_ref, z_ref, bias_ref, y_ref, m_ref, l_ref, acc_ref, s_scr, *,
                    n_chunks):
    b = pl.program_id(0)
    g = pl.program_id(1)
    qi = pl.program_id(2)
    nq = pl.num_programs(2)
    qb = Q_BLOCK
    q0 = qi * qb
    per_chunk = SEL_CHUNK // SLC_BLOCK

    q = _stack_heads(q_ref[0])
    slopes = [slopes_ref[g * NSA_REP + r] for r in range(NSA_REP)]

    m_ref[...] = jnp.full_like(m_ref, M_FLOOR)
    l_ref[...] = jnp.zeros_like(l_ref)
    acc_ref[...] = jnp.zeros_like(acc_ref)
    step = (b * NSA_GROUPS + g) * nq + qi
    n_active = cnt_ref[step]
    id_base = step * n_chunks

    def stage_scores(row0, st, mask):
        n = st.shape[0]
        tops = []
        for r in range(NSA_REP):
            s = jnp.where(mask, st[:, _head_cols(r)] + bias_ref[0, :n, _head_cols(r)], MASKED)
            s_scr[row0:row0 + n, _head_cols(r)] = s
            tops.append(jnp.max(s, axis=0, keepdims=True))
        return jnp.concatenate(tops, axis=1)

    def chunk_scores(slot, c, threshold):
        off = pl.multiple_of(c * SEL_CHUNK, SEL_CHUNK)
        st = lax.dot_general(ks_ref[0, pl.ds(off, SEL_CHUNK), :], q, _NT,
                             preferred_element_type=F32)
        rows8 = sel_ref[0, 0, 0, pl.ds(pl.multiple_of(c * 8, 8), 8), :]
        chosen = jnp.concatenate(
            [jnp.broadcast_to(rows8[i:i + 1], (SLC_BLOCK, qb)) for i in range(per_chunk)],
            axis=0)
        rel = _iota((SEL_CHUNK, qb), 1) - _iota((SEL_CHUNK, qb), 0)
        shift = _chunk_shift(slopes, off - q0)
        top = stage_scores(slot * SEL_CHUNK, st, (chosen > threshold) & (rel >= off - q0))
        return top + shift, shift

    def update(chunks, thresholds):
        staged = [chunk_scores(k, c, thr) for k, (c, thr) in enumerate(zip(chunks, thresholds))]
        m_old = m_ref[...]
        m_new = functools.reduce(jnp.maximum, [top for top, _ in staged], m_old)
        alpha = jnp.exp2(m_old - m_new)
        l = alpha * l_ref[...]
        acc = alpha * acc_ref[...]
        for k, (c, (_, shift)) in enumerate(zip(chunks, staged)):
            p = jnp.exp2(s_scr[k * SEL_CHUNK:(k + 1) * SEL_CHUNK, :] - (m_new - shift))
            l = l + jnp.sum(p, axis=0, keepdims=True)
            v = vs_ref[0, pl.ds(pl.multiple_of(c * SEL_CHUNK, SEL_CHUNK), SEL_CHUNK), :]
            acc = acc + lax.dot_general(v, p.astype(BF16), _TN, preferred_element_type=F32)
        m_ref[...] = m_new
        l_ref[...] = l
        acc_ref[...] = acc

    def quad_step(i, carry):
        update([ids_ref[id_base + 4 * i + k] for k in range(4)], [0.5] * 4)
        return carry

    def pair_step(i, carry):
        update([ids_ref[id_base + 2 * i], ids_ref[id_base + 2 * i + 1]],
               [0.5, jnp.where(2 * i + 1 < n_active, 0.5, 2.0)])
        return carry

    n_quads = lax.shift_right_logical(n_active, 2)
    lax.fori_loop(0, n_quads, quad_step, 0)
    lax.fori_loop(2 * n_quads, lax.shift_right_logical(n_active + 1, 1), pair_step, 0)

    wk = WINDOW + qb
    start = pl.multiple_of(jnp.maximum(q0 - WINDOW, 0), qb)
    st = lax.dot_general(kw_ref[0, pl.ds(start, wk), :], q, _NT, preferred_element_type=F32)
    rel = _iota((wk, qb), 1) - _iota((wk, qb), 0)
    m_win = stage_scores(0, st, (rel >= start - q0) & (rel < WINDOW + start - q0))
    p = jnp.exp2(s_scr[0:wk, :] - m_win)
    l_win = jnp.sum(p, axis=0, keepdims=True)
    o_win = lax.dot_general(vw_ref[0, pl.ds(start, wk), :], p.astype(BF16), _TN,
                            preferred_element_type=F32)

    gates = _sigmoid(gate_ref[0, 0])
    gate_row = lambda br: jnp.concatenate(
        [gates[3 * r + br:3 * r + br + 1] for r in range(NSA_REP)], axis=1)
    l_sel = l_ref[...]
    comb = (gate_row(0) * ocmp_ref[0, 0, 0]
            + gate_row(1) * (acc_ref[...] * jnp.where(l_sel > 0.0, 1.0 / l_sel, 0.0))
            + gate_row(2) * (o_win * (1.0 / l_win)))
    for r in range(NSA_REP):
        cols = slice(r * NSA_DH, (r + 1) * NSA_DH)
        z = z_ref[0, :, cols].astype(F32)
        y_ref[0, :, cols] = (z * _sigmoid(z) * comb[:, _head_cols(r)].T).astype(y_ref.dtype)


def _sel_win(ids, cnt, slopes, pa, sel, ocmp, gates_t):
    b, s, _ = pa.shape
    nq = s // Q_BLOCK
    gw = NSA_REP * NSA_DH
    q_cols = NSA_HEADS * NSA_DH // NSA_DH
    z_col = (NSA_HEADS * NSA_DH + 6 * NSA_GROUPS * NSA_DH) // gw
    kv = lambda slot: pl.BlockSpec(
        (1, s, NSA_DH), lambda bi, gi, qi, *_: (bi, 0, q_cols + slot * NSA_GROUPS + gi))
    per_q =lambda arr: pl.BlockSpec((1, 1, 1) + arr.shape[3:],
                                     lambda bi, gi, qi, *_: (bi, gi, qi, 0, 0))
    grid_spec = pltpu.PrefetchScalarGridSpec(
        num_scalar_prefetch=3,
        grid=(b, NSA_GROUPS, nq),
        in_specs=[
            pl.BlockSpec((1, Q_BLOCK, gw), lambda bi, gi, qi, *_: (bi, qi, gi)),
            kv(2), kv(3), kv(4), kv(5),
            per_q(sel), per_q(ocmp),
            pl.BlockSpec((1, 1, NSA_REP * 3, Q_BLOCK), lambda bi, gi, qi, *_: (bi, gi, 0, qi)),
            pl.BlockSpec((1, Q_BLOCK, gw), lambda bi, gi, qi, *_: (bi, qi, z_col + gi)),
            pl.BlockSpec((1, WINDOW + Q_BLOCK, gw), lambda bi, gi, qi, *_: (gi, 0, 0)),
        ],
        out_specs=pl.BlockSpec((1, Q_BLOCK, gw), lambda bi, gi, qi, *_: (bi, qi, gi)),
        scratch_shapes=[
            pltpu.VMEM((1, gw), F32),
            pltpu.VMEM((1, gw), F32),
            pltpu.VMEM((NSA_DH, gw), F32),
            pltpu.VMEM((max(4 * SEL_CHUNK, WINDOW + Q_BLOCK), gw), F32),
        ],
    )
    return pl.pallas_call(
        functools.partial(_sel_win_kernel, n_chunks=s // SEL_CHUNK),
        grid_spec=grid_spec,
        out_shape=jax.ShapeDtypeStruct((b, s, NSA_GROUPS * gw), BF16),
        compiler_params=pltpu.CompilerParams(
            dimension_semantics=("arbitrary", "arbitrary", "arbitrary"),
            vmem_limit_bytes=VMEM_LIMIT),
    )(ids, cnt, slopes, pa, pa, pa, pa, pa, sel, ocmp, gates_t, pa,
      _alibi_table(slopes, WINDOW + Q_BLOCK, 1))


def _active_chunk_lists(counts):
    b, g, nq, _, n_slc = counts.shape
    per_chunk = SEL_CHUNK // SLC_BLOCK
    n_chunks = n_slc // per_chunk
    chunk = jnp.arange(n_chunks, dtype=jnp.int32)
    causal = chunk[None, :] <= (jnp.arange(nq, dtype=jnp.int32)[:, None] * Q_BLOCK) // SEL_CHUNK
    active = counts[:, :, :, 0].reshape(b, g, nq, n_chunks, per_chunk).max(axis=-1) > 0.5
    active = active & causal
    ids = jnp.argsort(jnp.where(active, chunk, chunk + n_chunks), axis=-1).astype(jnp.int32)
    return ids.reshape(-1), active.sum(axis=-1, dtype=jnp.int32).reshape(-1)


def kernel(x, norm_g, ret_w_in, ret_w_out, nsa_w_in, nsa_cmp_pos_k, nsa_cmp_w1_k, nsa_cmp_w2_k,
           nsa_cmp_pos_v, nsa_cmp_w1_v, nsa_cmp_w2_v, nsa_w_out, final_g):
    b, s, d = x.shape
    m = b * s
    x2 = x.reshape(m, d)

    qk_w = 2 * RET_HEADS * RET_DK
    v_w = RET_HEADS * RET_DV
    w_in = ret_w_in[0].astype(BF16)
    scale = jnp.concatenate([jnp.ones((qk_w // 2,), F32), jnp.full((qk_w // 2,), RET_DK ** -0.5, F32),
                             jnp.ones((2 * v_w,), F32)])
    proj = _norm_proj(x2, norm_g[0], w_in, scale, BF16, tm=1024, tn=1024)
    y = _retention(proj.reshape(b, s, qk_w + 2 * v_w))
    h1, hn1 = _out_proj(y.reshape(m, v_w), ret_w_out[0].astype(BF16), x2, norm_g[1],
                        last_layer=False)

    width = NSA_HEADS * NSA_DH
    kvw = NSA_GROUPS * NSA_DH
    n_gate = NSA_HEADS * 3
    qkv_w = width + 6 * kvw
    w_n = nsa_w_in[0]
    w_a = jnp.concatenate([w_n[:, :qkv_w], w_n[:, qkv_w + n_gate:]], axis=1).astype(BF16)
    scale_a = jnp.concatenate([jnp.full((width,), NSA_DH ** -0.5 * LOG2E, F32),
                               jnp.ones((6 * kvw + width,), F32)])
    w_g = jnp.concatenate([w_n[:, qkv_w:qkv_w + n_gate], jnp.zeros((d, 128 - n_gate), F32)],
                          axis=1).astype(BF16)
    pa = _proj(hn1, w_a, scale_a, BF16, tm=min(2048, m), tn=1024)
    pa = pa.reshape(b, s, qkv_w + width)
    pg = _proj(hn1, w_g, jnp.ones((128,), F32), F32, tm=min(2048, m), tn=128)

    kv_cols = lambda slot: slice(width + slot * kvw, width + (slot + 1) * kvw)
    k_cmp = _compress(pa[:, :, kv_cols(0)], nsa_cmp_pos_k[0], nsa_cmp_w1_k[0], nsa_cmp_w2_k[0])
    v_cmp = _compress(pa[:, :, kv_cols(1)], nsa_cmp_pos_v[0], nsa_cmp_w1_v[0], nsa_cmp_w2_v[0])
    slopes = jnp.exp2(-8.0 * jnp.arange(1, NSA_HEADS + 1, dtype=F32) / NSA_HEADS) * LOG2E
    ocmp, sel, counts = _cmp_select(pa, k_cmp, v_cmp, slopes)
    ids, cnt = _active_chunk_lists(counts)
    gates_t = pg[:, :n_gate].reshape(b, s, NSA_GROUPS, NSA_REP * 3)
    gates_t = gates_t.transpose(0, 2, 3, 1)
    y = _sel_win(ids, cnt, slopes, pa, sel, ocmp, gates_t)
    out, = _out_proj(y.reshape(m, width), nsa_w_out[0].astype(BF16), h1, final_g, last_layer=True)
    return out.reshape(b, s, d)
```

```python
import functools

import jax
import jax.numpy as jnp
import numpy as np
from jax import lax
from jax.experimental import pallas as pl
from jax.experimental.pallas import tpu as pltpu

F32 = jnp.float32
BF16 = jnp.bfloat16

EPS = 1e-6
LOG2E = 1.4426950408889634
MASKED = -2e30
M_FLOOR = -1e30
TAKEN = -3e38
N_FORCED = 3

RET_HEADS = 8
RET_DK = 256
RET_DV = 512
RET_CHUNK = 256

NSA_HEADS = 16
NSA_GROUPS = 4
NSA_REP = 4
NSA_DH = 128
CMP_LEN = 32
CMP_STRIDE = 16
SLC_BLOCK = 64
TOP_N = 16
WINDOW = 512
Q_BLOCK = 128
SEL_CHUNK = 256
SEL_PIECE = 128
BIG_STEP = 8
SMALL_STEP = 4

VMEM_LIMIT = 56 * 1024 * 1024

_NT = (((1,), (1,)), ((), ()))
_TN = (((0,), (0,)), ((), ()))


def _sigmoid(x):
    return 1.0 / (1.0 + jnp.exp2(-LOG2E * x))


def _iota(shape, dim):
    return lax.broadcasted_iota(jnp.int32, shape, dim)


def _norm_proj_kernel(x_ref, g_ref, w_ref, s_ref, o_ref, xn_ref):
    @pl.when(pl.program_id(1) == 0)
    def _():
        x = x_ref[...]
        ms = jnp.mean(x * x, axis=-1, keepdims=True)
        xn_ref[...] = (x * lax.rsqrt(ms + EPS) * g_ref[...]).astype(xn_ref.dtype)

    acc = jnp.dot(xn_ref[...], w_ref[...], preferred_element_type=F32)
    o_ref[...] = (acc * s_ref[...]).astype(o_ref.dtype)


def _norm_proj(x2, g, w, col_scale, out_dtype, tm, tn):
    m, k = x2.shape
    n = w.shape[1]
    return pl.pallas_call(
        _norm_proj_kernel,
        grid=(m // tm, n // tn),
        in_specs=[
            pl.BlockSpec((tm, k), lambda i, j: (i, 0)),
            pl.BlockSpec((1, k), lambda i, j: (0, 0)),
            pl.BlockSpec((k, tn), lambda i, j: (0, j)),
            pl.BlockSpec((1, tn), lambda i, j: (0, j)),
        ],
        out_specs=pl.BlockSpec((tm, tn), lambda i, j: (i, j)),
        out_shape=jax.ShapeDtypeStruct((m, n), out_dtype),
        scratch_shapes=[pltpu.VMEM((tm, k), BF16)],
        compiler_params=pltpu.CompilerParams(
            dimension_semantics=("arbitrary", "arbitrary"), vmem_limit_bytes=VMEM_LIMIT),
    )(x2, g.reshape(1, k), w, col_scale.reshape(1, n))


def _proj_kernel(x_ref, w_ref, s_ref, o_ref):
    acc = jnp.dot(x_ref[...], w_ref[...], preferred_element_type=F32)
    o_ref[...] = (acc * s_ref[...]).astype(o_ref.dtype)


def _proj(x2, w, col_scale, out_dtype, tm, tn):
    m, k = x2.shape
    n = w.shape[1]
    return pl.pallas_call(
        _proj_kernel,
        grid=(m // tm, n // tn),
        in_specs=[
            pl.BlockSpec((tm, k), lambda i, j: (i, 0)),
            pl.BlockSpec((k, tn), lambda i, j: (0, j)),
            pl.BlockSpec((1, tn), lambda i, j: (0, j)),
        ],
        out_specs=pl.BlockSpec((tm, tn), lambda i, j: (i, j)),
        out_shape=jax.ShapeDtypeStruct((m, n), out_dtype),
        compiler_params=pltpu.CompilerParams(
            dimension_semantics=("arbitrary", "arbitrary"), vmem_limit_bytes=VMEM_LIMIT),
    )(x2, w, col_scale.reshape(1, n))


def _out_proj_kernel(y_ref, w_ref, res_ref, g_ref, *o_refs, last_layer):
    h = res_ref[...] + jnp.dot(y_ref[...], w_ref[...], preferred_element_type=F32)
    ms = jnp.mean(h * h, axis=-1, keepdims=True)
    hn = h * lax.rsqrt(ms + EPS) * g_ref[...]
    if last_layer:
        o_refs[0][...] = hn
    else:
        o_refs[0][...] = h
        o_refs[1][...] = hn.astype(o_refs[1].dtype)


def _out_proj(y2, w, res2, g, last_layer, tm=512):
    m, kd = y2.shape
    n = w.shape[1]
    row_blk = pl.BlockSpec((tm, n), lambda i: (i, 0))
    out_shape = [jax.ShapeDtypeStruct((m, n), F32)]
    if not last_layer:
        out_shape.append(jax.ShapeDtypeStruct((m, n), BF16))
    return pl.pallas_call(
        functools.partial(_out_proj_kernel, last_layer=last_layer),
        grid=(m // tm,),
        in_specs=[
            pl.BlockSpec((tm, kd), lambda i: (i, 0)),
            pl.BlockSpec((kd, n), lambda i: (0, 0), pipeline_mode=pl.Buffered(1)),
            row_blk,
            pl.BlockSpec((1, n), lambda i: (0, 0)),
        ],
        out_specs=[row_blk] * len(out_shape),
        out_shape=out_shape,
        compiler_params=pltpu.CompilerParams(
            dimension_semantics=("arbitrary",), vmem_limit_bytes=VMEM_LIMIT),
    )(y2, w, res2, g.reshape(1, n))


def _retention_kernel(q_ref, k_ref, v_ref, gate_ref, di_ref, qd_ref, kd_ref, cd_ref, y_ref,
                      state_ref, *, chunk, n_sub):
    @pl.when(pl.program_id(2) == 0)
    def _():
        state_ref[...] = jnp.zeros_like(state_ref)

    di = di_ref[0]
    qd = qd_ref[0]
    kd = kd_ref[0]
    cd = cd_ref[0]
    for i in range(n_sub):
        rows = pl.ds(i * chunk, chunk)
        q = q_ref[0, rows, :]
        k = k_ref[0, rows, :]
        v = v_ref[0, rows, :]
        state = state_ref[...]
        s = lax.dot_general(q, k, _NT, preferred_element_type=F32) * di
        o = jnp.dot(s.astype(BF16), v, preferred_element_type=F32)
        o = o + jnp.dot(q, state.astype(BF16), preferred_element_type=F32) * qd
        k_dec = (k.astype(F32) * kd).astype(BF16)
        state_ref[...] = state * cd + lax.dot_general(k_dec, v, _TN, preferred_element_type=F32)
        mu = jnp.mean(o, axis=-1, keepdims=True)
        oc = o - mu
        var = jnp.mean(oc * oc, axis=-1, keepdims=True)
        on = oc * lax.rsqrt(var + EPS)
        gt = gate_ref[0, rows, :].astype(F32)
        y_ref[0, rows, :] = (gt * _sigmoid(gt) * on).astype(y_ref.dtype)


def _retention(proj, tokens_per_step=1024):
    b, s, _ = proj.shape
    h, c = RET_HEADS, RET_CHUNK
    t = min(tokens_per_step, s)
    log_g = jnp.log1p(-jnp.exp2(-5.0 - jnp.arange(h, dtype=F32)))
    idx = jnp.arange(c, dtype=F32)
    diff = idx[:, None] - idx[None, :]
    decay_intra = jnp.where(diff >= 0, jnp.exp(diff[None] * log_g[:, None, None]), 0.0)
    q_decay = jnp.exp((idx[None, :] + 1.0) * log_g[:, None])[:, :, None]
    k_decay = jnp.exp((c - 1.0 - idx[None, :]) * log_g[:, None])[:, :, None]
    chunk_decay = jnp.exp(c * log_g)[:, None, None]
    nk = RET_HEADS * RET_DK // RET_DK
    nv = 2 * RET_HEADS * RET_DK // RET_DV
    return pl.pallas_call(
        functools.partial(_retention_kernel, chunk=c, n_sub=t // c),
        grid=(b, h, s // t),
        in_specs=[
            pl.BlockSpec((1, t, RET_DK), lambda bi, hi, ti: (bi, ti, hi)),
            pl.BlockSpec((1, t, RET_DK), lambda bi, hi, ti: (bi, ti, nk + hi)),
            pl.BlockSpec((1, t, RET_DV), lambda bi, hi, ti: (bi, ti, nv + hi)),
            pl.BlockSpec((1, t, RET_DV), lambda bi, hi, ti: (bi, ti, nv + h + hi)),
            pl.BlockSpec((1, c, c), lambda bi, hi, ti: (hi, 0, 0)),
            pl.BlockSpec((1, c, 1), lambda bi, hi, ti: (hi, 0, 0)),
            pl.BlockSpec((1, c, 1), lambda bi, hi, ti: (hi, 0, 0)),
            pl.BlockSpec((1, 1, 1), lambda bi, hi, ti: (hi, 0, 0)),
        ],
        out_specs=pl.BlockSpec((1, t, RET_DV), lambda bi, hi, ti: (bi, ti, hi)),
        out_shape=jax.ShapeDtypeStruct((b, s, h * RET_DV), BF16),
        scratch_shapes=[pltpu.VMEM((RET_DK, RET_DV), F32)],
        compiler_params=pltpu.CompilerParams(
            dimension_semantics=("arbitrary", "arbitrary", "arbitrary"),
            vmem_limit_bytes=VMEM_LIMIT),
    )(proj, proj, proj, proj, decay_intra, q_decay, k_decay, chunk_decay)


def _compress_kernel(x_ref, xn_ref, pos_ref, w1_ref, w2_ref, o_ref):
    half = CMP_LEN // 2
    width = NSA_GROUPS * NSA_DH
    for g in range(NSA_GROUPS):
        acc = jnp.zeros((x_ref.shape[1], NSA_DH), F32)
        for l in range(half):
            cols = slice(l * width + g * NSA_DH, l * width + (g + 1) * NSA_DH)
            xa = (x_ref[0, :, cols].astype(F32) + pos_ref[l:l + 1, :]).astype(BF16)
            acc += jnp.dot(xa, w1_ref[l * NSA_DH:(l + 1) * NSA_DH, :], preferred_element_type=F32)
            lb = half + l
            xb = (xn_ref[0, :, cols].astype(F32) + pos_ref[lb:lb + 1, :]).astype(BF16)
            acc += jnp.dot(xb, w1_ref[lb * NSA_DH:(lb + 1) * NSA_DH, :], preferred_element_type=F32)
        hid = (acc * _sigmoid(acc)).astype(BF16)
        out = jnp.dot(hid, w2_ref[...], preferred_element_type=F32)
        o_ref[0, :, g * NSA_DH:(g + 1) * NSA_DH] = out.astype(o_ref.dtype)


def _compress(raw, pos, w1, w2):
    b, s, width = raw.shape
    rows = s // CMP_STRIDE
    x = raw.reshape(b, rows, CMP_STRIDE * width)
    xn = jnp.concatenate([x[:, 1:], jnp.zeros_like(x[:, :1])], axis=1)
    tr = min(256, rows)
    blk = pl.BlockSpec((1, tr, CMP_STRIDE * width), lambda bi, ri: (bi, ri, 0))
    return pl.pallas_call(
        _compress_kernel,
        grid=(b, rows // tr),
        in_specs=[
            blk, blk,
            pl.BlockSpec((CMP_LEN, NSA_DH), lambda bi, ri: (0, 0)),
            pl.BlockSpec((CMP_LEN * NSA_DH, NSA_DH), lambda bi, ri: (0, 0)),
            pl.BlockSpec((NSA_DH, NSA_DH), lambda bi, ri: (0, 0)),
        ],
        out_specs=pl.BlockSpec((1, tr, width), lambda bi, ri: (bi, ri, 0)),
        out_shape=jax.ShapeDtypeStruct((b, rows, width), BF16),
        compiler_params=pltpu.CompilerParams(
            dimension_semantics=("arbitrary", "arbitrary"), vmem_limit_bytes=VMEM_LIMIT),
    )(x, xn, pos, w1.astype(BF16), w2.astype(BF16))


def _stack_heads(qblk):
    return jnp.concatenate(
        [qblk[:, r * NSA_DH:(r + 1) * NSA_DH] for r in range(NSA_REP)], axis=0)


def _head_cols(r):
    return slice(r * Q_BLOCK, (r + 1) * Q_BLOCK)


def _alibi_table(slopes, n_keys, key_stride):
    j = jnp.arange(n_keys, dtype=F32)[:, None] * key_stride
    i = jnp.arange(Q_BLOCK, dtype=F32)[None, :]
    table = slopes.reshape(NSA_GROUPS, 1, NSA_REP, 1) * (j - i)[None, :, None, :]
    return table.reshape(NSA_GROUPS, n_keys, NSA_REP * Q_BLOCK)


def _chunk_shift(slopes, delta):
    row = jnp.full((1, Q_BLOCK), delta, jnp.int32).astype(F32)
    return jnp.concatenate([s * row for s in slopes], axis=1)


def _cmp_select_kernel(slopes_ref, q_ref, kc_ref, vct_ref, ovt_ref, spread_ref, bias_ref, ocmp_ref,
                       sel_ref, cnt_ref, s_scr, *, ck, nq):
    g = pl.program_id(1)
    qi = pl.program_id(2)
    q0 = qi * Q_BLOCK
    qb = Q_BLOCK
    n_ck, n_slc = ovt_ref.shape[0], ovt_ref.shape[1]

    q = _stack_heads(q_ref[0])
    slopes = [slopes_ref[g * NSA_REP + r] for r in range(NSA_REP)]
    rel = _iota((ck, qb), 1) - CMP_STRIDE * _iota((ck, qb), 0)
    chunk_delta = lambda off: CMP_STRIDE * off + (CMP_LEN - 1) - q0

    def attend_and_select(chunks, rows):
        tops = []
        for c in range(chunks):
            off = c * ck
            st = lax.dot_general(kc_ref[0, off:off + ck, :], q, _NT, preferred_element_type=F32)
            valid = rel >= chunk_delta(off)
            top_c = []
            for r in range(NSA_REP):
                s = jnp.where(valid, st[:, _head_cols(r)] + bias_ref[0, :, _head_cols(r)], MASKED)
                s_scr[off:off + ck, _head_cols(r)] = s
                top_c.append(jnp.max(s, axis=0, keepdims=True))
            tops.append(jnp.concatenate(top_c, axis=1) + _chunk_shift(slopes, chunk_delta(off)))
        m_all = functools.reduce(jnp.maximum, tops, jnp.full((1, NSA_REP * qb), M_FLOOR, F32))

        l = jnp.zeros((1, NSA_REP * qb), F32)
        o = jnp.zeros((NSA_DH, NSA_REP * qb), F32)
        imp = jnp.zeros((rows, NSA_REP * qb), F32)
        for c in range(chunks):
            off = c * ck
            shifted_m = m_all - _chunk_shift(slopes, chunk_delta(off))
            p = jnp.exp2(s_scr[off:off + ck, :] - shifted_m)
            pb = p.astype(BF16)
            l = l + jnp.sum(p, axis=0, keepdims=True)
            o = o + jnp.dot(vct_ref[0, 0, c], pb, preferred_element_type=F32)
            imp = imp + jnp.dot(ovt_ref[c, :rows, :], pb, preferred_element_type=F32)
        inv = jnp.where(l > 0.0, 1.0 / l, 0.0)
        ocmp_ref[0, 0, 0] = o * inv
        imp_w = imp * inv
        imp_t = imp_w[:, _head_cols(0)]
        for r in range(1, NSA_REP):
            imp_t = imp_t + imp_w[:, _head_cols(r)]

        jrow = _iota((rows, qb), 0)
        cur = lax.shift_right_logical(q0 + _iota((rows, qb), 1), int(np.log2(SLC_BLOCK)))
        forced = (jrow == 0) | (jrow == cur) | (jrow == cur - 1)
        score = jnp.where(forced, TAKEN, jnp.where(jrow <= cur, imp_t, -1.0))
        for _ in range(min(TOP_N, rows) - N_FORCED):
            best = jnp.max(score, axis=0, keepdims=True)
            first = jnp.min(jnp.where(score == best, jrow, rows), axis=0, keepdims=True)
            score = jnp.where(jrow == first, TAKEN, score)
        chosen = jnp.where(score == TAKEN, 1.0, 0.0).astype(BF16)
        sel_ref[0, 0, 0] = jnp.dot(spread_ref[:, :rows], chosen, preferred_element_type=F32)
        cnt_ref[0, 0, 0, :, :rows] = lax.dot_general(jnp.ones((8, qb), BF16), chosen, _NT,
                                                     preferred_element_type=F32)
        if rows < n_slc:
            cnt_ref[0, 0, 0, :, rows:] = jnp.zeros((8, n_slc - rows), F32)

    if nq % n_ck == 0 and n_slc % n_ck == 0:
        for k in range(1, n_ck + 1):
            in_share = (qi >= (k - 1) * (nq // n_ck)) & (qi < k * (nq // n_ck))
            pl.when(in_share)(functools.partial(attend_and_select, k, k * (n_slc // n_ck)))
    else:
        attend_and_select(n_ck, n_slc)


def _cmp_select(pa, k_cmp, v_cmp, slopes):
    b, s, _ = pa.shape
    nq = s // Q_BLOCK
    n_cmp = k_cmp.shape[1]
    n_slc = s // SLC_BLOCK
    ck = min(256, n_cmp)
    n_ck = n_cmp // ck
    gw = NSA_REP * NSA_DH
    cstart = np.arange(n_cmp)[None, :] * CMP_STRIDE
    sstart = np.arange(n_slc)[:, None] * SLC_BLOCK
    real = np.arange(n_cmp)[None, :] < (s - CMP_LEN) // CMP_STRIDE + 1
    ovt = ((cstart < sstart + SLC_BLOCK) & (cstart + CMP_LEN > sstart) & real).astype(np.float32)
    ovt = jnp.asarray(ovt.reshape(n_slc, n_ck, ck).transpose(1, 0, 2), BF16)
    per_chunk = SEL_CHUNK // SLC_BLOCK
    n_sel_chunks = s // SEL_CHUNK
    spread = np.zeros((8 * n_sel_chunks, n_slc), np.float32)
    for j in range(n_slc):
        spread[8 * (j // per_chunk) + j % per_chunk, j] = 1.0
    spread = jnp.asarray(spread, BF16)
    vct = v_cmp.reshape(b, n_ck, ck, NSA_GROUPS, NSA_DH).transpose(0, 3, 1, 4, 2)
    bias = _alibi_table(slopes, ck, CMP_STRIDE)
    return pl.pallas_call(
        functools.partial(_cmp_select_kernel, ck=ck, nq=nq),
        grid=(b, NSA_GROUPS, nq),
        in_specs=[
            pl.BlockSpec(memory_space=pltpu.SMEM),
            pl.BlockSpec((1, Q_BLOCK, gw), lambda bi, gi, qi: (bi, qi, gi)),
            pl.BlockSpec((1, n_cmp, NSA_DH), lambda bi, gi, qi: (bi, 0, gi)),
            pl.BlockSpec((1, 1, n_ck, NSA_DH, ck), lambda bi, gi, qi: (bi, gi, 0, 0, 0)),
            pl.BlockSpec((n_ck, n_slc, ck), lambda bi, gi, qi: (0, 0, 0)),
            pl.BlockSpec((8 * n_sel_chunks, n_slc), lambda bi, gi, qi: (0, 0)),
            pl.BlockSpec((1, ck, gw), lambda bi, gi, qi: (gi, 0, 0)),
        ],
        out_specs=[
            pl.BlockSpec((1, 1, 1, NSA_DH, gw), lambda bi, gi, qi: (bi, gi, qi, 0, 0)),
            pl.BlockSpec((1, 1, 1, 8 * n_sel_chunks, Q_BLOCK), lambda bi, gi, qi: (bi, gi, qi, 0, 0)),
            pl.BlockSpec((1, 1, 1, 8, n_slc), lambda bi, gi, qi: (bi, gi, qi, 0, 0)),
        ],
        out_shape=[
            jax.ShapeDtypeStruct((b, NSA_GROUPS, nq, NSA_DH, gw), F32),
            jax.ShapeDtypeStruct((b, NSA_GROUPS, nq, 8 * n_sel_chunks, Q_BLOCK), F32),
            jax.ShapeDtypeStruct((b, NSA_GROUPS, nq, 8, n_slc), F32),
        ],
        scratch_shapes=[pltpu.VMEM((n_cmp, gw), F32)],
        compiler_params=pltpu.CompilerParams(
            dimension_semantics=("arbitrary", "arbitrary", "arbitrary"),
            vmem_limit_bytes=VMEM_LIMIT),
    )(slopes, pa, k_cmp, vct, ovt, spread, bias)


def _sel_win_kernel(ids_ref, cnt_ref, slopes_ref, q_ref, ks_ref, vs_ref, kw_ref, vw_ref, sel_ref,
                    ocmp_ref, gate_ref, z_ref, bias_ref, y_ref, m_ref, l_ref, acc_ref, s_scr, *,
                    n_chunks):
    b = pl.program_id(0)
    g = pl.program_id(1)
    qi = pl.program_id(2)
    nq = pl.num_programs(2)
    qb = Q_BLOCK
    q0 = qi * qb
    per_piece = SEL_PIECE // SLC_BLOCK
    pieces_per_chunk = SEL_CHUNK // SEL_PIECE

    q = _stack_heads(q_ref[0])
    slopes = [slopes_ref[g * NSA_REP + r] for r in range(NSA_REP)]

    m_ref[...] = jnp.full_like(m_ref, M_FLOOR)
    l_ref[...] = jnp.zeros_like(l_ref)
    acc_ref[...] = jnp.zeros_like(acc_ref)
    step = (b * NSA_GROUPS + g) * nq + qi
    n_active = cnt_ref[step]
    id_base = step * n_chunks

    def stage_scores(row0, st, mask):
        n = st.shape[0]
        tops = []
        for r in range(NSA_REP):
            s = jnp.where(mask, st[:, _head_cols(r)] + bias_ref[0, :n, _head_cols(r)], MASKED)
            s_scr[row0:row0 + n, _head_cols(r)] = s
            tops.append(jnp.max(s, axis=0, keepdims=True))
        return jnp.concatenate(tops, axis=1)

    def piece_rows(ref, c):
        return ref[0, pl.ds(pl.multiple_of(c * SEL_PIECE, SEL_PIECE), SEL_PIECE), :]

    def piece_mask(c, threshold):
        group = lax.shift_right_logical(c, int(np.log2(pieces_per_chunk)))
        rows8 = sel_ref[0, 0, 0, pl.ds(pl.multiple_of(group * 8, 8), 8), :]
        within = c & (pieces_per_chunk - 1)
        flags = []
        for i in range(per_piece):
            row = rows8[i:i + 1]
            for w in range(1, pieces_per_chunk):
                row = jnp.where(within == w, rows8[w * per_piece + i:w * per_piece + i + 1], row)
            flags.append(jnp.broadcast_to(row, (SLC_BLOCK, qb)))
        chosen = jnp.concatenate(flags, axis=0)
        rel = _iota((SEL_PIECE, qb), 1) - _iota((SEL_PIECE, qb), 0)
        return (chosen > threshold) & (rel >= c * SEL_PIECE - q0)

    def stage_pair(slot, ca, cb, thr_a, thr_b):
        keys = jnp.concatenate([piece_rows(ks_ref, ca), piece_rows(ks_ref, cb)], axis=0)
        st = lax.dot_general(keys, q, _NT, preferred_element_type=F32)
        out = []
        for h, (c, thr) in enumerate(((ca, thr_a), (cb, thr_b))):
            rows = slice(h * SEL_PIECE, (h + 1) * SEL_PIECE)
            top = stage_scores((2 * slot + h) * SEL_PIECE, st[rows], piece_mask(c, thr))
            shift = _chunk_shift(slopes, c * SEL_PIECE - q0)
            out.append((top + shift, shift))
        return out

    def update(pieces, thresholds):
        pairs = [(pieces[2 * k], pieces[2 * k + 1]) for k in range(len(pieces) // 2)]
        staged = [stage_pair(k, ca, cb, thresholds[2 * k], thresholds[2 * k + 1])
                  for k, (ca, cb) in enumerate(pairs)]
        m_old = m_ref[...]
        m_new = functools.reduce(jnp.maximum, [top for pair in staged for top, _ in pair], m_old)
        alpha = jnp.exp2(m_old - m_new)
        l = alpha * l_ref[...]
        acc = alpha * acc_ref[...]
        for k, ((ca, cb), pair) in enumerate(zip(pairs, staged)):
            halves = []
            for h, (_, shift) in enumerate(pair):
                row0 = (2 * k + h) * SEL_PIECE
                p = jnp.exp2(s_scr[row0:row0 + SEL_PIECE, :] - (m_new - shift))
                l = l + jnp.sum(p, axis=0, keepdims=True)
                halves.append(p.astype(BF16))
            values = jnp.concatenate([piece_rows(vs_ref, ca), piece_rows(vs_ref, cb)], axis=0)
            acc = acc + lax.dot_general(values, jnp.concatenate(halves, axis=0), _TN,
                                        preferred_element_type=F32)
        m_ref[...] = m_new
        l_ref[...] = l
        acc_ref[...] = acc

    def piece_ids(first, n):
        return [ids_ref[id_base + first + k] for k in range(n)]

    def step(first, n):
        update(piece_ids(first, n), [jnp.where(first + k < n_active, 0.5, 2.0) for k in range(n)])

    def big_step(i, carry):
        step(BIG_STEP * i, BIG_STEP)
        return carry

    n_big = lax.shift_right_logical(n_active + BIG_STEP - SMALL_STEP - 1, int(np.log2(BIG_STEP)))
    lax.fori_loop(0, n_big, big_step, 0)
    pl.when(n_active > BIG_STEP * n_big)(lambda: step(BIG_STEP * n_big, SMALL_STEP))

    wk = WINDOW + qb
    start = pl.multiple_of(jnp.maximum(q0 - WINDOW, 0), qb)
    st = lax.dot_general(kw_ref[0, pl.ds(start, wk), :], q, _NT, preferred_element_type=F32)
    rel = _iota((wk, qb), 1) - _iota((wk, qb), 0)
    m_win = stage_scores(0, st, (rel >= start - q0) & (rel < WINDOW + start - q0))
    p = jnp.exp2(s_scr[0:wk, :] - m_win)
    l_win = jnp.sum(p, axis=0, keepdims=True)
    o_win = lax.dot_general(vw_ref[0, pl.ds(start, wk), :], p.astype(BF16), _TN,
                            preferred_element_type=F32)

    gates = _sigmoid(gate_ref[0, 0])
    gate_row = lambda br: jnp.concatenate(
        [gates[3 * r + br:3 * r + br + 1] for r in range(NSA_REP)], axis=1)
    l_sel = l_ref[...]
    comb = (gate_row(0) * ocmp_ref[0, 0, 0]
            + gate_row(1) * (acc_ref[...] * jnp.where(l_sel > 0.0, 1.0 / l_sel, 0.0))
            + gate_row(2) * (o_win * (1.0 / l_win)))
    for r in range(NSA_REP):
        cols = slice(r * NSA_DH, (r + 1) * NSA_DH)
        z = z_ref[0, :, cols].astype(F32)
        y_ref[0, :, cols] = (z * _sigmoid(z) * comb[:, _head_cols(r)].T).astype(y_ref.dtype)


def _sel_win(ids, cnt, slopes, pa, sel, ocmp, gates_t):
    b, s, _ = pa.shape
    nq = s // Q_BLOCK
    gw = NSA_REP * NSA_DH
    q_cols = NSA_HEADS * NSA_DH // NSA_DH
    z_col = (NSA_HEADS * NSA_DH + 6 * NSA_GROUPS * NSA_DH) // gw
    kv = lambda slot: pl.BlockSpec(
        (1, s, NSA_DH), lambda bi, gi, qi, *_: (bi, 0, q_cols + slot * NSA_GROUPS + gi))
    per_q =lambda arr: pl.BlockSpec((1, 1, 1) + arr.shape[3:],
                                     lambda bi, gi, qi, *_: (bi, gi, qi, 0, 0))
    grid_spec = pltpu.PrefetchScalarGridSpec(
        num_scalar_prefetch=3,
        grid=(b, NSA_GROUPS, nq),
        in_specs=[
            pl.BlockSpec((1, Q_BLOCK, gw), lambda bi, gi, qi, *_: (bi, qi, gi)),
            kv(2), kv(3), kv(4), kv(5),
            per_q(sel), per_q(ocmp),
            pl.BlockSpec((1, 1, NSA_REP * 3, Q_BLOCK), lambda bi, gi, qi, *_: (bi, gi, 0, qi)),
            pl.BlockSpec((1, Q_BLOCK, gw), lambda bi, gi, qi, *_: (bi, qi, z_col + gi)),
            pl.BlockSpec((1, WINDOW + Q_BLOCK, gw), lambda bi, gi, qi, *_: (gi, 0, 0)),
        ],
        out_specs=pl.BlockSpec((1, Q_BLOCK, gw), lambda bi, gi, qi, *_: (bi, qi, gi)),
        scratch_shapes=[
            pltpu.VMEM((1, gw), F32),
            pltpu.VMEM((1, gw), F32),
            pltpu.VMEM((NSA_DH, gw), F32),
            pltpu.VMEM((max(BIG_STEP * SEL_PIECE, WINDOW + Q_BLOCK), gw), F32),
        ],
    )
    return pl.pallas_call(
        functools.partial(_sel_win_kernel, n_chunks=s // SEL_PIECE),
        grid_spec=grid_spec,
        out_shape=jax.ShapeDtypeStruct((b, s, NSA_GROUPS * gw), BF16),
        compiler_params=pltpu.CompilerParams(
            dimension_semantics=("arbitrary", "arbitrary", "arbitrary"),
            vmem_limit_bytes=VMEM_LIMIT),
    )(ids, cnt, slopes, pa, pa, pa, pa, pa, sel, ocmp, gates_t, pa,
      _alibi_table(slopes, WINDOW + Q_BLOCK, 1))


def _active_chunk_lists(counts):
    b, g, nq, _, n_slc = counts.shape
    per_chunk = SEL_PIECE // SLC_BLOCK
    n_chunks = n_slc // per_chunk
    chunk = jnp.arange(n_chunks, dtype=jnp.int32)
    causal = chunk[None, :] <= (jnp.arange(nq, dtype=jnp.int32)[:, None] * Q_BLOCK) // SEL_PIECE
    active = counts[:, :, :, 0].reshape(b, g, nq, n_chunks, per_chunk).max(axis=-1) > 0.5
    active = active & causal
    ids = jnp.argsort(jnp.where(active, chunk, chunk + n_chunks), axis=-1).astype(jnp.int32)
    return ids.reshape(-1), active.sum(axis=-1, dtype=jnp.int32).reshape(-1)


def kernel(x, norm_g, ret_w_in, ret_w_out, nsa_w_in, nsa_cmp_pos_k, nsa_cmp_w1_k, nsa_cmp_w2_k,
           nsa_cmp_pos_v, nsa_cmp_w1_v, nsa_cmp_w2_v, nsa_w_out, final_g):
    b, s, d = x.shape
    m = b * s
    x2 = x.reshape(m, d)

    qk_w = 2 * RET_HEADS * RET_DK
    v_w = RET_HEADS * RET_DV
    w_in = ret_w_in[0].astype(BF16)
    scale = jnp.concatenate([jnp.ones((qk_w // 2,), F32), jnp.full((qk_w // 2,), RET_DK ** -0.5, F32),
                             jnp.ones((2 * v_w,), F32)])
    proj = _norm_proj(x2, norm_g[0], w_in, scale, BF16, tm=1024, tn=2048)
    y = _retention(proj.reshape(b, s, qk_w + 2 * v_w))
    h1, hn1 = _out_proj(y.reshape(m, v_w), ret_w_out[0].astype(BF16), x2, norm_g[1],
                        last_layer=False)

    width = NSA_HEADS * NSA_DH
    kvw = NSA_GROUPS * NSA_DH
    n_gate = NSA_HEADS * 3
    qkv_w = width + 6 * kvw
    w_n = nsa_w_in[0]
    w_a = jnp.concatenate([w_n[:, :qkv_w], w_n[:, qkv_w + n_gate:]], axis=1).astype(BF16)
    scale_a = jnp.concatenate([jnp.full((width,), NSA_DH ** -0.5 * LOG2E, F32),
                               jnp.ones((6 * kvw + width,), F32)])
    w_g = jnp.concatenate([w_n[:, qkv_w:qkv_w + n_gate], jnp.zeros((d, 128 - n_gate), F32)],
                          axis=1).astype(BF16)
    pa = _proj(hn1, w_a, scale_a, BF16, tm=min(2048, m), tn=1024)
    pa = pa.reshape(b, s, qkv_w + width)
    pg = _proj(hn1, w_g, jnp.ones((128,), F32), F32, tm=min(2048, m), tn=128)

    kv_cols = lambda slot: slice(width + slot * kvw, width + (slot + 1) * kvw)
    k_cmp = _compress(pa[:, :, kv_cols(0)], nsa_cmp_pos_k[0], nsa_cmp_w1_k[0], nsa_cmp_w2_k[0])
    v_cmp = _compress(pa[:, :, kv_cols(1)], nsa_cmp_pos_v[0], nsa_cmp_w1_v[0], nsa_cmp_w2_v[0])
    slopes = jnp.exp2(-8.0 * jnp.arange(1, NSA_HEADS + 1, dtype=F32) / NSA_HEADS) * LOG2E
    ocmp, sel, counts = _cmp_select(pa, k_cmp, v_cmp, slopes)
    ids, cnt = _active_chunk_lists(counts)
    gates_t = pg[:, :n_gate].reshape(b, s, NSA_GROUPS, NSA_REP * 3)
    gates_t = gates_t.transpose(0, 2, 3, 1)
    y = _sel_win(ids, cnt, slopes, pa, sel, ocmp, gates_t)
    out, = _out_proj(y.reshape(m, width), nsa_w_out[0].astype(BF16), h1, final_g, last_layer=True)
    return out.reshape(b, s, d)
```

```python
import functools

import jax
import jax.numpy as jnp
import numpy as np
from jax import lax
from jax.experimental import pallas as pl
from jax.experimental.pallas import tpu as pltpu

F32 = jnp.float32
BF16 = jnp.bfloat16

EPS = 1e-6
LOG2E = 1.4426950408889634
MASKED = -2e30
M_FLOOR = -1e30
TAKEN = -3e38
N_FORCED = 3

RET_HEADS = 8
RET_DK = 256
RET_DV = 512
RET_CHUNK = 256

NSA_HEADS = 16
NSA_GROUPS = 4
NSA_REP = 4
NSA_DH = 128
CMP_LEN = 32
CMP_STRIDE = 16
SLC_BLOCK = 64
TOP_N = 16
WINDOW = 512
Q_BLOCK = 128
SEL_CHUNK = 256
SEL_PIECE = 128
BIG_STEP = 10
SMALL_STEP = 2

VMEM_LIMIT = 56 * 1024 * 1024

_NT = (((1,), (1,)), ((), ()))
_TN = (((0,), (0,)), ((), ()))


def _sigmoid(x):
    return 1.0 / (1.0 + jnp.exp2(-LOG2E * x))


def _iota(shape, dim):
    return lax.broadcasted_iota(jnp.int32, shape, dim)


def _norm_proj_kernel(x_ref, g_ref, w_ref, s_ref, o_ref, xn_ref):
    @pl.when(pl.program_id(1) == 0)
    def _():
        x = x_ref[...]
        ms = jnp.mean(x * x, axis=-1, keepdims=True)
        xn_ref[...] = (x * lax.rsqrt(ms + EPS) * g_ref[...]).astype(xn_ref.dtype)

    acc = jnp.dot(xn_ref[...], w_ref[...], preferred_element_type=F32)
    o_ref[...] = (acc * s_ref[...]).astype(o_ref.dtype)


def _norm_proj(x2, g, w, col_scale, out_dtype, tm, tn):
    m, k = x2.shape
    n = w.shape[1]
    return pl.pallas_call(
        _norm_proj_kernel,
        grid=(m // tm, n // tn),
        in_specs=[
            pl.BlockSpec((tm, k), lambda i, j: (i, 0)),
            pl.BlockSpec((1, k), lambda i, j: (0, 0)),
            pl.BlockSpec((k, tn), lambda i, j: (0, j)),
            pl.BlockSpec((1, tn), lambda i, j: (0, j)),
        ],
        out_specs=pl.BlockSpec((tm, tn), lambda i, j: (i, j)),
        out_shape=jax.ShapeDtypeStruct((m, n), out_dtype),
        scratch_shapes=[pltpu.VMEM((tm, k), BF16)],
        compiler_params=pltpu.CompilerParams(
            dimension_semantics=("arbitrary", "arbitrary"), vmem_limit_bytes=VMEM_LIMIT),
    )(x2, g.reshape(1, k), w, col_scale.reshape(1, n))


def _proj_kernel(x_ref, w_ref, s_ref, o_ref):
    acc = jnp.dot(x_ref[...], w_ref[...], preferred_element_type=F32)
    o_ref[...] = (acc * s_ref[...]).astype(o_ref.dtype)


def _proj(x2, w, col_scale, out_dtype, tm, tn):
    m, k = x2.shape
    n = w.shape[1]
    return pl.pallas_call(
        _proj_kernel,
        grid=(m // tm, n // tn),
        in_specs=[
            pl.BlockSpec((tm, k), lambda i, j: (i, 0)),
            pl.BlockSpec((k, tn), lambda i, j: (0, j)),
            pl.BlockSpec((1, tn), lambda i, j: (0, j)),
        ],
        out_specs=pl.BlockSpec((tm, tn), lambda i, j: (i, j)),
        out_shape=jax.ShapeDtypeStruct((m, n), out_dtype),
        compiler_params=pltpu.CompilerParams(
            dimension_semantics=("arbitrary", "arbitrary"), vmem_limit_bytes=VMEM_LIMIT),
    )(x2, w, col_scale.reshape(1, n))


def _out_proj_kernel(y_ref, w_ref, res_ref, g_ref, *o_refs, last_layer):
    h = res_ref[...] + jnp.dot(y_ref[...], w_ref[...], preferred_element_type=F32)
    ms = jnp.mean(h * h, axis=-1, keepdims=True)
    hn = h * lax.rsqrt(ms + EPS) * g_ref[...]
    if last_layer:
        o_refs[0][...] = hn
    else:
        o_refs[0][...] = h
        o_refs[1][...] = hn.astype(o_refs[1].dtype)


def _out_proj(y2, w, res2, g, last_layer, tm=512):
    m, kd = y2.shape
    n = w.shape[1]
    row_blk = pl.BlockSpec((tm, n), lambda i: (i, 0))
    out_shape = [jax.ShapeDtypeStruct((m, n), F32)]
    if not last_layer:
        out_shape.append(jax.ShapeDtypeStruct((m, n), BF16))
    return pl.pallas_call(
        functools.partial(_out_proj_kernel, last_layer=last_layer),
        grid=(m // tm,),
        in_specs=[
            pl.BlockSpec((tm, kd), lambda i: (i, 0)),
            pl.BlockSpec((kd, n), lambda i: (0, 0), pipeline_mode=pl.Buffered(1)),
            row_blk,
            pl.BlockSpec((1, n), lambda i: (0, 0)),
        ],
        out_specs=[row_blk] * len(out_shape),
        out_shape=out_shape,
        compiler_params=pltpu.CompilerParams(
            dimension_semantics=("arbitrary",), vmem_limit_bytes=VMEM_LIMIT),
    )(y2, w, res2, g.reshape(1, n))


def _retention_kernel(q_ref, k_ref, v_ref, gate_ref, di_ref, qd_ref, kd_ref, cd_ref, y_ref,
                      state_ref, *, chunk, n_sub):
    @pl.when(pl.program_id(2) == 0)
    def _():
        state_ref[...] = jnp.zeros_like(state_ref)

    di = di_ref[0]
    qd = qd_ref[0]
    kd = kd_ref[0]
    cd = cd_ref[0]
    for i in range(n_sub):
        rows = pl.ds(i * chunk, chunk)
        q = q_ref[0, rows, :]
        k = k_ref[0, rows, :]
        v = v_ref[0, rows, :]
        state = state_ref[...]
        s = lax.dot_general(q, k, _NT, preferred_element_type=F32) * di
        o = jnp.dot(s.astype(BF16), v, preferred_element_type=F32)
        o = o + jnp.dot(q, state.astype(BF16), preferred_element_type=F32) * qd
        k_dec = (k.astype(F32) * kd).astype(BF16)
        state_ref[...] = state * cd + lax.dot_general(k_dec, v, _TN, preferred_element_type=F32)
        mu = jnp.mean(o, axis=-1, keepdims=True)
        oc = o - mu
        var = jnp.mean(oc * oc, axis=-1, keepdims=True)
        on = oc * lax.rsqrt(var + EPS)
        gt = gate_ref[0, rows, :].astype(F32)
        y_ref[0, rows, :] = (gt * _sigmoid(gt) * on).astype(y_ref.dtype)


def _retention(proj, tokens_per_step=1024):
    b, s, _ = proj.shape
    h, c = RET_HEADS, RET_CHUNK
    t = min(tokens_per_step, s)
    log_g = jnp.log1p(-jnp.exp2(-5.0 - jnp.arange(h, dtype=F32)))
    idx = jnp.arange(c, dtype=F32)
    diff = idx[:, None] - idx[None, :]
    decay_intra = jnp.where(diff >= 0, jnp.exp(diff[None] * log_g[:, None, None]), 0.0)
    q_decay = jnp.exp((idx[None, :] + 1.0) * log_g[:, None])[:, :, None]
    k_decay = jnp.exp((c - 1.0 - idx[None, :]) * log_g[:, None])[:, :, None]
    chunk_decay = jnp.exp(c * log_g)[:, None, None]
    nk = RET_HEADS * RET_DK // RET_DK
    nv = 2 * RET_HEADS * RET_DK // RET_DV
    return pl.pallas_call(
        functools.partial(_retention_kernel, chunk=c, n_sub=t // c),
        grid=(b, h, s // t),
        in_specs=[
            pl.BlockSpec((1, t, RET_DK), lambda bi, hi, ti: (bi, ti, hi)),
            pl.BlockSpec((1, t, RET_DK), lambda bi, hi, ti: (bi, ti, nk + hi)),
            pl.BlockSpec((1, t, RET_DV), lambda bi, hi, ti: (bi, ti, nv + hi)),
            pl.BlockSpec((1, t, RET_DV), lambda bi, hi, ti: (bi, ti, nv + h + hi)),
            pl.BlockSpec((1, c, c), lambda bi, hi, ti: (hi, 0, 0)),
            pl.BlockSpec((1, c, 1), lambda bi, hi, ti: (hi, 0, 0)),
            pl.BlockSpec((1, c, 1), lambda bi, hi, ti: (hi, 0, 0)),
            pl.BlockSpec((1, 1, 1), lambda bi, hi, ti: (hi, 0, 0)),
        ],
        out_specs=pl.BlockSpec((1, t, RET_DV), lambda bi, hi, ti: (bi, ti, hi)),
        out_shape=jax.ShapeDtypeStruct((b, s, h * RET_DV), BF16),
        scratch_shapes=[pltpu.VMEM((RET_DK, RET_DV), F32)],
        compiler_params=pltpu.CompilerParams(
            dimension_semantics=("arbitrary", "arbitrary", "arbitrary"),
            vmem_limit_bytes=VMEM_LIMIT),
    )(proj, proj, proj, proj, decay_intra, q_decay, k_decay, chunk_decay)


def _compress_kernel(x_ref, xn_ref, pos_ref, w1_ref, w2_ref, o_ref):
    half = CMP_LEN // 2
    width = NSA_GROUPS * NSA_DH
    for g in range(NSA_GROUPS):
        acc = jnp.zeros((x_ref.shape[1], NSA_DH), F32)
        for l in range(half):
            cols = slice(l * width + g * NSA_DH, l * width + (g + 1) * NSA_DH)
            xa = (x_ref[0, :, cols].astype(F32) + pos_ref[l:l + 1, :]).astype(BF16)
            acc += jnp.dot(xa, w1_ref[l * NSA_DH:(l + 1) * NSA_DH, :], preferred_element_type=F32)
            lb = half + l
            xb = (xn_ref[0, :, cols].astype(F32) + pos_ref[lb:lb + 1, :]).astype(BF16)
            acc += jnp.dot(xb, w1_ref[lb * NSA_DH:(lb + 1) * NSA_DH, :], preferred_element_type=F32)
        hid = (acc * _sigmoid(acc)).astype(BF16)
        out = jnp.dot(hid, w2_ref[...], preferred_element_type=F32)
        o_ref[0, :, g * NSA_DH:(g + 1) * NSA_DH] = out.astype(o_ref.dtype)


def _compress(raw, pos, w1, w2):
    b, s, width = raw.shape
    rows = s // CMP_STRIDE
    x = raw.reshape(b, rows, CMP_STRIDE * width)
    xn = jnp.concatenate([x[:, 1:], jnp.zeros_like(x[:, :1])], axis=1)
    tr = min(256, rows)
    blk = pl.BlockSpec((1, tr, CMP_STRIDE * width), lambda bi, ri: (bi, ri, 0))
    return pl.pallas_call(
        _compress_kernel,
        grid=(b, rows // tr),
        in_specs=[
            blk, blk,
            pl.BlockSpec((CMP_LEN, NSA_DH), lambda bi, ri: (0, 0)),
            pl.BlockSpec((CMP_LEN * NSA_DH, NSA_DH), lambda bi, ri: (0, 0)),
            pl.BlockSpec((NSA_DH, NSA_DH), lambda bi, ri: (0, 0)),
        ],
        out_specs=pl.BlockSpec((1, tr, width), lambda bi, ri: (bi, ri, 0)),
        out_shape=jax.ShapeDtypeStruct((b, rows, width), BF16),
        compiler_params=pltpu.CompilerParams(
            dimension_semantics=("arbitrary", "arbitrary"), vmem_limit_bytes=VMEM_LIMIT),
    )(x, xn, pos, w1.astype(BF16), w2.astype(BF16))


def _stack_heads(qblk):
    return jnp.concatenate(
        [qblk[:, r * NSA_DH:(r + 1) * NSA_DH] for r in range(NSA_REP)], axis=0)


def _head_cols(r):
    return slice(r * Q_BLOCK, (r + 1) * Q_BLOCK)


def _alibi_table(slopes, n_keys, key_stride):
    j = jnp.arange(n_keys, dtype=F32)[:, None] * key_stride
    i = jnp.arange(Q_BLOCK, dtype=F32)[None, :]
    table = slopes.reshape(NSA_GROUPS, 1, NSA_REP, 1) * (j - i)[None, :, None, :]
    return table.reshape(NSA_GROUPS, n_keys, NSA_REP * Q_BLOCK)


def _chunk_shift(slopes, delta):
    row = jnp.full((1, Q_BLOCK), delta, jnp.int32).astype(F32)
    return jnp.concatenate([s * row for s in slopes], axis=1)


def _cmp_select_kernel(slopes_ref, q_ref, kc_ref, vct_ref, ovt_ref, spread_ref, bias_ref, ocmp_ref,
                       sel_ref, cnt_ref, s_scr, *, ck, nq):
    g = pl.program_id(1)
    qi = pl.program_id(2)
    q0 = qi * Q_BLOCK
    qb = Q_BLOCK
    n_ck, n_slc = ovt_ref.shape[0], ovt_ref.shape[1]

    q = _stack_heads(q_ref[0])
    slopes = [slopes_ref[g * NSA_REP + r] for r in range(NSA_REP)]
    rel = _iota((ck, qb), 1) - CMP_STRIDE * _iota((ck, qb), 0)
    chunk_delta = lambda off: CMP_STRIDE * off + (CMP_LEN - 1) - q0

    def attend_and_select(chunks, rows):
        tops = []
        for c in range(chunks):
            off = c * ck
            st = lax.dot_general(kc_ref[0, off:off + ck, :], q, _NT, preferred_element_type=F32)
            valid = rel >= chunk_delta(off)
            top_c = []
            for r in range(NSA_REP):
                s = jnp.where(valid, st[:, _head_cols(r)] + bias_ref[0, :, _head_cols(r)], MASKED)
                s_scr[off:off + ck, _head_cols(r)] = s
                top_c.append(jnp.max(s, axis=0, keepdims=True))
            tops.append(jnp.concatenate(top_c, axis=1) + _chunk_shift(slopes, chunk_delta(off)))
        m_all = functools.reduce(jnp.maximum, tops, jnp.full((1, NSA_REP * qb), M_FLOOR, F32))

        l = jnp.zeros((1, NSA_REP * qb), F32)
        o = jnp.zeros((NSA_DH, NSA_REP * qb), F32)
        imp = jnp.zeros((rows, NSA_REP * qb), F32)
        for c in range(chunks):
            off = c * ck
            shifted_m = m_all - _chunk_shift(slopes, chunk_delta(off))
            p = jnp.exp2(s_scr[off:off + ck, :] - shifted_m)
            pb = p.astype(BF16)
            l = l + jnp.sum(p, axis=0, keepdims=True)
            o = o + jnp.dot(vct_ref[0, 0, c], pb, preferred_element_type=F32)
            imp = imp + jnp.dot(ovt_ref[c, :rows, :], pb, preferred_element_type=F32)
        inv = jnp.where(l > 0.0, 1.0 / l, 0.0)
        ocmp_ref[0, 0, 0] = o * inv
        imp_w = imp * inv
        imp_t = imp_w[:, _head_cols(0)]
        for r in range(1, NSA_REP):
            imp_t = imp_t + imp_w[:, _head_cols(r)]

        jrow = _iota((rows, qb), 0)
        cur = lax.shift_right_logical(q0 + _iota((rows, qb), 1), int(np.log2(SLC_BLOCK)))
        forced = (jrow == 0) | (jrow == cur) | (jrow == cur - 1)
        score = jnp.where(forced, TAKEN, jnp.where(jrow <= cur, imp_t, -1.0))
        for _ in range(min(TOP_N, rows) - N_FORCED):
            best = jnp.max(score, axis=0, keepdims=True)
            first = jnp.min(jnp.where(score == best, jrow, rows), axis=0, keepdims=True)
            score = jnp.where(jrow == first, TAKEN, score)
        chosen = jnp.where(score == TAKEN, 1.0, 0.0).astype(BF16)
        sel_ref[0, 0, 0] = jnp.dot(spread_ref[:, :rows], chosen, preferred_element_type=F32)
        cnt_ref[0, 0, 0, :, :rows] = lax.dot_general(jnp.ones((8, qb), BF16), chosen, _NT,
                                                     preferred_element_type=F32)
        if rows < n_slc:
            cnt_ref[0, 0, 0, :, rows:] = jnp.zeros((8, n_slc - rows), F32)

    if nq % n_ck == 0 and n_slc % n_ck == 0:
        for k in range(1, n_ck + 1):
            in_share = (qi >= (k - 1) * (nq // n_ck)) & (qi < k * (nq // n_ck))
            pl.when(in_share)(functools.partial(attend_and_select, k, k * (n_slc // n_ck)))
    else:
        attend_and_select(n_ck, n_slc)


def _cmp_select(pa, k_cmp, v_cmp, slopes):
    b, s, _ = pa.shape
    nq = s // Q_BLOCK
    n_cmp = k_cmp.shape[1]
    n_slc = s // SLC_BLOCK
    ck = min(256, n_cmp)
    n_ck = n_cmp // ck
    gw = NSA_REP * NSA_DH
    cstart = np.arange(n_cmp)[None, :] * CMP_STRIDE
    sstart = np.arange(n_slc)[:, None] * SLC_BLOCK
    real = np.arange(n_cmp)[None, :] < (s - CMP_LEN) // CMP_STRIDE + 1
    ovt = ((cstart < sstart + SLC_BLOCK) & (cstart + CMP_LEN > sstart) & real).astype(np.float32)
    ovt = jnp.asarray(ovt.reshape(n_slc, n_ck, ck).transpose(1, 0, 2), BF16)
    per_chunk = SEL_CHUNK // SLC_BLOCK
    n_sel_chunks = s // SEL_CHUNK
    spread = np.zeros((8 * n_sel_chunks, n_slc), np.float32)
    for j in range(n_slc):
        spread[8 * (j // per_chunk) + j % per_chunk, j] = 1.0
    spread = jnp.asarray(spread, BF16)
    vct = v_cmp.reshape(b, n_ck, ck, NSA_GROUPS, NSA_DH).transpose(0, 3, 1, 4, 2)
    bias = _alibi_table(slopes, ck, CMP_STRIDE)
    return pl.pallas_call(
        functools.partial(_cmp_select_kernel, ck=ck, nq=nq),
        grid=(b, NSA_GROUPS, nq),
        in_specs=[
            pl.BlockSpec(memory_space=pltpu.SMEM),
            pl.BlockSpec((1, Q_BLOCK, gw), lambda bi, gi, qi: (bi, qi, gi)),
            pl.BlockSpec((1, n_cmp, NSA_DH), lambda bi, gi, qi: (bi, 0, gi)),
            pl.BlockSpec((1, 1, n_ck, NSA_DH, ck), lambda bi, gi, qi: (bi, gi, 0, 0, 0)),
            pl.BlockSpec((n_ck, n_slc, ck), lambda bi, gi, qi: (0, 0, 0)),
            pl.BlockSpec((8 * n_sel_chunks, n_slc), lambda bi, gi, qi: (0, 0)),
            pl.BlockSpec((1, ck, gw), lambda bi, gi, qi: (gi, 0, 0)),
        ],
        out_specs=[
            pl.BlockSpec((1, 1, 1, NSA_DH, gw), lambda bi, gi, qi: (bi, gi, qi, 0, 0)),
            pl.BlockSpec((1, 1, 1, 8 * n_sel_chunks, Q_BLOCK), lambda bi, gi, qi: (bi, gi, qi, 0, 0)),
            pl.BlockSpec((1, 1, 1, 8, n_slc), lambda bi, gi, qi: (bi, gi, qi, 0, 0)),
        ],
        out_shape=[
            jax.ShapeDtypeStruct((b, NSA_GROUPS, nq, NSA_DH, gw), F32),
            jax.ShapeDtypeStruct((b, NSA_GROUPS, nq, 8 * n_sel_chunks, Q_BLOCK), F32),
            jax.ShapeDtypeStruct((b, NSA_GROUPS, nq, 8, n_slc), F32),
        ],
        scratch_shapes=[pltpu.VMEM((n_cmp, gw), F32)],
        compiler_params=pltpu.CompilerParams(
            dimension_semantics=("arbitrary", "arbitrary", "arbitrary"),
            vmem_limit_bytes=VMEM_LIMIT),
    )(slopes, pa, k_cmp, vct, ovt, spread, bias)


def _sel_win_kernel(ids_ref, cnt_ref, slopes_ref, q_ref, ks_ref, vs_ref, kw_ref, vw_ref, sel_ref,
                    ocmp_ref, gate_ref, z_ref, bias_ref, y_ref, m_ref, l_ref, acc_ref, s_scr, *,
                    n_chunks):
    b = pl.program_id(0)
    g = pl.program_id(1)
    qi = pl.program_id(2)
    nq = pl.num_programs(2)
    qb = Q_BLOCK
    q0 = qi * qb
    per_piece = SEL_PIECE // SLC_BLOCK
    pieces_per_chunk = SEL_CHUNK // SEL_PIECE

    q = _stack_heads(q_ref[0])
    slopes = [slopes_ref[g * NSA_REP + r] for r in range(NSA_REP)]

    m_ref[...] = jnp.full_like(m_ref, M_FLOOR)
    l_ref[...] = jnp.zeros_like(l_ref)
    acc_ref[...] = jnp.zeros_like(acc_ref)
    step = (b * NSA_GROUPS + g) * nq + qi
    n_active = cnt_ref[3 * step]
    n_big = cnt_ref[3 * step + 1]
    n_small = cnt_ref[3 * step + 2]
    id_base = step * n_chunks

    def stage_scores(row0, st, mask):
        n = st.shape[0]
        tops = []
        for r in range(NSA_REP):
            s = jnp.where(mask, st[:, _head_cols(r)] + bias_ref[0, :n, _head_cols(r)], MASKED)
            s_scr[row0:row0 + n, _head_cols(r)] = s
            tops.append(jnp.max(s, axis=0, keepdims=True))
        return jnp.concatenate(tops, axis=1)

    def piece_rows(ref, c):
        return ref[0, pl.ds(pl.multiple_of(c * SEL_PIECE, SEL_PIECE), SEL_PIECE), :]

    def piece_mask(c, threshold):
        group = lax.shift_right_logical(c, int(np.log2(pieces_per_chunk)))
        rows8 = sel_ref[0, 0, 0, pl.ds(pl.multiple_of(group * 8, 8), 8), :]
        within = c & (pieces_per_chunk - 1)
        flags = []
        for i in range(per_piece):
            row = rows8[i:i + 1]
            for w in range(1, pieces_per_chunk):
                row = jnp.where(within == w, rows8[w * per_piece + i:w * per_piece + i + 1], row)
            flags.append(jnp.broadcast_to(row, (SLC_BLOCK, qb)))
        chosen = jnp.concatenate(flags, axis=0)
        rel = _iota((SEL_PIECE, qb), 1) - _iota((SEL_PIECE, qb), 0)
        return (chosen > threshold) & (rel >= c * SEL_PIECE - q0)

    def stage_pair(slot, ca, cb, thr_a, thr_b):
        keys = jnp.concatenate([piece_rows(ks_ref, ca), piece_rows(ks_ref, cb)], axis=0)
        st = lax.dot_general(keys, q, _NT, preferred_element_type=F32)
        out = []
        for h, (c, thr) in enumerate(((ca, thr_a), (cb, thr_b))):
            rows = slice(h * SEL_PIECE, (h + 1) * SEL_PIECE)
            top = stage_scores((2 * slot + h) * SEL_PIECE, st[rows], piece_mask(c, thr))
            shift = _chunk_shift(slopes, c * SEL_PIECE - q0)
            out.append((top + shift, shift))
        return out

    def update(pieces, thresholds):
        pairs = [(pieces[2 * k], pieces[2 * k + 1]) for k in range(len(pieces) // 2)]
        staged = [stage_pair(k, ca, cb, thresholds[2 * k], thresholds[2 * k + 1])
                  for k, (ca, cb) in enumerate(pairs)]
        m_old = m_ref[...]
        m_new = functools.reduce(jnp.maximum, [top for pair in staged for top, _ in pair], m_old)
        alpha = jnp.exp2(m_old - m_new)
        l = alpha * l_ref[...]
        acc = alpha * acc_ref[...]
        for k, ((ca, cb), pair) in enumerate(zip(pairs, staged)):
            halves = []
            for h, (_, shift) in enumerate(pair):
                row0 = (2 * k + h) * SEL_PIECE
                p = jnp.exp2(s_scr[row0:row0 + SEL_PIECE, :] - (m_new - shift))
                l = l + jnp.sum(p, axis=0, keepdims=True)
                halves.append(p.astype(BF16))
            values = jnp.concatenate([piece_rows(vs_ref, ca), piece_rows(vs_ref, cb)], axis=0)
            acc = acc + lax.dot_general(values, jnp.concatenate(halves, axis=0), _TN,
                                        preferred_element_type=F32)
        m_ref[...] = m_new
        l_ref[...] = l
        acc_ref[...] = acc

    def run_step(first, n):
        pieces = [ids_ref[id_base + jnp.minimum(first + k, n_chunks - 1)] for k in range(n)]
        update(pieces, [jnp.where(first + k < n_active, 0.5, 2.0) for k in range(n)])

    def big_step(i, carry):
        run_step(BIG_STEP * i, BIG_STEP)
        return carry

    def small_step(i, carry):
        run_step(BIG_STEP * n_big + SMALL_STEP * i, SMALL_STEP)
        return carry

    lax.fori_loop(0, n_big, big_step, 0)
    lax.fori_loop(0, n_small, small_step, 0)

    wk = WINDOW + qb
    start = pl.multiple_of(jnp.maximum(q0 - WINDOW, 0), qb)
    st = lax.dot_general(kw_ref[0, pl.ds(start, wk), :], q, _NT, preferred_element_type=F32)
    rel = _iota((wk, qb), 1) - _iota((wk, qb), 0)
    m_win = stage_scores(0, st, (rel >= start - q0) & (rel < WINDOW + start - q0))
    p = jnp.exp2(s_scr[0:wk, :] - m_win)
    l_win = jnp.sum(p, axis=0, keepdims=True)
    o_win = lax.dot_general(vw_ref[0, pl.ds(start, wk), :], p.astype(BF16), _TN,
                            preferred_element_type=F32)

    gates = _sigmoid(gate_ref[0, 0])
    gate_row = lambda br: jnp.concatenate(
        [gates[3 * r + br:3 * r + br + 1] for r in range(NSA_REP)], axis=1)
    l_sel = l_ref[...]
    comb = (gate_row(0) * ocmp_ref[0, 0, 0]
            + gate_row(1) * (acc_ref[...] * jnp.where(l_sel > 0.0, 1.0 / l_sel, 0.0))
            + gate_row(2) * (o_win * (1.0 / l_win)))
    for r in range(NSA_REP):
        cols = slice(r * NSA_DH, (r + 1) * NSA_DH)
        z = z_ref[0, :, cols].astype(F32)
        y_ref[0, :, cols] = (z * _sigmoid(z) * comb[:, _head_cols(r)].T).astype(y_ref.dtype)


def _sel_win(ids, cnt, slopes, pa, sel, ocmp, gates_t):
    b, s, _ = pa.shape
    nq = s // Q_BLOCK
    gw = NSA_REP * NSA_DH
    q_cols = NSA_HEADS * NSA_DH // NSA_DH
    z_col = (NSA_HEADS * NSA_DH + 6 * NSA_GROUPS * NSA_DH) // gw
    kv = lambda slot: pl.BlockSpec(
        (1, s, NSA_DH), lambda bi, gi, qi, *_: (bi, 0, q_cols + slot * NSA_GROUPS + gi))
    per_q =lambda arr: pl.BlockSpec((1, 1, 1) + arr.shape[3:],
                                     lambda bi, gi, qi, *_: (bi, gi, qi, 0, 0))
    grid_spec = pltpu.PrefetchScalarGridSpec(
        num_scalar_prefetch=3,
        grid=(b, NSA_GROUPS, nq),
        in_specs=[
            pl.BlockSpec((1, Q_BLOCK, gw), lambda bi, gi, qi, *_: (bi, qi, gi)),
            kv(2), kv(3), kv(4), kv(5),
            per_q(sel), per_q(ocmp),
            pl.BlockSpec((1, 1, NSA_REP * 3, Q_BLOCK), lambda bi, gi, qi, *_: (bi, gi, 0, qi)),
            pl.BlockSpec((1, Q_BLOCK, gw), lambda bi, gi, qi, *_: (bi, qi, z_col + gi)),
            pl.BlockSpec((1, WINDOW + Q_BLOCK, gw), lambda bi, gi, qi, *_: (gi, 0, 0)),
        ],
        out_specs=pl.BlockSpec((1, Q_BLOCK, gw), lambda bi, gi, qi, *_: (bi, qi, gi)),
        scratch_shapes=[
            pltpu.VMEM((1, gw), F32),
            pltpu.VMEM((1, gw), F32),
            pltpu.VMEM((NSA_DH, gw), F32),
            pltpu.VMEM((max(BIG_STEP * SEL_PIECE, WINDOW + Q_BLOCK), gw), F32),
        ],
    )
    return pl.pallas_call(
        functools.partial(_sel_win_kernel, n_chunks=s // SEL_PIECE),
        grid_spec=grid_spec,
        out_shape=jax.ShapeDtypeStruct((b, s, NSA_GROUPS * gw), BF16),
        compiler_params=pltpu.CompilerParams(
            dimension_semantics=("arbitrary", "arbitrary", "arbitrary"),
            vmem_limit_bytes=VMEM_LIMIT),
    )(ids, cnt, slopes, pa, pa, pa, pa, pa, sel, ocmp, gates_t, pa,
      _alibi_table(slopes, WINDOW + Q_BLOCK, 1))


def _active_chunk_lists(counts):
    b, g, nq, _, n_slc = counts.shape
    per_chunk = SEL_PIECE // SLC_BLOCK
    n_chunks = n_slc // per_chunk
    chunk = jnp.arange(n_chunks, dtype=jnp.int32)
    causal = chunk[None, :] <= (jnp.arange(nq, dtype=jnp.int32)[:, None] * Q_BLOCK) // SEL_PIECE
    active = counts[:, :, :, 0].reshape(b, g, nq, n_chunks, per_chunk).max(axis=-1) > 0.5
    active = active & causal
    ids = jnp.argsort(jnp.where(active, chunk, chunk + n_chunks), axis=-1).astype(jnp.int32)
    n = active.sum(axis=-1, dtype=jnp.int32)
    full, rest = n // BIG_STEP, n % BIG_STEP
    one_more = rest > 2 * SMALL_STEP
    n_big = full + one_more
    n_small = jnp.where(one_more, 0, (rest + SMALL_STEP - 1) // SMALL_STEP)
    return ids.reshape(-1), jnp.stack([n, n_big, n_small], axis=-1).astype(jnp.int32).reshape(-1)


def kernel(x, norm_g, ret_w_in, ret_w_out, nsa_w_in, nsa_cmp_pos_k, nsa_cmp_w1_k, nsa_cmp_w2_k,
           nsa_cmp_pos_v, nsa_cmp_w1_v, nsa_cmp_w2_v, nsa_w_out, final_g):
    b, s, d = x.shape
    m = b * s
    x2 = x.reshape(m, d)

    qk_w = 2 * RET_HEADS * RET_DK
    v_w = RET_HEADS * RET_DV
    w_in = ret_w_in[0].astype(BF16)
    scale = jnp.concatenate([jnp.ones((qk_w // 2,), F32), jnp.full((qk_w // 2,), RET_DK ** -0.5, F32),
                             jnp.ones((2 * v_w,), F32)])
    proj = _norm_proj(x2, norm_g[0], w_in, scale, BF16, tm=1024, tn=2048)
    y = _retention(proj.reshape(b, s, qk_w + 2 * v_w))
    h1, hn1 = _out_proj(y.reshape(m, v_w), ret_w_out[0].astype(BF16), x2, norm_g[1],
                        last_layer=False)

    width = NSA_HEADS * NSA_DH
    kvw = NSA_GROUPS * NSA_DH
    n_gate = NSA_HEADS * 3
    qkv_w = width + 6 * kvw
    w_n = nsa_w_in[0]
    w_a = jnp.concatenate([w_n[:, :qkv_w], w_n[:, qkv_w + n_gate:]], axis=1).astype(BF16)
    scale_a = jnp.concatenate([jnp.full((width,), NSA_DH ** -0.5 * LOG2E, F32),
                               jnp.ones((6 * kvw + width,), F32)])
    w_g = jnp.concatenate([w_n[:, qkv_w:qkv_w + n_gate], jnp.zeros((d, 128 - n_gate), F32)],
                          axis=1).astype(BF16)
    pa = _proj(hn1, w_a, scale_a, BF16, tm=min(2048, m), tn=1024)
    pa = pa.reshape(b, s, qkv_w + width)
    pg = _proj(hn1, w_g, jnp.ones((128,), F32), F32, tm=min(2048, m), tn=128)

    kv_cols = lambda slot: slice(width + slot * kvw, width + (slot + 1) * kvw)
    k_cmp = _compress(pa[:, :, kv_cols(0)], nsa_cmp_pos_k[0], nsa_cmp_w1_k[0], nsa_cmp_w2_k[0])
    v_cmp = _compress(pa[:, :, kv_cols(1)], nsa_cmp_pos_v[0], nsa_cmp_w1_v[0], nsa_cmp_w2_v[0])
    slopes = jnp.exp2(-8.0 * jnp.arange(1, NSA_HEADS + 1, dtype=F32) / NSA_HEADS) * LOG2E
    ocmp, sel, counts = _cmp_select(pa, k_cmp, v_cmp, slopes)
    ids, cnt = _active_chunk_lists(counts)
    gates_t = pg[:, :n_gate].reshape(b, s, NSA_GROUPS, NSA_REP * 3)
    gates_t = gates_t.transpose(0, 2, 3, 1)
    y = _sel_win(ids, cnt, slopes, pa, sel, ocmp, gates_t)
    out, = _out_proj(y.reshape(m, width), nsa_w_out[0].astype(BF16), h1, final_g, last_layer=True)
    return out.reshape(b, s, d)
```

```python
import functools

import jax
import jax.numpy as jnp
import numpy as np
from jax import lax
from jax.experimental import pallas as pl
from jax.experimental.pallas import tpu as pltpu

F32 = jnp.float32
BF16 = jnp.bfloat16

EPS = 1e-6
LOG2E = 1.4426950408889634
MASKED = -2e30
M_FLOOR = -1e30
TAKEN = -3e38
N_FORCED = 3

RET_HEADS = 8
RET_DK = 256
RET_DV = 512
RET_CHUNK = 256

NSA_HEADS = 16
NSA_GROUPS = 4
NSA_REP = 4
NSA_DH = 128
CMP_LEN = 32
CMP_STRIDE = 16
SLC_BLOCK = 64
TOP_N = 16
WINDOW = 512
Q_BLOCK = 128
SEL_CHUNK = 256
SEL_PIECE = 128
BIG_STEP = 10
SMALL_STEP = 2

VMEM_LIMIT = 56 * 1024 * 1024

_NT = (((1,), (1,)), ((), ()))
_TN = (((0,), (0,)), ((), ()))


def _sigmoid(x):
    return 1.0 / (1.0 + jnp.exp2(-LOG2E * x))


def _iota(shape, dim):
    return lax.broadcasted_iota(jnp.int32, shape, dim)


def _norm_proj_kernel(x_ref, g_ref, w_ref, s_ref, o_ref, xn_ref):
    @pl.when(pl.program_id(1) == 0)
    def _():
        x = x_ref[...]
        ms = jnp.mean(x * x, axis=-1, keepdims=True)
        xn_ref[...] = (x * lax.rsqrt(ms + EPS) * g_ref[...]).astype(xn_ref.dtype)

    acc = jnp.dot(xn_ref[...], w_ref[...], preferred_element_type=F32)
    o_ref[...] = (acc * s_ref[...]).astype(o_ref.dtype)


def _norm_proj(x2, g, w, col_scale, out_dtype, tm, tn):
    m, k = x2.shape
    n = w.shape[1]
    return pl.pallas_call(
        _norm_proj_kernel,
        grid=(m // tm, n // tn),
        in_specs=[
            pl.BlockSpec((tm, k), lambda i, j: (i, 0)),
            pl.BlockSpec((1, k), lambda i, j: (0, 0)),
            pl.BlockSpec((k, tn), lambda i, j: (0, j)),
            pl.BlockSpec((1, tn), lambda i, j: (0, j)),
        ],
        out_specs=pl.BlockSpec((tm, tn), lambda i, j: (i, j)),
        out_shape=jax.ShapeDtypeStruct((m, n), out_dtype),
        scratch_shapes=[pltpu.VMEM((tm, k), BF16)],
        compiler_params=pltpu.CompilerParams(
            dimension_semantics=("arbitrary", "arbitrary"), vmem_limit_bytes=VMEM_LIMIT),
    )(x2, g.reshape(1, k), w, col_scale.reshape(1, n))


def _proj_kernel(x_ref, w_ref, s_ref, o_ref):
    acc = jnp.dot(x_ref[...], w_ref[...], preferred_element_type=F32)
    o_ref[...] = (acc * s_ref[...]).astype(o_ref.dtype)


def _proj(x2, w, col_scale, out_dtype, tm, tn):
    m, k = x2.shape
    n = w.shape[1]
    return pl.pallas_call(
        _proj_kernel,
        grid=(m // tm, n // tn),
        in_specs=[
            pl.BlockSpec((tm, k), lambda i, j: (i, 0)),
            pl.BlockSpec((k, tn), lambda i, j: (0, j)),
            pl.BlockSpec((1, tn), lambda i, j: (0, j)),
        ],
        out_specs=pl.BlockSpec((tm, tn), lambda i, j: (i, j)),
        out_shape=jax.ShapeDtypeStruct((m, n), out_dtype),
        compiler_params=pltpu.CompilerParams(
            dimension_semantics=("arbitrary", "arbitrary"), vmem_limit_bytes=VMEM_LIMIT),
    )(x2, w, col_scale.reshape(1, n))


def _out_proj_kernel(y_ref, w_ref, res_ref, g_ref, *o_refs, last_layer):
    h = res_ref[...] + jnp.dot(y_ref[...], w_ref[...], preferred_element_type=F32)
    ms = jnp.mean(h * h, axis=-1, keepdims=True)
    hn = h * lax.rsqrt(ms + EPS) * g_ref[...]
    if last_layer:
        o_refs[0][...] = hn
    else:
        o_refs[0][...] = h
        o_refs[1][...] = hn.astype(o_refs[1].dtype)


def _out_proj(y2, w, res2, g, last_layer, tm=512):
    m, kd = y2.shape
    n = w.shape[1]
    row_blk = pl.BlockSpec((tm, n), lambda i: (i, 0))
    out_shape = [jax.ShapeDtypeStruct((m, n), F32)]
    if not last_layer:
        out_shape.append(jax.ShapeDtypeStruct((m, n), BF16))
    return pl.pallas_call(
        functools.partial(_out_proj_kernel, last_layer=last_layer),
        grid=(m // tm,),
        in_specs=[
            pl.BlockSpec((tm, kd), lambda i: (i, 0)),
            pl.BlockSpec((kd, n), lambda i: (0, 0), pipeline_mode=pl.Buffered(1)),
            row_blk,
            pl.BlockSpec((1, n), lambda i: (0, 0)),
        ],
        out_specs=[row_blk] * len(out_shape),
        out_shape=out_shape,
        compiler_params=pltpu.CompilerParams(
            dimension_semantics=("arbitrary",), vmem_limit_bytes=VMEM_LIMIT),
    )(y2, w, res2, g.reshape(1, n))


def _retention_kernel(q_ref, k_ref, v_ref, gate_ref, di_ref, qd_ref, kd_ref, cd_ref, y_ref,
                      state_ref, *, chunk, n_sub):
    @pl.when(pl.program_id(2) == 0)
    def _():
        state_ref[...] = jnp.zeros_like(state_ref)

    di = di_ref[0]
    qd = qd_ref[0]
    kd = kd_ref[0]
    cd = cd_ref[0]
    for i in range(n_sub):
        rows = pl.ds(i * chunk, chunk)
        q = q_ref[0, rows, :]
        k = k_ref[0, rows, :]
        v = v_ref[0, rows, :]
        state = state_ref[...]
        s = lax.dot_general(q, k, _NT, preferred_element_type=F32) * di
        o = jnp.dot(s.astype(BF16), v, preferred_element_type=F32)
        o = o + jnp.dot(q, state.astype(BF16), preferred_element_type=F32) * qd
        k_dec = (k.astype(F32) * kd).astype(BF16)
        state_ref[...] = state * cd + lax.dot_general(k_dec, v, _TN, preferred_element_type=F32)
        mu = jnp.mean(o, axis=-1, keepdims=True)
        oc = o - mu
        var = jnp.mean(oc * oc, axis=-1, keepdims=True)
        on = oc * lax.rsqrt(var + EPS)
        gt = gate_ref[0, rows, :].astype(F32)
        y_ref[0, rows, :] = (gt * _sigmoid(gt) * on).astype(y_ref.dtype)


def _retention(proj, tokens_per_step=1024):
    b, s, _ = proj.shape
    h, c = RET_HEADS, RET_CHUNK
    t = min(tokens_per_step, s)
    log_g = jnp.log1p(-jnp.exp2(-5.0 - jnp.arange(h, dtype=F32)))
    idx = jnp.arange(c, dtype=F32)
    diff = idx[:, None] - idx[None, :]
    decay_intra = jnp.where(diff >= 0, jnp.exp(diff[None] * log_g[:, None, None]), 0.0)
    q_decay = jnp.exp((idx[None, :] + 1.0) * log_g[:, None])[:, :, None]
    k_decay = jnp.exp((c - 1.0 - idx[None, :]) * log_g[:, None])[:, :, None]
    chunk_decay = jnp.exp(c * log_g)[:, None, None]
    nk = RET_HEADS * RET_DK // RET_DK
    nv = 2 * RET_HEADS * RET_DK // RET_DV
    return pl.pallas_call(
        functools.partial(_retention_kernel, chunk=c, n_sub=t // c),
        grid=(b, h, s // t),
        in_specs=[
            pl.BlockSpec((1, t, RET_DK), lambda bi, hi, ti: (bi, ti, hi)),
            pl.BlockSpec((1, t, RET_DK), lambda bi, hi, ti: (bi, ti, nk + hi)),
            pl.BlockSpec((1, t, RET_DV), lambda bi, hi, ti: (bi, ti, nv + hi)),
            pl.BlockSpec((1, t, RET_DV), lambda bi, hi, ti: (bi, ti, nv + h + hi)),
            pl.BlockSpec((1, c, c), lambda bi, hi, ti: (hi, 0, 0)),
            pl.BlockSpec((1, c, 1), lambda bi, hi, ti: (hi, 0, 0)),
            pl.BlockSpec((1, c, 1), lambda bi, hi, ti: (hi, 0, 0)),
            pl.BlockSpec((1, 1, 1), lambda bi, hi, ti: (hi, 0, 0)),
        ],
        out_specs=pl.BlockSpec((1, t, RET_DV), lambda bi, hi, ti: (bi, ti, hi)),
        out_shape=jax.ShapeDtypeStruct((b, s, h * RET_DV), BF16),
        scratch_shapes=[pltpu.VMEM((RET_DK, RET_DV), F32)],
        compiler_params=pltpu.CompilerParams(
            dimension_semantics=("arbitrary", "arbitrary", "arbitrary"),
            vmem_limit_bytes=VMEM_LIMIT),
    )(proj, proj, proj, proj, decay_intra, q_decay, k_decay, chunk_decay)


def _compress_kernel(x_ref, xn_ref, pos_ref, w1_ref, w2_ref, o_ref):
    half = CMP_LEN // 2
    width = NSA_GROUPS * NSA_DH
    for g in range(NSA_GROUPS):
        acc = jnp.zeros((x_ref.shape[1], NSA_DH), F32)
        for l in range(half):
            cols = slice(l * width + g * NSA_DH, l * width + (g + 1) * NSA_DH)
            xa = (x_ref[0, :, cols].astype(F32) + pos_ref[l:l + 1, :]).astype(BF16)
            acc += jnp.dot(xa, w1_ref[l * NSA_DH:(l + 1) * NSA_DH, :], preferred_element_type=F32)
            lb = half + l
            xb = (xn_ref[0, :, cols].astype(F32) + pos_ref[lb:lb + 1, :]).astype(BF16)
            acc += jnp.dot(xb, w1_ref[lb * NSA_DH:(lb + 1) * NSA_DH, :], preferred_element_type=F32)
        hid = (acc * _sigmoid(acc)).astype(BF16)
        out = jnp.dot(hid, w2_ref[...], preferred_element_type=F32)
        o_ref[0, :, g * NSA_DH:(g + 1) * NSA_DH] = out.astype(o_ref.dtype)


def _compress(raw, pos, w1, w2):
    b, s, width = raw.shape
    rows = s // CMP_STRIDE
    x = raw.reshape(b, rows, CMP_STRIDE * width)
    xn = jnp.concatenate([x[:, 1:], jnp.zeros_like(x[:, :1])], axis=1)
    tr = min(256, rows)
    blk = pl.BlockSpec((1, tr, CMP_STRIDE * width), lambda bi, ri: (bi, ri, 0))
    return pl.pallas_call(
        _compress_kernel,
        grid=(b, rows // tr),
        in_specs=[
            blk, blk,
            pl.BlockSpec((CMP_LEN, NSA_DH), lambda bi, ri: (0, 0)),
            pl.BlockSpec((CMP_LEN * NSA_DH, NSA_DH), lambda bi, ri: (0, 0)),
            pl.BlockSpec((NSA_DH, NSA_DH), lambda bi, ri: (0, 0)),
        ],
        out_specs=pl.BlockSpec((1, tr, width), lambda bi, ri: (bi, ri, 0)),
        out_shape=jax.ShapeDtypeStruct((b, rows, width), BF16),
        compiler_params=pltpu.CompilerParams(
            dimension_semantics=("arbitrary", "arbitrary"), vmem_limit_bytes=VMEM_LIMIT),
    )(x, xn, pos, w1.astype(BF16), w2.astype(BF16))


def _stack_heads(qblk):
    return jnp.concatenate(
        [qblk[:, r * NSA_DH:(r + 1) * NSA_DH] for r in range(NSA_REP)], axis=0)


def _head_cols(r):
    return slice(r * Q_BLOCK, (r + 1) * Q_BLOCK)


def _alibi_table(slopes, n_keys, key_stride):
    j = jnp.arange(n_keys, dtype=F32)[:, None] * key_stride
    i = jnp.arange(Q_BLOCK, dtype=F32)[None, :]
    table = slopes.reshape(NSA_GROUPS, 1, NSA_REP, 1) * (j - i)[None, :, None, :]
    return table.reshape(NSA_GROUPS, n_keys, NSA_REP * Q_BLOCK)


def _chunk_shift(slopes, delta):
    row = jnp.full((1, Q_BLOCK), delta, jnp.int32).astype(F32)
    return jnp.concatenate([s * row for s in slopes], axis=1)


def _cmp_select_kernel(slopes_ref, q_ref, kc_ref, vct_ref, ovt_ref, spread_ref, bias_ref, ocmp_ref,
                       sel_ref, cnt_ref, s_scr, *, ck, nq):
    g = pl.program_id(1)
    qi = pl.program_id(2)
    q0 = qi * Q_BLOCK
    qb = Q_BLOCK
    n_ck, n_slc = ovt_ref.shape[0], ovt_ref.shape[1]

    q = _stack_heads(q_ref[0])
    slopes = [slopes_ref[g * NSA_REP + r] for r in range(NSA_REP)]
    rel = _iota((ck, qb), 1) - CMP_STRIDE * _iota((ck, qb), 0)
    chunk_delta = lambda off: CMP_STRIDE * off + (CMP_LEN - 1) - q0

    def attend_and_select(chunks, rows):
        tops = []
        for c in range(chunks):
            off = c * ck
            st = lax.dot_general(kc_ref[0, off:off + ck, :], q, _NT, preferred_element_type=F32)
            valid = rel >= chunk_delta(off)
            top_c = []
            for r in range(NSA_REP):
                s = jnp.where(valid, st[:, _head_cols(r)] + bias_ref[0, :, _head_cols(r)], MASKED)
                s_scr[off:off + ck, _head_cols(r)] = s
                top_c.append(jnp.max(s, axis=0, keepdims=True))
            tops.append(jnp.concatenate(top_c, axis=1) + _chunk_shift(slopes, chunk_delta(off)))
        m_all = functools.reduce(jnp.maximum, tops, jnp.full((1, NSA_REP * qb), M_FLOOR, F32))

        l = jnp.zeros((1, NSA_REP * qb), F32)
        o = jnp.zeros((NSA_DH, NSA_REP * qb), F32)
        imp = jnp.zeros((rows, NSA_REP * qb), F32)
        for c in range(chunks):
            off = c * ck
            shifted_m = m_all - _chunk_shift(slopes, chunk_delta(off))
            p = jnp.exp2(s_scr[off:off + ck, :] - shifted_m)
            pb = p.astype(BF16)
            l = l + jnp.sum(p, axis=0, keepdims=True)
            o = o + jnp.dot(vct_ref[0, 0, c], pb, preferred_element_type=F32)
            imp = imp + jnp.dot(ovt_ref[c, :rows, :], pb, preferred_element_type=F32)
        inv = jnp.where(l > 0.0, 1.0 / l, 0.0)
        ocmp_ref[0, 0, 0] = o * inv
        imp_w = imp * inv
        imp_t = imp_w[:, _head_cols(0)]
        for r in range(1, NSA_REP):
            imp_t = imp_t + imp_w[:, _head_cols(r)]

        jrow = _iota((rows, qb), 0)
        cur = lax.shift_right_logical(q0 + _iota((rows, qb), 1), int(np.log2(SLC_BLOCK)))
        forced = (jrow == 0) | (jrow == cur) | (jrow == cur - 1)
        score = jnp.where(forced, TAKEN, jnp.where(jrow <= cur, imp_t, -1.0))
        for _ in range(min(TOP_N, rows) - N_FORCED):
            best = jnp.max(score, axis=0, keepdims=True)
            first = jnp.min(jnp.where(score == best, jrow, rows), axis=0, keepdims=True)
            score = jnp.where(jrow == first, TAKEN, score)
        chosen = jnp.where(score == TAKEN, 1.0, 0.0).astype(BF16)
        sel_ref[0, 0, 0] = jnp.dot(spread_ref[:, :rows], chosen, preferred_element_type=F32)
        cnt_ref[0, 0, 0, :, :rows] = lax.dot_general(jnp.ones((8, qb), BF16), chosen, _NT,
                                                     preferred_element_type=F32)
        if rows < n_slc:
            cnt_ref[0, 0, 0, :, rows:] = jnp.zeros((8, n_slc - rows), F32)

    if nq % n_ck == 0 and n_slc % n_ck == 0:
        for k in range(1, n_ck + 1):
            in_share = (qi >= (k - 1) * (nq // n_ck)) & (qi < k * (nq // n_ck))
            pl.when(in_share)(functools.partial(attend_and_select, k, k * (n_slc // n_ck)))
    else:
        attend_and_select(n_ck, n_slc)


def _cmp_select(pa, k_cmp, v_cmp, slopes):
    b, s, _ = pa.shape
    nq = s // Q_BLOCK
    n_cmp = k_cmp.shape[1]
    n_slc = s // SLC_BLOCK
    ck = min(256, n_cmp)
    n_ck = n_cmp // ck
    gw = NSA_REP * NSA_DH
    cstart = np.arange(n_cmp)[None, :] * CMP_STRIDE
    sstart = np.arange(n_slc)[:, None] * SLC_BLOCK
    real = np.arange(n_cmp)[None, :] < (s - CMP_LEN) // CMP_STRIDE + 1
    ovt = ((cstart < sstart + SLC_BLOCK) & (cstart + CMP_LEN > sstart) & real).astype(np.float32)
    ovt = jnp.asarray(ovt.reshape(n_slc, n_ck, ck).transpose(1, 0, 2), BF16)
    per_chunk = SEL_CHUNK // SLC_BLOCK
    n_sel_chunks = s // SEL_CHUNK
    spread = np.zeros((8 * n_sel_chunks, n_slc), np.float32)
    for j in range(n_slc):
        spread[8 * (j // per_chunk) + j % per_chunk, j] = 1.0
    spread = jnp.asarray(spread, BF16)
    vct = v_cmp.reshape(b, n_ck, ck, NSA_GROUPS, NSA_DH).transpose(0, 3, 1, 4, 2)
    bias = _alibi_table(slopes, ck, CMP_STRIDE)
    return pl.pallas_call(
        functools.partial(_cmp_select_kernel, ck=ck, nq=nq),
        grid=(b, NSA_GROUPS, nq),
        in_specs=[
            pl.BlockSpec(memory_space=pltpu.SMEM),
            pl.BlockSpec((1, Q_BLOCK, gw), lambda bi, gi, qi: (bi, qi, 2 * gi)),
            pl.BlockSpec((1, n_cmp, NSA_DH), lambda bi, gi, qi: (bi, 0, gi)),
            pl.BlockSpec((1, 1, n_ck, NSA_DH, ck), lambda bi, gi, qi: (bi, gi, 0, 0, 0)),
            pl.BlockSpec((n_ck, n_slc, ck), lambda bi, gi, qi: (0, 0, 0)),
            pl.BlockSpec((8 * n_sel_chunks, n_slc), lambda bi, gi, qi: (0, 0)),
            pl.BlockSpec((1, ck, gw), lambda bi, gi, qi: (gi, 0, 0)),
        ],
        out_specs=[
            pl.BlockSpec((1, 1, 1, NSA_DH, gw), lambda bi, gi, qi: (bi, gi, qi, 0, 0)),
            pl.BlockSpec((1, 1, 1, 8 * n_sel_chunks, Q_BLOCK), lambda bi, gi, qi: (bi, gi, qi, 0, 0)),
            pl.BlockSpec((1, 1, 1, 8, n_slc), lambda bi, gi, qi: (bi, gi, qi, 0, 0)),
        ],
        out_shape=[
            jax.ShapeDtypeStruct((b, NSA_GROUPS, nq, NSA_DH, gw), F32),
            jax.ShapeDtypeStruct((b, NSA_GROUPS, nq, 8 * n_sel_chunks, Q_BLOCK), F32),
            jax.ShapeDtypeStruct((b, NSA_GROUPS, nq, 8, n_slc), F32),
        ],
        scratch_shapes=[pltpu.VMEM((n_cmp, gw), F32)],
        compiler_params=pltpu.CompilerParams(
            dimension_semantics=("arbitrary", "arbitrary", "arbitrary"),
            vmem_limit_bytes=VMEM_LIMIT),
    )(slopes, pa, k_cmp, vct, ovt, spread, bias)


def _sel_win_kernel(ids_ref, cnt_ref, slopes_ref, qz_ref, kv_ref, sel_ref,
                    ocmp_ref, gate_ref, bias_ref, y_ref, m_ref, l_ref, acc_ref, s_scr, w_scr,
                    win_ref, *, n_chunks):
    b = pl.program_id(0)
    g = pl.program_id(1)
    qi = pl.program_id(2)
    nq = pl.num_programs(2)
    qb = Q_BLOCK
    q0 = qi * qb
    per_piece = SEL_PIECE // SLC_BLOCK
    pieces_per_chunk = SEL_CHUNK // SEL_PIECE

    gw = NSA_REP * NSA_DH
    q = _stack_heads(qz_ref[0, :, :gw])
    slopes = [slopes_ref[g * NSA_REP + r] for r in range(NSA_REP)]
    ks_cols, vs_cols, kw_cols, vw_cols = (slice(i * NSA_DH, (i + 1) * NSA_DH) for i in range(4))

    m_ref[...] = jnp.full_like(m_ref, M_FLOOR)
    l_ref[...] = jnp.zeros_like(l_ref)
    acc_ref[...] = jnp.zeros_like(acc_ref)
    step = (b * NSA_GROUPS + g) * nq + qi
    n_active = cnt_ref[3 * step]
    n_big = cnt_ref[3 * step + 1]
    n_small = cnt_ref[3 * step + 2]
    id_base = step * n_chunks

    def stage_scores(row0, st, mask, scr=s_scr):
        n = st.shape[0]
        tops = []
        for r in range(NSA_REP):
            s = jnp.where(mask, st[:, _head_cols(r)] + bias_ref[0, :n, _head_cols(r)], MASKED)
            scr[row0:row0 + n, _head_cols(r)] = s
            tops.append(jnp.max(s, axis=0, keepdims=True))
        return jnp.concatenate(tops, axis=1)

    def piece_rows(cols, c):
        return kv_ref[0, pl.ds(pl.multiple_of(c * SEL_PIECE, SEL_PIECE), SEL_PIECE), cols]

    def piece_mask(c, threshold):
        group = lax.shift_right_logical(c, int(np.log2(pieces_per_chunk)))
        rows8 = sel_ref[0, 0, 0, pl.ds(pl.multiple_of(group * 8, 8), 8), :]
        within = c & (pieces_per_chunk - 1)
        flags = []
        for i in range(per_piece):
            row = rows8[i:i + 1]
            for w in range(1, pieces_per_chunk):
                row = jnp.where(within == w, rows8[w * per_piece + i:w * per_piece + i + 1], row)
            flags.append(jnp.broadcast_to(row, (SLC_BLOCK, qb)))
        chosen = jnp.concatenate(flags, axis=0)
        rel = _iota((SEL_PIECE, qb), 1) - _iota((SEL_PIECE, qb), 0)
        return (chosen > threshold) & (rel >= c * SEL_PIECE - q0)

    def stage_pair(slot, ca, cb, thr_a, thr_b):
        keys = jnp.concatenate([piece_rows(ks_cols, ca), piece_rows(ks_cols, cb)], axis=0)
        st = lax.dot_general(keys, q, _NT, preferred_element_type=F32)
        out = []
        for h, (c, thr) in enumerate(((ca, thr_a), (cb, thr_b))):
            rows = slice(h * SEL_PIECE, (h + 1) * SEL_PIECE)
            top = stage_scores((2 * slot + h) * SEL_PIECE, st[rows], piece_mask(c, thr))
            shift = _chunk_shift(slopes, c * SEL_PIECE - q0)
            out.append((top + shift, shift))
        return out

    def update(pieces, thresholds):
        pairs = [(pieces[2 * k], pieces[2 * k + 1]) for k in range(len(pieces) // 2)]
        staged = [stage_pair(k, ca, cb, thresholds[2 * k], thresholds[2 * k + 1])
                  for k, (ca, cb) in enumerate(pairs)]
        m_old = m_ref[...]
        m_new = functools.reduce(jnp.maximum, [top for pair in staged for top, _ in pair], m_old)
        alpha = jnp.exp2(m_old - m_new)
        l = alpha * l_ref[...]
        acc = alpha * acc_ref[...]
        for k, ((ca, cb), pair) in enumerate(zip(pairs, staged)):
            halves = []
            for h, (_, shift) in enumerate(pair):
                row0 = (2 * k + h) * SEL_PIECE
                p = jnp.exp2(s_scr[row0:row0 + SEL_PIECE, :] - (m_new - shift))
                l = l + jnp.sum(p, axis=0, keepdims=True)
                halves.append(p.astype(BF16))
            values = jnp.concatenate([piece_rows(vs_cols, ca), piece_rows(vs_cols, cb)], axis=0)
            acc = acc + lax.dot_general(values, jnp.concatenate(halves, axis=0), _TN,
                                        preferred_element_type=F32)
        m_ref[...] = m_new
        l_ref[...] = l
        acc_ref[...] = acc

    def run_step(first, n):
        pieces = [ids_ref[id_base + jnp.minimum(first + k, n_chunks - 1)] for k in range(n)]
        update(pieces, [jnp.where(first + k < n_active, 0.5, 2.0) for k in range(n)])

    def big_step(i, carry):
        run_step(BIG_STEP * (i + 1), BIG_STEP)
        return carry

    def small_step(i, carry):
        run_step(BIG_STEP * (n_big + 1) + SMALL_STEP * i, SMALL_STEP)
        return carry

    wk = WINDOW + qb
    start = pl.multiple_of(jnp.maximum(q0 - WINDOW, 0), qb)
    st = lax.dot_general(kv_ref[0, pl.ds(start, wk), kw_cols], q, _NT, preferred_element_type=F32)
    rel = _iota((wk, qb), 1) - _iota((wk, qb), 0)
    m_win = stage_scores(0, st, (rel >= start - q0) & (rel < WINDOW + start - q0), w_scr)
    p = jnp.exp2(w_scr[...] - m_win)
    l_win = jnp.sum(p, axis=0, keepdims=True)
    win_ref[...] = lax.dot_general(kv_ref[0, pl.ds(start, wk), vw_cols], p.astype(BF16), _TN,
                                   preferred_element_type=F32) * (1.0 / l_win)

    run_step(0, BIG_STEP)
    lax.fori_loop(0, n_big, big_step, 0)
    lax.fori_loop(0, n_small, small_step, 0)

    gates = _sigmoid(gate_ref[0, 0])
    gate_row = lambda br: jnp.concatenate(
        [gates[3 * r + br:3 * r + br + 1] for r in range(NSA_REP)], axis=1)
    l_sel = l_ref[...]
    comb = (gate_row(0) * ocmp_ref[0, 0, 0]
            + gate_row(1) * (acc_ref[...] * jnp.where(l_sel > 0.0, 1.0 / l_sel, 0.0))
            + gate_row(2) * win_ref[...])
    for r in range(NSA_REP):
        cols = slice(r * NSA_DH, (r + 1) * NSA_DH)
        z = qz_ref[0, :, gw + r * NSA_DH:gw + (r + 1) * NSA_DH].astype(F32)
        y_ref[0, :, cols] = (z * _sigmoid(z) * comb[:, _head_cols(r)].T).astype(y_ref.dtype)


def _sel_win(ids, cnt, slopes, pa, sel, ocmp, gates_t):
    b, s, _ = pa.shape
    nq = s // Q_BLOCK
    gw = NSA_REP * NSA_DH
    kv_col = 2 * NSA_GROUPS
    per_q = lambda arr: pl.BlockSpec((1, 1, 1) + arr.shape[3:],
                                     lambda bi, gi, qi, *_: (bi, gi, qi, 0, 0))
    grid_spec = pltpu.PrefetchScalarGridSpec(
        num_scalar_prefetch=3,
        grid=(b, NSA_GROUPS, nq),
        in_specs=[
            pl.BlockSpec((1, Q_BLOCK, 2 * gw), lambda bi, gi, qi, *_: (bi, qi, gi)),
            pl.BlockSpec((1, s, gw), lambda bi, gi, qi, *_: (bi, 0, kv_col + gi)),
            per_q(sel), per_q(ocmp),
            pl.BlockSpec((1, 1, NSA_REP * 3, Q_BLOCK), lambda bi, gi, qi, *_: (bi, gi, 0, qi)),
            pl.BlockSpec((1, WINDOW + Q_BLOCK, gw), lambda bi, gi, qi, *_: (gi, 0, 0)),
        ],
        out_specs=pl.BlockSpec((1, Q_BLOCK, gw), lambda bi, gi, qi, *_: (bi, qi, gi)),
        scratch_shapes=[
            pltpu.VMEM((1, gw), F32),
            pltpu.VMEM((1, gw), F32),
            pltpu.VMEM((NSA_DH, gw), F32),
            pltpu.VMEM((BIG_STEP * SEL_PIECE, gw), F32),
            pltpu.VMEM((WINDOW + Q_BLOCK, gw), F32),
            pltpu.VMEM((NSA_DH, gw), F32),
        ],
    )
    return pl.pallas_call(
        functools.partial(_sel_win_kernel, n_chunks=s // SEL_PIECE),
        grid_spec=grid_spec,
        out_shape=jax.ShapeDtypeStruct((b, s, NSA_GROUPS * gw), BF16),
        compiler_params=pltpu.CompilerParams(
            dimension_semantics=("arbitrary", "arbitrary", "arbitrary"),
            vmem_limit_bytes=VMEM_LIMIT),
    )(ids, cnt, slopes, pa, pa, sel, ocmp, gates_t, _alibi_table(slopes, WINDOW + Q_BLOCK, 1))


def _active_chunk_lists(counts):
    b, g, nq, _, n_slc = counts.shape
    per_chunk = SEL_PIECE // SLC_BLOCK
    n_chunks = n_slc // per_chunk
    chunk = jnp.arange(n_chunks, dtype=jnp.int32)
    causal = chunk[None, :] <= (jnp.arange(nq, dtype=jnp.int32)[:, None] * Q_BLOCK) // SEL_PIECE
    active = counts[:, :, :, 0].reshape(b, g, nq, n_chunks, per_chunk).max(axis=-1) > 0.5
    active = active & causal
    ids = jnp.sort(jnp.where(active, chunk, chunk + n_chunks), axis=-1) % n_chunks
    n = active.sum(axis=-1, dtype=jnp.int32)
    left = jnp.maximum(n - BIG_STEP, 0)
    full, rest = left // BIG_STEP, left % BIG_STEP
    one_more = rest > 2 * SMALL_STEP
    n_big = full + one_more
    n_small = jnp.where(one_more, 0, (rest + SMALL_STEP - 1) // SMALL_STEP)
    return ids.reshape(-1), jnp.stack([n, n_big, n_small], axis=-1).astype(jnp.int32).reshape(-1)


def kernel(x, norm_g, ret_w_in, ret_w_out, nsa_w_in, nsa_cmp_pos_k, nsa_cmp_w1_k, nsa_cmp_w2_k,
           nsa_cmp_pos_v, nsa_cmp_w1_v, nsa_cmp_w2_v, nsa_w_out, final_g):
    b, s, d = x.shape
    m = b * s
    x2 = x.reshape(m, d)

    qk_w = 2 * RET_HEADS * RET_DK
    v_w = RET_HEADS * RET_DV
    w_in = ret_w_in[0].astype(BF16)
    scale = jnp.concatenate([jnp.ones((qk_w // 2,), F32), jnp.full((qk_w // 2,), RET_DK ** -0.5, F32),
                             jnp.ones((2 * v_w,), F32)])
    proj = _norm_proj(x2, norm_g[0], w_in, scale, BF16, tm=1024, tn=2048)
    y = _retention(proj.reshape(b, s, qk_w + 2 * v_w))
    h1, hn1 = _out_proj(y.reshape(m, v_w), ret_w_out[0].astype(BF16), x2, norm_g[1],
                        last_layer=False)

    width = NSA_HEADS * NSA_DH
    kvw = NSA_GROUPS * NSA_DH
    n_gate = NSA_HEADS * 3
    qkv_w = width + 6 * kvw
    w_n = nsa_w_in[0]
    gw = NSA_REP * NSA_DH
    col = lambda start, width_: np.arange(start, start + width_)
    kv_start = lambda slot: width + slot * kvw
    z_start = qkv_w + n_gate
    order = [np.concatenate([col(gi * gw, gw), col(z_start + gi * gw, gw)])
             for gi in range(NSA_GROUPS)]
    order += [np.concatenate([col(kv_start(slot) + gi * NSA_DH, NSA_DH) for slot in (2, 3, 4, 5)])
              for gi in range(NSA_GROUPS)]
    order += [col(kv_start(0), kvw), col(kv_start(1), kvw)]
    order = np.concatenate(order)
    w_a = w_n[:, order].astype(BF16)
    scale_a = jnp.asarray(np.where(order < width, NSA_DH ** -0.5 * LOG2E, 1.0), F32)
    w_g = jnp.concatenate([w_n[:, qkv_w:qkv_w + n_gate], jnp.zeros((d, 128 - n_gate), F32)],
                          axis=1).astype(BF16)
    pa = _proj(hn1, w_a, scale_a, BF16, tm=min(2048, m), tn=1024)
    pa = pa.reshape(b, s, qkv_w + width)
    pg = _proj(hn1, w_g, jnp.ones((128,), F32), F32, tm=min(2048, m), tn=128)

    raw_k = slice(qkv_w + width - 2 * kvw, qkv_w + width - kvw)
    raw_v = slice(qkv_w + width - kvw, qkv_w + width)
    k_cmp = _compress(pa[:, :, raw_k], nsa_cmp_pos_k[0], nsa_cmp_w1_k[0], nsa_cmp_w2_k[0])
    v_cmp = _compress(pa[:, :, raw_v], nsa_cmp_pos_v[0], nsa_cmp_w1_v[0], nsa_cmp_w2_v[0])
    slopes = jnp.exp2(-8.0 * jnp.arange(1, NSA_HEADS + 1, dtype=F32) / NSA_HEADS) * LOG2E
    ocmp, sel, counts = _cmp_select(pa, k_cmp, v_cmp, slopes)
    ids, cnt = _active_chunk_lists(counts)
    gates_t = pg[:, :n_gate].reshape(b, s, NSA_GROUPS, NSA_REP * 3)
    gates_t = gates_t.transpose(0, 2, 3, 1)
    y = _sel_win(ids, cnt, slopes, pa, sel, ocmp, gates_t)
    out, = _out_proj(y.reshape(m, width), nsa_w_out[0].astype(BF16), h1, final_g, last_layer=True)
    return out.reshape(b, s, d)
```

```python
import functools

import jax
import jax.numpy as jnp
import numpy as np
from jax import lax
from jax.experimental import pallas as pl
from jax.experimental.pallas import tpu as pltpu

F32 = jnp.float32
BF16 = jnp.bfloat16

EPS = 1e-6
LOG2E = 1.4426950408889634
MASKED = -2e30
M_FLOOR = -1e30
TAKEN = -3e38
N_FORCED = 3

RET_HEADS = 8
RET_DK = 256
RET_DV = 512
RET_CHUNK = 256

NSA_HEADS = 16
NSA_GROUPS = 4
NSA_REP = 4
NSA_DH = 128
CMP_LEN = 32
CMP_STRIDE = 16
SLC_BLOCK = 64
TOP_N = 16
WINDOW = 512
Q_BLOCK = 128
CMP_Q_PER_STEP = 2
SEL_CHUNK = 256
SEL_PIECE = 128
BIG_STEP = 10
SMALL_STEP = 2

VMEM_LIMIT = 56 * 1024 * 1024

_NT = (((1,), (1,)), ((), ()))
_TN = (((0,), (0,)), ((), ()))


def _sigmoid(x):
    return 1.0 / (1.0 + jnp.exp2(-LOG2E * x))


def _iota(shape, dim):
    return lax.broadcasted_iota(jnp.int32, shape, dim)


def _norm_proj_kernel(x_ref, g_ref, w_ref, s_ref, o_ref, xn_ref):
    @pl.when(pl.program_id(1) == 0)
    def _():
        x = x_ref[...]
        ms = jnp.mean(x * x, axis=-1, keepdims=True)
        xn_ref[...] = (x * lax.rsqrt(ms + EPS) * g_ref[...]).astype(xn_ref.dtype)

    acc = jnp.dot(xn_ref[...], w_ref[...], preferred_element_type=F32)
    o_ref[...] = (acc * s_ref[...]).astype(o_ref.dtype)


def _norm_proj(x2, g, w, col_scale, out_dtype, tm, tn):
    m, k = x2.shape
    n = w.shape[1]
    return pl.pallas_call(
        _norm_proj_kernel,
        grid=(m // tm, n // tn),
        in_specs=[
            pl.BlockSpec((tm, k), lambda i, j: (i, 0)),
            pl.BlockSpec((1, k), lambda i, j: (0, 0)),
            pl.BlockSpec((k, tn), lambda i, j: (0, j)),
            pl.BlockSpec((1, tn), lambda i, j: (0, j)),
        ],
        out_specs=pl.BlockSpec((tm, tn), lambda i, j: (i, j)),
        out_shape=jax.ShapeDtypeStruct((m, n), out_dtype),
        scratch_shapes=[pltpu.VMEM((tm, k), BF16)],
        compiler_params=pltpu.CompilerParams(
            dimension_semantics=("arbitrary", "arbitrary"), vmem_limit_bytes=VMEM_LIMIT),
    )(x2, g.reshape(1, k), w, col_scale.reshape(1, n))


def _proj_kernel(x_ref, w_ref, s_ref, o_ref):
    acc = jnp.dot(x_ref[...], w_ref[...], preferred_element_type=F32)
    o_ref[...] = (acc * s_ref[...]).astype(o_ref.dtype)


def _proj(x2, w, col_scale, out_dtype, tm, tn):
    m, k = x2.shape
    n = w.shape[1]
    return pl.pallas_call(
        _proj_kernel,
        grid=(m // tm, n // tn),
        in_specs=[
            pl.BlockSpec((tm, k), lambda i, j: (i, 0)),
            pl.BlockSpec((k, tn), lambda i, j: (0, j)),
            pl.BlockSpec((1, tn), lambda i, j: (0, j)),
        ],
        out_specs=pl.BlockSpec((tm, tn), lambda i, j: (i, j)),
        out_shape=jax.ShapeDtypeStruct((m, n), out_dtype),
        compiler_params=pltpu.CompilerParams(
            dimension_semantics=("arbitrary", "arbitrary"), vmem_limit_bytes=VMEM_LIMIT),
    )(x2, w, col_scale.reshape(1, n))


def _out_proj_kernel(y_ref, w_ref, res_ref, g_ref, *o_refs, last_layer):
    h = res_ref[...] + jnp.dot(y_ref[...], w_ref[...], preferred_element_type=F32)
    ms = jnp.mean(h * h, axis=-1, keepdims=True)
    hn = h * lax.rsqrt(ms + EPS) * g_ref[...]
    if last_layer:
        o_refs[0][...] = hn
    else:
        o_refs[0][...] = h
        o_refs[1][...] = hn.astype(o_refs[1].dtype)


def _out_proj(y2, w, res2, g, last_layer, tm=512):
    m, kd = y2.shape
    n = w.shape[1]
    row_blk = pl.BlockSpec((tm, n), lambda i: (i, 0))
    out_shape = [jax.ShapeDtypeStruct((m, n), F32)]
    if not last_layer:
        out_shape.append(jax.ShapeDtypeStruct((m, n), BF16))
    return pl.pallas_call(
        functools.partial(_out_proj_kernel, last_layer=last_layer),
        grid=(m // tm,),
        in_specs=[
            pl.BlockSpec((tm, kd), lambda i: (i, 0)),
            pl.BlockSpec((kd, n), lambda i: (0, 0), pipeline_mode=pl.Buffered(1)),
            row_blk,
            pl.BlockSpec((1, n), lambda i: (0, 0)),
        ],
        out_specs=[row_blk] * len(out_shape),
        out_shape=out_shape,
        compiler_params=pltpu.CompilerParams(
            dimension_semantics=("arbitrary",), vmem_limit_bytes=VMEM_LIMIT),
    )(y2, w, res2, g.reshape(1, n))


def _retention_kernel(q_ref, k_ref, v_ref, gate_ref, di_ref, qd_ref, kd_ref, cd_ref, y_ref,
                      state_ref, *, chunk, n_sub):
    @pl.when(pl.program_id(2) == 0)
    def _():
        state_ref[...] = jnp.zeros_like(state_ref)

    di = di_ref[0]
    qd = qd_ref[0]
    kd = kd_ref[0]
    cd = cd_ref[0]
    for i in range(n_sub):
        rows = pl.ds(i * chunk, chunk)
        q = q_ref[0, rows, :]
        k = k_ref[0, rows, :]
        v = v_ref[0, rows, :]
        state = state_ref[...]
        s = lax.dot_general(q, k, _NT, preferred_element_type=F32) * di
        o = jnp.dot(s.astype(BF16), v, preferred_element_type=F32)
        o = o + jnp.dot(q, state.astype(BF16), preferred_element_type=F32) * qd
        k_dec = (k.astype(F32) * kd).astype(BF16)
        state_ref[...] = state * cd + lax.dot_general(k_dec, v, _TN, preferred_element_type=F32)
        mu = jnp.mean(o, axis=-1, keepdims=True)
        oc = o - mu
        var = jnp.mean(oc * oc, axis=-1, keepdims=True)
        on = oc * lax.rsqrt(var + EPS)
        gt = gate_ref[0, rows, :].astype(F32)
        y_ref[0, rows, :] = (gt * _sigmoid(gt) * on).astype(y_ref.dtype)


def _retention(proj, tokens_per_step=1024):
    b, s, _ = proj.shape
    h, c = RET_HEADS, RET_CHUNK
    t = min(tokens_per_step, s)
    log_g = jnp.log1p(-jnp.exp2(-5.0 - jnp.arange(h, dtype=F32)))
    idx = jnp.arange(c, dtype=F32)
    diff = idx[:, None] - idx[None, :]
    decay_intra = jnp.where(diff >= 0, jnp.exp(diff[None] * log_g[:, None, None]), 0.0)
    q_decay = jnp.exp((idx[None, :] + 1.0) * log_g[:, None])[:, :, None]
    k_decay = jnp.exp((c - 1.0 - idx[None, :]) * log_g[:, None])[:, :, None]
    chunk_decay = jnp.exp(c * log_g)[:, None, None]
    nk = RET_HEADS * RET_DK // RET_DK
    nv = 2 * RET_HEADS * RET_DK // RET_DV
    return pl.pallas_call(
        functools.partial(_retention_kernel, chunk=c, n_sub=t // c),
        grid=(b, h, s // t),
        in_specs=[
            pl.BlockSpec((1, t, RET_DK), lambda bi, hi, ti: (bi, ti, hi)),
            pl.BlockSpec((1, t, RET_DK), lambda bi, hi, ti: (bi, ti, nk + hi)),
            pl.BlockSpec((1, t, RET_DV), lambda bi, hi, ti: (bi, ti, nv + hi)),
            pl.BlockSpec((1, t, RET_DV), lambda bi, hi, ti: (bi, ti, nv + h + hi)),
            pl.BlockSpec((1, c, c), lambda bi, hi, ti: (hi, 0, 0)),
            pl.BlockSpec((1, c, 1), lambda bi, hi, ti: (hi, 0, 0)),
            pl.BlockSpec((1, c, 1), lambda bi, hi, ti: (hi, 0, 0)),
            pl.BlockSpec((1, 1, 1), lambda bi, hi, ti: (hi, 0, 0)),
        ],
        out_specs=pl.BlockSpec((1, t, RET_DV), lambda bi, hi, ti: (bi, ti, hi)),
        out_shape=jax.ShapeDtypeStruct((b, s, h * RET_DV), BF16),
        scratch_shapes=[pltpu.VMEM((RET_DK, RET_DV), F32)],
        compiler_params=pltpu.CompilerParams(
            dimension_semantics=("arbitrary", "arbitrary", "arbitrary"),
            vmem_limit_bytes=VMEM_LIMIT),
    )(proj, proj, proj, proj, decay_intra, q_decay, k_decay, chunk_decay)


def _compress_kernel(x_ref, xn_ref, pos_ref, w1_ref, w2_ref, o_ref):
    half = CMP_LEN // 2
    width = NSA_GROUPS * NSA_DH
    for g in range(NSA_GROUPS):
        acc = jnp.zeros((x_ref.shape[1], NSA_DH), F32)
        for l in range(half):
            cols = slice(l * width + g * NSA_DH, l * width + (g + 1) * NSA_DH)
            xa = (x_ref[0, :, cols].astype(F32) + pos_ref[l:l + 1, :]).astype(BF16)
            acc += jnp.dot(xa, w1_ref[l * NSA_DH:(l + 1) * NSA_DH, :], preferred_element_type=F32)
            lb = half + l
            xb = (xn_ref[0, :, cols].astype(F32) + pos_ref[lb:lb + 1, :]).astype(BF16)
            acc += jnp.dot(xb, w1_ref[lb * NSA_DH:(lb + 1) * NSA_DH, :], preferred_element_type=F32)
        hid = (acc * _sigmoid(acc)).astype(BF16)
        out = jnp.dot(hid, w2_ref[...], preferred_element_type=F32)
        o_ref[0, :, g * NSA_DH:(g + 1) * NSA_DH] = out.astype(o_ref.dtype)


def _compress(raw, pos, w1, w2):
    b, s, width = raw.shape
    rows = s // CMP_STRIDE
    x = raw.reshape(b, rows, CMP_STRIDE * width)
    xn = jnp.concatenate([x[:, 1:], jnp.zeros_like(x[:, :1])], axis=1)
    tr = min(256, rows)
    blk = pl.BlockSpec((1, tr, CMP_STRIDE * width), lambda bi, ri: (bi, ri, 0))
    return pl.pallas_call(
        _compress_kernel,
        grid=(b, rows // tr),
        in_specs=[
            blk, blk,
            pl.BlockSpec((CMP_LEN, NSA_DH), lambda bi, ri: (0, 0)),
            pl.BlockSpec((CMP_LEN * NSA_DH, NSA_DH), lambda bi, ri: (0, 0)),
            pl.BlockSpec((NSA_DH, NSA_DH), lambda bi, ri: (0, 0)),
        ],
        out_specs=pl.BlockSpec((1, tr, width), lambda bi, ri: (bi, ri, 0)),
        out_shape=jax.ShapeDtypeStruct((b, rows, width), BF16),
        compiler_params=pltpu.CompilerParams(
            dimension_semantics=("arbitrary", "arbitrary"), vmem_limit_bytes=VMEM_LIMIT),
    )(x, xn, pos, w1.astype(BF16), w2.astype(BF16))


def _stack_heads(qblk):
    return jnp.concatenate(
        [qblk[:, r * NSA_DH:(r + 1) * NSA_DH] for r in range(NSA_REP)], axis=0)


def _head_cols(r):
    return slice(r * Q_BLOCK, (r + 1) * Q_BLOCK)


def _alibi_table(slopes, n_keys, key_stride):
    j = jnp.arange(n_keys, dtype=F32)[:, None] * key_stride
    i = jnp.arange(Q_BLOCK, dtype=F32)[None, :]
    table = slopes.reshape(NSA_GROUPS, 1, NSA_REP, 1) * (j - i)[None, :, None, :]
    return table.reshape(NSA_GROUPS, n_keys, NSA_REP * Q_BLOCK)


def _chunk_shift(slopes, delta):
    row = jnp.full((1, Q_BLOCK), delta, jnp.int32).astype(F32)
    return jnp.concatenate([s * row for s in slopes], axis=1)


def _cmp_select_kernel(slopes_ref, q_ref, kc_ref, vct_ref, ovt_ref, spread_ref, bias_ref, ocmp_ref,
                       sel_ref, cnt_ref, s_scrs, *, ck, nq):
    g = pl.program_id(1)
    qi_first = pl.program_id(2) * CMP_Q_PER_STEP
    qb = Q_BLOCK
    n_ck, n_slc = ovt_ref.shape[0], ovt_ref.shape[1]
    slopes = [slopes_ref[g * NSA_REP + r] for r in range(NSA_REP)]
    rel = _iota((ck, qb), 1) - CMP_STRIDE * _iota((ck, qb), 0)

    def attend_and_select(chunks, rows, sub):
        q0 = (qi_first + sub) * qb
        q = _stack_heads(q_ref[0, sub * qb:(sub + 1) * qb, :])
        chunk_delta = lambda off: CMP_STRIDE * off + (CMP_LEN - 1) - q0
        s_scr = s_scrs.at[sub]
        ocmp_out, sel_out, cnt_out = ocmp_ref.at[0, 0, sub], sel_ref.at[0, 0, sub], cnt_ref.at[0, 0, sub]
        tops = []
        for c in range(chunks):
            off = c * ck
            st = lax.dot_general(kc_ref[0, off:off + ck, :], q, _NT, preferred_element_type=F32)
            valid = rel >= chunk_delta(off)
            top_c = []
            for r in range(NSA_REP):
                s = jnp.where(valid, st[:, _head_cols(r)] + bias_ref[0, :, _head_cols(r)], MASKED)
                s_scr[off:off + ck, _head_cols(r)] = s
                top_c.append(jnp.max(s, axis=0, keepdims=True))
            tops.append(jnp.concatenate(top_c, axis=1) + _chunk_shift(slopes, chunk_delta(off)))
        m_all = functools.reduce(jnp.maximum, tops, jnp.full((1, NSA_REP * qb), M_FLOOR, F32))

        l = jnp.zeros((1, NSA_REP * qb), F32)
        o = jnp.zeros((NSA_DH, NSA_REP * qb), F32)
        imp = jnp.zeros((rows, NSA_REP * qb), F32)
        for c in range(chunks):
            off = c * ck
            shifted_m = m_all - _chunk_shift(slopes, chunk_delta(off))
            p = jnp.exp2(s_scr[off:off + ck, :] - shifted_m)
            pb = p.astype(BF16)
            l = l + jnp.sum(p, axis=0, keepdims=True)
            o = o + jnp.dot(vct_ref[0, 0, c], pb, preferred_element_type=F32)
            imp = imp + jnp.dot(ovt_ref[c, :rows, :], pb, preferred_element_type=F32)
        inv = jnp.where(l > 0.0, 1.0 / l, 0.0)
        ocmp_out[...] = o * inv
        imp_w = imp * inv
        imp_t = imp_w[:, _head_cols(0)]
        for r in range(1, NSA_REP):
            imp_t = imp_t + imp_w[:, _head_cols(r)]

        jrow = _iota((rows, qb), 0)
        cur = lax.shift_right_logical(q0 + _iota((rows, qb), 1), int(np.log2(SLC_BLOCK)))
        forced = (jrow == 0) | (jrow == cur) | (jrow == cur - 1)
        score = jnp.where(forced, TAKEN, jnp.where(jrow <= cur, imp_t, -1.0))
        for _ in range(min(TOP_N, rows) - N_FORCED):
            best = jnp.max(score, axis=0, keepdims=True)
            first = jnp.min(jnp.where(score == best, jrow, rows), axis=0, keepdims=True)
            score = jnp.where(jrow == first, TAKEN, score)
        chosen = jnp.where(score == TAKEN, 1.0, 0.0).astype(BF16)
        sel_out[...] = jnp.dot(spread_ref[:, :rows], chosen, preferred_element_type=F32)
        cnt_out[:, :rows] = lax.dot_general(jnp.ones((8, qb), BF16), chosen, _NT,
                                            preferred_element_type=F32)
        if rows < n_slc:
            cnt_out[:, rows:] = jnp.zeros((8, n_slc - rows), F32)

    if nq % n_ck == 0 and n_slc % n_ck == 0:
        for k in range(1, n_ck + 1):
            in_share = (qi_first >= (k - 1) * (nq // n_ck)) & (qi_first < k * (nq // n_ck))

            @pl.when(in_share)
            def _(k=k):
                for sub in range(CMP_Q_PER_STEP):
                    attend_and_select(k, k * (n_slc // n_ck), sub)
    else:
        for sub in range(CMP_Q_PER_STEP):
            attend_and_select(n_ck, n_slc, sub)


def _cmp_select(pa, k_cmp, v_cmp, slopes):
    b, s, _ = pa.shape
    nq = s // Q_BLOCK
    n_cmp = k_cmp.shape[1]
    n_slc = s // SLC_BLOCK
    ck = min(256, n_cmp)
    n_ck = n_cmp // ck
    gw = NSA_REP * NSA_DH
    cstart = np.arange(n_cmp)[None, :] * CMP_STRIDE
    sstart = np.arange(n_slc)[:, None] * SLC_BLOCK
    real = np.arange(n_cmp)[None, :] < (s - CMP_LEN) // CMP_STRIDE + 1
    ovt = ((cstart < sstart + SLC_BLOCK) & (cstart + CMP_LEN > sstart) & real).astype(np.float32)
    ovt = jnp.asarray(ovt.reshape(n_slc, n_ck, ck).transpose(1, 0, 2), BF16)
    per_chunk = SEL_CHUNK // SLC_BLOCK
    n_sel_chunks = s // SEL_CHUNK
    spread = np.zeros((8 * n_sel_chunks, n_slc), np.float32)
    for j in range(n_slc):
        spread[8 * (j // per_chunk) + j % per_chunk, j] = 1.0
    spread = jnp.asarray(spread, BF16)
    vct = v_cmp.reshape(b, n_ck, ck, NSA_GROUPS, NSA_DH).transpose(0, 3, 1, 4, 2)
    bias = _alibi_table(slopes, ck, CMP_STRIDE)
    per_step = CMP_Q_PER_STEP
    assert nq % per_step == 0 and (nq // n_ck) % per_step == 0
    return pl.pallas_call(
        functools.partial(_cmp_select_kernel, ck=ck, nq=nq),
        grid=(b, NSA_GROUPS, nq // per_step),
        in_specs=[
            pl.BlockSpec(memory_space=pltpu.SMEM),
            pl.BlockSpec((1, per_step * Q_BLOCK, gw), lambda bi, gi, qi: (bi, qi, 2 * gi)),
            pl.BlockSpec((1, n_cmp, NSA_DH), lambda bi, gi, qi: (bi, 0, gi)),
            pl.BlockSpec((1, 1, n_ck, NSA_DH, ck), lambda bi, gi, qi: (bi, gi, 0, 0, 0)),
            pl.BlockSpec((n_ck, n_slc, ck), lambda bi, gi, qi: (0, 0, 0)),
            pl.BlockSpec((8 * n_sel_chunks, n_slc), lambda bi, gi, qi: (0, 0)),
            pl.BlockSpec((1, ck, gw), lambda bi, gi, qi: (gi, 0, 0)),
        ],
        out_specs=[
            pl.BlockSpec((1, 1, per_step, NSA_DH, gw), lambda bi, gi, qi: (bi, gi, qi, 0, 0)),
            pl.BlockSpec((1, 1, per_step, 8 * n_sel_chunks, Q_BLOCK),
                         lambda bi, gi, qi: (bi, gi, qi, 0, 0)),
            pl.BlockSpec((1, 1, per_step, 8, n_slc), lambda bi, gi, qi: (bi, gi, qi, 0, 0)),
        ],
        out_shape=[
            jax.ShapeDtypeStruct((b, NSA_GROUPS, nq, NSA_DH, gw), F32),
            jax.ShapeDtypeStruct((b, NSA_GROUPS, nq, 8 * n_sel_chunks, Q_BLOCK), F32),
            jax.ShapeDtypeStruct((b, NSA_GROUPS, nq, 8, n_slc), F32),
        ],
        scratch_shapes=[pltpu.VMEM((per_step, n_cmp, gw), F32)],
        compiler_params=pltpu.CompilerParams(
            dimension_semantics=("arbitrary", "arbitrary", "arbitrary"),
            vmem_limit_bytes=VMEM_LIMIT),
    )(slopes, pa, k_cmp, vct, ovt, spread, bias)


def _sel_win_kernel(ids_ref, cnt_ref, slopes_ref, qz_ref, kv_ref, sel_ref,
                    ocmp_ref, gate_ref, bias_ref, y_ref, m_ref, l_ref, acc_ref, s_scr, w_scr,
                    win_ref, *, n_chunks):
    b = pl.program_id(0)
    g = pl.program_id(1)
    qi = pl.program_id(2)
    nq = pl.num_programs(2)
    qb = Q_BLOCK
    q0 = qi * qb
    per_piece = SEL_PIECE // SLC_BLOCK
    pieces_per_chunk = SEL_CHUNK // SEL_PIECE

    gw = NSA_REP * NSA_DH
    q = _stack_heads(qz_ref[0, :, :gw])
    slopes = [slopes_ref[g * NSA_REP + r] for r in range(NSA_REP)]
    ks_cols, vs_cols, kw_cols, vw_cols = (slice(i * NSA_DH, (i + 1) * NSA_DH) for i in range(4))

    m_ref[...] = jnp.full_like(m_ref, M_FLOOR)
    l_ref[...] = jnp.zeros_like(l_ref)
    acc_ref[...] = jnp.zeros_like(acc_ref)
    step = (b * NSA_GROUPS + g) * nq + qi
    n_active = cnt_ref[3 * step]
    n_big = cnt_ref[3 * step + 1]
    n_small = cnt_ref[3 * step + 2]
    id_base = step * n_chunks

    def stage_scores(row0, st, mask, scr=s_scr):
        n = st.shape[0]
        tops = []
        for r in range(NSA_REP):
            s = jnp.where(mask, st[:, _head_cols(r)] + bias_ref[0, :n, _head_cols(r)], MASKED)
            scr[row0:row0 + n, _head_cols(r)] = s
            tops.append(jnp.max(s, axis=0, keepdims=True))
        return jnp.concatenate(tops, axis=1)

    def piece_rows(cols, c):
        return kv_ref[0, pl.ds(pl.multiple_of(c * SEL_PIECE, SEL_PIECE), SEL_PIECE), cols]

    def piece_mask(c, threshold):
        group = lax.shift_right_logical(c, int(np.log2(pieces_per_chunk)))
        rows8 = sel_ref[0, 0, 0, pl.ds(pl.multiple_of(group * 8, 8), 8), :]
        within = c & (pieces_per_chunk - 1)
        flags = []
        for i in range(per_piece):
            row = rows8[i:i + 1]
            for w in range(1, pieces_per_chunk):
                row = jnp.where(within == w, rows8[w * per_piece + i:w * per_piece + i + 1], row)
            flags.append(jnp.broadcast_to(row, (SLC_BLOCK, qb)))
        chosen = jnp.concatenate(flags, axis=0)
        rel = _iota((SEL_PIECE, qb), 1) - _iota((SEL_PIECE, qb), 0)
        return (chosen > threshold) & (rel >= c * SEL_PIECE - q0)

    def stage_pair(slot, ca, cb, thr_a, thr_b):
        keys = jnp.concatenate([piece_rows(ks_cols, ca), piece_rows(ks_cols, cb)], axis=0)
        st = lax.dot_general(keys, q, _NT, preferred_element_type=F32)
        out = []
        for h, (c, thr) in enumerate(((ca, thr_a), (cb, thr_b))):
            rows = slice(h * SEL_PIECE, (h + 1) * SEL_PIECE)
            top = stage_scores((2 * slot + h) * SEL_PIECE, st[rows], piece_mask(c, thr))
            shift = _chunk_shift(slopes, c * SEL_PIECE - q0)
            out.append((top + shift, shift))
        return out

    def update(pieces, thresholds):
        pairs = [(pieces[2 * k], pieces[2 * k + 1]) for k in range(len(pieces) // 2)]
        staged = [stage_pair(k, ca, cb, thresholds[2 * k], thresholds[2 * k + 1])
                  for k, (ca, cb) in enumerate(pairs)]
        m_old = m_ref[...]
        m_new = functools.reduce(jnp.maximum, [top for pair in staged for top, _ in pair], m_old)
        alpha = jnp.exp2(m_old - m_new)
        l = alpha * l_ref[...]
        acc = alpha * acc_ref[...]
        for k, ((ca, cb), pair) in enumerate(zip(pairs, staged)):
            halves = []
            for h, (_, shift) in enumerate(pair):
                row0 = (2 * k + h) * SEL_PIECE
                p = jnp.exp2(s_scr[row0:row0 + SEL_PIECE, :] - (m_new - shift))
                l = l + jnp.sum(p, axis=0, keepdims=True)
                halves.append(p.astype(BF16))
            values = jnp.concatenate([piece_rows(vs_cols, ca), piece_rows(vs_cols, cb)], axis=0)
            acc = acc + lax.dot_general(values, jnp.concatenate(halves, axis=0), _TN,
                                        preferred_element_type=F32)
        m_ref[...] = m_new
        l_ref[...] = l
        acc_ref[...] = acc

    def run_step(first, n):
        pieces = [ids_ref[id_base + jnp.minimum(first + k, n_chunks - 1)] for k in range(n)]
        update(pieces, [jnp.where(first + k < n_active, 0.5, 2.0) for k in range(n)])

    def big_step(i, carry):
        run_step(BIG_STEP * (i + 1), BIG_STEP)
        return carry

    def small_step(i, carry):
        run_step(BIG_STEP * (n_big + 1) + SMALL_STEP * i, SMALL_STEP)
        return carry

    wk = WINDOW + qb
    start = pl.multiple_of(jnp.maximum(q0 - WINDOW, 0), qb)
    st = lax.dot_general(kv_ref[0, pl.ds(start, wk), kw_cols], q, _NT, preferred_element_type=F32)
    rel = _iota((wk, qb), 1) - _iota((wk, qb), 0)
    m_win = stage_scores(0, st, (rel >= start - q0) & (rel < WINDOW + start - q0), w_scr)
    p = jnp.exp2(w_scr[...] - m_win)
    l_win = jnp.sum(p, axis=0, keepdims=True)
    win_ref[...] = lax.dot_general(kv_ref[0, pl.ds(start, wk), vw_cols], p.astype(BF16), _TN,
                                   preferred_element_type=F32) * (1.0 / l_win)

    run_step(0, BIG_STEP)
    lax.fori_loop(0, n_big, big_step, 0)
    lax.fori_loop(0, n_small, small_step, 0)

    gates = _sigmoid(gate_ref[0, 0])
    gate_row = lambda br: jnp.concatenate(
        [gates[3 * r + br:3 * r + br + 1] for r in range(NSA_REP)], axis=1)
    l_sel = l_ref[...]
    comb = (gate_row(0) * ocmp_ref[0, 0, 0]
            + gate_row(1) * (acc_ref[...] * jnp.where(l_sel > 0.0, 1.0 / l_sel, 0.0))
            + gate_row(2) * win_ref[...])
    for r in range(NSA_REP):
        cols = slice(r * NSA_DH, (r + 1) * NSA_DH)
        z = qz_ref[0, :, gw + r * NSA_DH:gw + (r + 1) * NSA_DH].astype(F32)
        y_ref[0, :, cols] = (z * _sigmoid(z) * comb[:, _head_cols(r)].T).astype(y_ref.dtype)


def _sel_win(ids, cnt, slopes, pa, sel, ocmp, gates_t):
    b, s, _ = pa.shape
    nq = s // Q_BLOCK
    gw = NSA_REP * NSA_DH
    kv_col = 2 * NSA_GROUPS
    per_q = lambda arr: pl.BlockSpec((1, 1, 1) + arr.shape[3:],
                                     lambda bi, gi, qi, *_: (bi, gi, qi, 0, 0))
    grid_spec = pltpu.PrefetchScalarGridSpec(
        num_scalar_prefetch=3,
        grid=(b, NSA_GROUPS, nq),
        in_specs=[
            pl.BlockSpec((1, Q_BLOCK, 2 * gw), lambda bi, gi, qi, *_: (bi, qi, gi)),
            pl.BlockSpec((1, s, gw), lambda bi, gi, qi, *_: (bi, 0, kv_col + gi)),
            per_q(sel), per_q(ocmp),
            pl.BlockSpec((1, 1, NSA_REP * 3, Q_BLOCK), lambda bi, gi, qi, *_: (bi, gi, 0, qi)),
            pl.BlockSpec((1, WINDOW + Q_BLOCK, gw), lambda bi, gi, qi, *_: (gi, 0, 0)),
        ],
        out_specs=pl.BlockSpec((1, Q_BLOCK, gw), lambda bi, gi, qi, *_: (bi, qi, gi)),
        scratch_shapes=[
            pltpu.VMEM((1, gw), F32),
            pltpu.VMEM((1, gw), F32),
            pltpu.VMEM((NSA_DH, gw), F32),
            pltpu.VMEM((BIG_STEP * SEL_PIECE, gw), F32),
            pltpu.VMEM((WINDOW + Q_BLOCK, gw), F32),
            pltpu.VMEM((NSA_DH, gw), F32),
        ],
    )
    return pl.pallas_call(
        functools.partial(_sel_win_kernel, n_chunks=s // SEL_PIECE),
        grid_spec=grid_spec,
        out_shape=jax.ShapeDtypeStruct((b, s, NSA_GROUPS * gw), BF16),
        compiler_params=pltpu.CompilerParams(
            dimension_semantics=("arbitrary", "arbitrary", "arbitrary"),
            vmem_limit_bytes=VMEM_LIMIT),
    )(ids, cnt, slopes, pa, pa, sel, ocmp, gates_t, _alibi_table(slopes, WINDOW + Q_BLOCK, 1))


def _active_chunk_lists(counts):
    b, g, nq, _, n_slc = counts.shape
    per_chunk = SEL_PIECE // SLC_BLOCK
    n_chunks = n_slc // per_chunk
    chunk = jnp.arange(n_chunks, dtype=jnp.int32)
    causal = chunk[None, :] <= (jnp.arange(nq, dtype=jnp.int32)[:, None] * Q_BLOCK) // SEL_PIECE
    active = counts[:, :, :, 0].reshape(b, g, nq, n_chunks, per_chunk).max(axis=-1) > 0.5
    active = active & causal
    ids = jnp.sort(jnp.where(active, chunk, chunk + n_chunks), axis=-1) % n_chunks
    n = active.sum(axis=-1, dtype=jnp.int32)
    left = jnp.maximum(n - BIG_STEP, 0)
    full, rest = left // BIG_STEP, left % BIG_STEP
    one_more = rest > 2 * SMALL_STEP
    n_big = full + one_more
    n_small = jnp.where(one_more, 0, (rest + SMALL_STEP - 1) // SMALL_STEP)
    return ids.reshape(-1), jnp.stack([n, n_big, n_small], axis=-1).astype(jnp.int32).reshape(-1)


def kernel(x, norm_g, ret_w_in, ret_w_out, nsa_w_in, nsa_cmp_pos_k, nsa_cmp_w1_k, nsa_cmp_w2_k,
           nsa_cmp_pos_v, nsa_cmp_w1_v, nsa_cmp_w2_v, nsa_w_out, final_g):
    b, s, d = x.shape
    m = b * s
    x2 = x.reshape(m, d)

    qk_w = 2 * RET_HEADS * RET_DK
    v_w = RET_HEADS * RET_DV
    w_in = ret_w_in[0].astype(BF16)
    scale = jnp.concatenate([jnp.ones((qk_w // 2,), F32), jnp.full((qk_w // 2,), RET_DK ** -0.5, F32),
                             jnp.ones((2 * v_w,), F32)])
    proj = _norm_proj(x2, norm_g[0], w_in, scale, BF16, tm=1024, tn=2048)
    y = _retention(proj.reshape(b, s, qk_w + 2 * v_w))
    h1, hn1 = _out_proj(y.reshape(m, v_w), ret_w_out[0].astype(BF16), x2, norm_g[1],
                        last_layer=False)

    width = NSA_HEADS * NSA_DH
    kvw = NSA_GROUPS * NSA_DH
    n_gate = NSA_HEADS * 3
    qkv_w = width + 6 * kvw
    w_n = nsa_w_in[0]
    gw = NSA_REP * NSA_DH
    kv_start = lambda slot: width + slot * kvw
    z_start = qkv_w + n_gate
    runs = []
    for gi in range(NSA_GROUPS):
        runs += [(gi * gw, gw), (z_start + gi * gw, gw)]
    for gi in range(NSA_GROUPS):
        runs += [(kv_start(slot) + gi * NSA_DH, NSA_DH) for slot in (2, 3, 4, 5)]
    runs += [(kv_start(0), kvw), (kv_start(1), kvw)]
    w_a = jnp.concatenate([w_n[:, c0:c0 + cw] for c0, cw in runs], axis=1).astype(BF16)
    scale_a = jnp.concatenate([jnp.full((cw,), NSA_DH ** -0.5 * LOG2E if c0 < width else 1.0, F32)
                               for c0, cw in runs])
    w_g = jnp.concatenate([w_n[:, qkv_w:qkv_w + n_gate], jnp.zeros((d, 128 - n_gate), F32)],
                          axis=1).astype(BF16)
    pa = _proj(hn1, w_a, scale_a, BF16, tm=min(2048, m), tn=1024)
    pa = pa.reshape(b, s, qkv_w + width)
    pg = _proj(hn1, w_g, jnp.ones((128,), F32), F32, tm=min(2048, m), tn=128)

    raw_k = slice(qkv_w + width - 2 * kvw, qkv_w + width - kvw)
    raw_v = slice(qkv_w + width - kvw, qkv_w + width)
    k_cmp = _compress(pa[:, :, raw_k], nsa_cmp_pos_k[0], nsa_cmp_w1_k[0], nsa_cmp_w2_k[0])
    v_cmp = _compress(pa[:, :, raw_v], nsa_cmp_pos_v[0], nsa_cmp_w1_v[0], nsa_cmp_w2_v[0])
    slopes = jnp.exp2(-8.0 * jnp.arange(1, NSA_HEADS + 1, dtype=F32) / NSA_HEADS) * LOG2E
    ocmp, sel, counts = _cmp_select(pa, k_cmp, v_cmp, slopes)
    ids, cnt = _active_chunk_lists(counts)
    gates_t = pg[:, :n_gate].reshape(b, s, NSA_GROUPS, NSA_REP * 3)
    gates_t = gates_t.transpose(0, 2, 3, 1)
    y = _sel_win(ids, cnt, slopes, pa, sel, ocmp, gates_t)
    out, = _out_proj(y.reshape(m, width), nsa_w_out[0].astype(BF16), h1, final_g, last_layer=True)
    return out.reshape(b, s, d)
```

```python
import functools

import jax
import jax.numpy as jnp
import numpy as np
from jax import lax
from jax.experimental import pallas as pl
from jax.experimental.pallas import tpu as pltpu

F32 = jnp.float32
BF16 = jnp.bfloat16

EPS = 1e-6
LOG2E = 1.4426950408889634
MASKED = -2e30
M_FLOOR = -1e30
TAKEN = -3e38
N_FORCED = 3

RET_HEADS = 8
RET_DK = 256
RET_DV = 512
RET_CHUNK = 256

NSA_HEADS = 16
NSA_GROUPS = 4
NSA_REP = 4
NSA_DH = 128
CMP_LEN = 32
CMP_STRIDE = 16
SLC_BLOCK = 64
TOP_N = 16
WINDOW = 512
Q_BLOCK = 128
CMP_Q_PER_STEP = 2
SEL_Q_PER_STEP = 2
SEL_CHUNK = 256
SEL_PIECE = 128
BIG_STEP = 10
SMALL_STEP = 2

VMEM_LIMIT = 56 * 1024 * 1024

_NT = (((1,), (1,)), ((), ()))
_TN = (((0,), (0,)), ((), ()))


def _sigmoid(x):
    return 1.0 / (1.0 + jnp.exp2(-LOG2E * x))


def _iota(shape, dim):
    return lax.broadcasted_iota(jnp.int32, shape, dim)


def _norm_proj_kernel(x_ref, g_ref, w_ref, s_ref, o_ref, xn_ref):
    @pl.when(pl.program_id(1) == 0)
    def _():
        x = x_ref[...]
        ms = jnp.mean(x * x, axis=-1, keepdims=True)
        xn_ref[...] = (x * lax.rsqrt(ms + EPS) * g_ref[...]).astype(xn_ref.dtype)

    acc = jnp.dot(xn_ref[...], w_ref[...], preferred_element_type=F32)
    o_ref[...] = (acc * s_ref[...]).astype(o_ref.dtype)


def _norm_proj(x2, g, w, col_scale, out_dtype, tm, tn):
    m, k = x2.shape
    n = w.shape[1]
    return pl.pallas_call(
        _norm_proj_kernel,
        grid=(m // tm, n // tn),
        in_specs=[
            pl.BlockSpec((tm, k), lambda i, j: (i, 0)),
            pl.BlockSpec((1, k), lambda i, j: (0, 0)),
            pl.BlockSpec((k, tn), lambda i, j: (0, j)),
            pl.BlockSpec((1, tn), lambda i, j: (0, j)),
        ],
        out_specs=pl.BlockSpec((tm, tn), lambda i, j: (i, j)),
        out_shape=jax.ShapeDtypeStruct((m, n), out_dtype),
        scratch_shapes=[pltpu.VMEM((tm, k), BF16)],
        compiler_params=pltpu.CompilerParams(
            dimension_semantics=("arbitrary", "arbitrary"), vmem_limit_bytes=VMEM_LIMIT),
    )(x2, g.reshape(1, k), w, col_scale.reshape(1, n))


def _proj_kernel(x_ref, w_ref, s_ref, o_ref):
    acc = jnp.dot(x_ref[...], w_ref[...], preferred_element_type=F32)
    o_ref[...] = (acc * s_ref[...]).astype(o_ref.dtype)


def _proj(x2, w, col_scale, out_dtype, tm, tn):
    m, k = x2.shape
    n = w.shape[1]
    return pl.pallas_call(
        _proj_kernel,
        grid=(m // tm, n // tn),
        in_specs=[
            pl.BlockSpec((tm, k), lambda i, j: (i, 0)),
            pl.BlockSpec((k, tn), lambda i, j: (0, j)),
            pl.BlockSpec((1, tn), lambda i, j: (0, j)),
        ],
        out_specs=pl.BlockSpec((tm, tn), lambda i, j: (i, j)),
        out_shape=jax.ShapeDtypeStruct((m, n), out_dtype),
        compiler_params=pltpu.CompilerParams(
            dimension_semantics=("arbitrary", "arbitrary"), vmem_limit_bytes=VMEM_LIMIT),
    )(x2, w, col_scale.reshape(1, n))


def _out_proj_kernel(y_ref, w_ref, res_ref, g_ref, *o_refs, last_layer):
    h = res_ref[...] + jnp.dot(y_ref[...], w_ref[...], preferred_element_type=F32)
    ms = jnp.mean(h * h, axis=-1, keepdims=True)
    hn = h * lax.rsqrt(ms + EPS) * g_ref[...]
    if last_layer:
        o_refs[0][...] = hn
    else:
        o_refs[0][...] = h
        o_refs[1][...] = hn.astype(o_refs[1].dtype)


def _out_proj(y2, w, res2, g, last_layer, tm=512):
    m, kd = y2.shape
    n = w.shape[1]
    row_blk = pl.BlockSpec((tm, n), lambda i: (i, 0))
    out_shape = [jax.ShapeDtypeStruct((m, n), F32)]
    if not last_layer:
        out_shape.append(jax.ShapeDtypeStruct((m, n), BF16))
    return pl.pallas_call(
        functools.partial(_out_proj_kernel, last_layer=last_layer),
        grid=(m // tm,),
        in_specs=[
            pl.BlockSpec((tm, kd), lambda i: (i, 0)),
            pl.BlockSpec((kd, n), lambda i: (0, 0), pipeline_mode=pl.Buffered(1)),
            row_blk,
            pl.BlockSpec((1, n), lambda i: (0, 0)),
        ],
        out_specs=[row_blk] * len(out_shape),
        out_shape=out_shape,
        compiler_params=pltpu.CompilerParams(
            dimension_semantics=("arbitrary",), vmem_limit_bytes=VMEM_LIMIT),
    )(y2, w, res2, g.reshape(1, n))


def _retention_kernel(q_ref, k_ref, v_ref, gate_ref, di_ref, qd_ref, kd_ref, cd_ref, y_ref,
                      state_ref, *, chunk, n_sub):
    @pl.when(pl.program_id(2) == 0)
    def _():
        state_ref[...] = jnp.zeros_like(state_ref)

    di = di_ref[0]
    qd = qd_ref[0]
    kd = kd_ref[0]
    cd = cd_ref[0]
    for i in range(n_sub):
        rows = pl.ds(i * chunk, chunk)
        q = q_ref[0, rows, :]
        k = k_ref[0, rows, :]
        v = v_ref[0, rows, :]
        state = state_ref[...]
        s = lax.dot_general(q, k, _NT, preferred_element_type=F32) * di
        o = jnp.dot(s.astype(BF16), v, preferred_element_type=F32)
        o = o + jnp.dot(q, state.astype(BF16), preferred_element_type=F32) * qd
        k_dec = (k.astype(F32) * kd).astype(BF16)
        state_ref[...] = state * cd + lax.dot_general(k_dec, v, _TN, preferred_element_type=F32)
        mu = jnp.mean(o, axis=-1, keepdims=True)
        oc = o - mu
        var = jnp.mean(oc * oc, axis=-1, keepdims=True)
        on = oc * lax.rsqrt(var + EPS)
        gt = gate_ref[0, rows, :].astype(F32)
        y_ref[0, rows, :] = (gt * _sigmoid(gt) * on).astype(y_ref.dtype)


def _retention(proj, tokens_per_step=1024):
    b, s, _ = proj.shape
    h, c = RET_HEADS, RET_CHUNK
    t = min(tokens_per_step, s)
    log_g = jnp.log1p(-jnp.exp2(-5.0 - jnp.arange(h, dtype=F32)))
    idx = jnp.arange(c, dtype=F32)
    diff = idx[:, None] - idx[None, :]
    decay_intra = jnp.where(diff >= 0, jnp.exp(diff[None] * log_g[:, None, None]), 0.0)
    q_decay = jnp.exp((idx[None, :] + 1.0) * log_g[:, None])[:, :, None]
    k_decay = jnp.exp((c - 1.0 - idx[None, :]) * log_g[:, None])[:, :, None]
    chunk_decay = jnp.exp(c * log_g)[:, None, None]
    nk = RET_HEADS * RET_DK // RET_DK
    nv = 2 * RET_HEADS * RET_DK // RET_DV
    return pl.pallas_call(
        functools.partial(_retention_kernel, chunk=c, n_sub=t // c),
        grid=(b, h, s // t),
        in_specs=[
            pl.BlockSpec((1, t, RET_DK), lambda bi, hi, ti: (bi, ti, hi)),
            pl.BlockSpec((1, t, RET_DK), lambda bi, hi, ti: (bi, ti, nk + hi)),
            pl.BlockSpec((1, t, RET_DV), lambda bi, hi, ti: (bi, ti, nv + hi)),
            pl.BlockSpec((1, t, RET_DV), lambda bi, hi, ti: (bi, ti, nv + h + hi)),
            pl.BlockSpec((1, c, c), lambda bi, hi, ti: (hi, 0, 0)),
            pl.BlockSpec((1, c, 1), lambda bi, hi, ti: (hi, 0, 0)),
            pl.BlockSpec((1, c, 1), lambda bi, hi, ti: (hi, 0, 0)),
            pl.BlockSpec((1, 1, 1), lambda bi, hi, ti: (hi, 0, 0)),
        ],
        out_specs=pl.BlockSpec((1, t, RET_DV), lambda bi, hi, ti: (bi, ti, hi)),
        out_shape=jax.ShapeDtypeStruct((b, s, h * RET_DV), BF16),
        scratch_shapes=[pltpu.VMEM((RET_DK, RET_DV), F32)],
        compiler_params=pltpu.CompilerParams(
            dimension_semantics=("arbitrary", "arbitrary", "arbitrary"),
            vmem_limit_bytes=VMEM_LIMIT),
    )(proj, proj, proj, proj, decay_intra, q_decay, k_decay, chunk_decay)


def _compress_kernel(x_ref, xn_ref, pos_ref, w1_ref, w2_ref, o_ref):
    half = CMP_LEN // 2
    width = NSA_GROUPS * NSA_DH
    for g in range(NSA_GROUPS):
        acc = jnp.zeros((x_ref.shape[1], NSA_DH), F32)
        for l in range(half):
            cols = slice(l * width + g * NSA_DH, l * width + (g + 1) * NSA_DH)
            xa = (x_ref[0, :, cols].astype(F32) + pos_ref[l:l + 1, :]).astype(BF16)
            acc += jnp.dot(xa, w1_ref[l * NSA_DH:(l + 1) * NSA_DH, :], preferred_element_type=F32)
            lb = half + l
            xb = (xn_ref[0, :, cols].astype(F32) + pos_ref[lb:lb + 1, :]).astype(BF16)
            acc += jnp.dot(xb, w1_ref[lb * NSA_DH:(lb + 1) * NSA_DH, :], preferred_element_type=F32)
        hid = (acc * _sigmoid(acc)).astype(BF16)
        out = jnp.dot(hid, w2_ref[...], preferred_element_type=F32)
        o_ref[0, :, g * NSA_DH:(g + 1) * NSA_DH] = out.astype(o_ref.dtype)


def _compress(raw, pos, w1, w2):
    b, s, width = raw.shape
    rows = s // CMP_STRIDE
    x = raw.reshape(b, rows, CMP_STRIDE * width)
    xn = jnp.concatenate([x[:, 1:], jnp.zeros_like(x[:, :1])], axis=1)
    tr = min(256, rows)
    blk = pl.BlockSpec((1, tr, CMP_STRIDE * width), lambda bi, ri: (bi, ri, 0))
    return pl.pallas_call(
        _compress_kernel,
        grid=(b, rows // tr),
        in_specs=[
            blk, blk,
            pl.BlockSpec((CMP_LEN, NSA_DH), lambda bi, ri: (0, 0)),
            pl.BlockSpec((CMP_LEN * NSA_DH, NSA_DH), lambda bi, ri: (0, 0)),
            pl.BlockSpec((NSA_DH, NSA_DH), lambda bi, ri: (0, 0)),
        ],
        out_specs=pl.BlockSpec((1, tr, width), lambda bi, ri: (bi, ri, 0)),
        out_shape=jax.ShapeDtypeStruct((b, rows, width), BF16),
        compiler_params=pltpu.CompilerParams(
            dimension_semantics=("arbitrary", "arbitrary"), vmem_limit_bytes=VMEM_LIMIT),
    )(x, xn, pos, w1.astype(BF16), w2.astype(BF16))


def _stack_heads(qblk):
    return jnp.concatenate(
        [qblk[:, r * NSA_DH:(r + 1) * NSA_DH] for r in range(NSA_REP)], axis=0)


def _head_cols(r):
    return slice(r * Q_BLOCK, (r + 1) * Q_BLOCK)


def _alibi_table(slopes, n_keys, key_stride):
    j = jnp.arange(n_keys, dtype=F32)[:, None] * key_stride
    i = jnp.arange(Q_BLOCK, dtype=F32)[None, :]
    table = slopes.reshape(NSA_GROUPS, 1, NSA_REP, 1) * (j - i)[None, :, None, :]
    return table.reshape(NSA_GROUPS, n_keys, NSA_REP * Q_BLOCK)


def _chunk_shift(slopes, delta):
    row = jnp.full((1, Q_BLOCK), delta, jnp.int32).astype(F32)
    return jnp.concatenate([s * row for s in slopes], axis=1)


def _cmp_select_kernel(slopes_ref, q_ref, kc_ref, vct_ref, ovt_ref, spread_ref, bias_ref, ocmp_ref,
                       sel_ref, cnt_ref, s_scrs, *, ck, nq):
    g = pl.program_id(1)
    qi_first = pl.program_id(2) * CMP_Q_PER_STEP
    qb = Q_BLOCK
    n_ck, n_slc = ovt_ref.shape[0], ovt_ref.shape[1]
    slopes = [slopes_ref[g * NSA_REP + r] for r in range(NSA_REP)]
    rel = _iota((ck, qb), 1) - CMP_STRIDE * _iota((ck, qb), 0)

    def attend_and_select(chunks, rows, sub):
        q0 = (qi_first + sub) * qb
        q = _stack_heads(q_ref[0, sub * qb:(sub + 1) * qb, :])
        chunk_delta = lambda off: CMP_STRIDE * off + (CMP_LEN - 1) - q0
        s_scr = s_scrs.at[sub]
        ocmp_out, sel_out, cnt_out = ocmp_ref.at[0, 0, sub], sel_ref.at[0, 0, sub], cnt_ref.at[0, 0, sub]
        tops = []
        for c in range(chunks):
            off = c * ck
            st = lax.dot_general(kc_ref[0, off:off + ck, :], q, _NT, preferred_element_type=F32)
            valid = rel >= chunk_delta(off)
            top_c = []
            for r in range(NSA_REP):
                s = jnp.where(valid, st[:, _head_cols(r)] + bias_ref[0, :, _head_cols(r)], MASKED)
                s_scr[off:off + ck, _head_cols(r)] = s
                top_c.append(jnp.max(s, axis=0, keepdims=True))
            tops.append(jnp.concatenate(top_c, axis=1) + _chunk_shift(slopes, chunk_delta(off)))
        m_all = functools.reduce(jnp.maximum, tops, jnp.full((1, NSA_REP * qb), M_FLOOR, F32))

        l = jnp.zeros((1, NSA_REP * qb), F32)
        o = jnp.zeros((NSA_DH, NSA_REP * qb), F32)
        imp = jnp.zeros((rows, NSA_REP * qb), F32)
        for c in range(chunks):
            off = c * ck
            shifted_m = m_all - _chunk_shift(slopes, chunk_delta(off))
            p = jnp.exp2(s_scr[off:off + ck, :] - shifted_m)
            pb = p.astype(BF16)
            l = l + jnp.sum(p, axis=0, keepdims=True)
            o = o + jnp.dot(vct_ref[0, 0, c], pb, preferred_element_type=F32)
            imp = imp + jnp.dot(ovt_ref[c, :rows, :], pb, preferred_element_type=F32)
        inv = jnp.where(l > 0.0, 1.0 / l, 0.0)
        ocmp_out[...] = o * inv
        imp_w = imp * inv
        imp_t = imp_w[:, _head_cols(0)]
        for r in range(1, NSA_REP):
            imp_t = imp_t + imp_w[:, _head_cols(r)]

        jrow = _iota((rows, qb), 0)
        cur = lax.shift_right_logical(q0 + _iota((rows, qb), 1), int(np.log2(SLC_BLOCK)))
        forced = (jrow == 0) | (jrow == cur) | (jrow == cur - 1)
        score = jnp.where(forced, TAKEN, jnp.where(jrow <= cur, imp_t, -1.0))
        for _ in range(min(TOP_N, rows) - N_FORCED):
            best = jnp.max(score, axis=0, keepdims=True)
            first = jnp.min(jnp.where(score == best, jrow, rows), axis=0, keepdims=True)
            score = jnp.where(jrow == first, TAKEN, score)
        chosen = jnp.where(score == TAKEN, 1.0, 0.0).astype(BF16)
        sel_out[...] = jnp.dot(spread_ref[:, :rows], chosen, preferred_element_type=F32)
        cnt_out[:, :rows] = lax.dot_general(jnp.ones((8, qb), BF16), chosen, _NT,
                                            preferred_element_type=F32)
        if rows < n_slc:
            cnt_out[:, rows:] = jnp.zeros((8, n_slc - rows), F32)

    if nq % n_ck == 0 and n_slc % n_ck == 0:
        for k in range(1, n_ck + 1):
            in_share = (qi_first >= (k - 1) * (nq // n_ck)) & (qi_first < k * (nq // n_ck))

            @pl.when(in_share)
            def _(k=k):
                for sub in range(CMP_Q_PER_STEP):
                    attend_and_select(k, k * (n_slc // n_ck), sub)
    else:
        for sub in range(CMP_Q_PER_STEP):
            attend_and_select(n_ck, n_slc, sub)


def _cmp_select(pa, k_cmp, v_cmp, slopes):
    b, s, _ = pa.shape
    nq = s // Q_BLOCK
    n_cmp = k_cmp.shape[1]
    n_slc = s // SLC_BLOCK
    ck = min(256, n_cmp)
    n_ck = n_cmp // ck
    gw = NSA_REP * NSA_DH
    cstart = np.arange(n_cmp)[None, :] * CMP_STRIDE
    sstart = np.arange(n_slc)[:, None] * SLC_BLOCK
    real = np.arange(n_cmp)[None, :] < (s - CMP_LEN) // CMP_STRIDE + 1
    ovt = ((cstart < sstart + SLC_BLOCK) & (cstart + CMP_LEN > sstart) & real).astype(np.float32)
    ovt = jnp.asarray(ovt.reshape(n_slc, n_ck, ck).transpose(1, 0, 2), BF16)
    per_chunk = SEL_CHUNK // SLC_BLOCK
    n_sel_chunks = s // SEL_CHUNK
    spread = np.zeros((8 * n_sel_chunks, n_slc), np.float32)
    for j in range(n_slc):
        spread[8 * (j // per_chunk) + j % per_chunk, j] = 1.0
    spread = jnp.asarray(spread, BF16)
    vct = v_cmp.reshape(b, n_ck, ck, NSA_GROUPS, NSA_DH).transpose(0, 3, 1, 4, 2)
    bias = _alibi_table(slopes, ck, CMP_STRIDE)
    per_step = CMP_Q_PER_STEP
    assert nq % per_step == 0 and (nq // n_ck) % per_step == 0
    return pl.pallas_call(
        functools.partial(_cmp_select_kernel, ck=ck, nq=nq),
        grid=(b, NSA_GROUPS, nq // per_step),
        in_specs=[
            pl.BlockSpec(memory_space=pltpu.SMEM),
            pl.BlockSpec((1, per_step * Q_BLOCK, gw), lambda bi, gi, qi: (bi, qi, 2 * gi)),
            pl.BlockSpec((1, n_cmp, NSA_DH), lambda bi, gi, qi: (bi, 0, gi)),
            pl.BlockSpec((1, 1, n_ck, NSA_DH, ck), lambda bi, gi, qi: (bi, gi, 0, 0, 0)),
            pl.BlockSpec((n_ck, n_slc, ck), lambda bi, gi, qi: (0, 0, 0)),
            pl.BlockSpec((8 * n_sel_chunks, n_slc), lambda bi, gi, qi: (0, 0)),
            pl.BlockSpec((1, ck, gw), lambda bi, gi, qi: (gi, 0, 0)),
        ],
        out_specs=[
            pl.BlockSpec((1, 1, per_step, NSA_DH, gw), lambda bi, gi, qi: (bi, gi, qi, 0, 0)),
            pl.BlockSpec((1, 1, per_step, 8 * n_sel_chunks, Q_BLOCK),
                         lambda bi, gi, qi: (bi, gi, qi, 0, 0)),
            pl.BlockSpec((1, 1, per_step, 8, n_slc), lambda bi, gi, qi: (bi, gi, qi, 0, 0)),
        ],
        out_shape=[
            jax.ShapeDtypeStruct((b, NSA_GROUPS, nq, NSA_DH, gw), F32),
            jax.ShapeDtypeStruct((b, NSA_GROUPS, nq, 8 * n_sel_chunks, Q_BLOCK), F32),
            jax.ShapeDtypeStruct((b, NSA_GROUPS, nq, 8, n_slc), F32),
        ],
        scratch_shapes=[pltpu.VMEM((per_step, n_cmp, gw), F32)],
        compiler_params=pltpu.CompilerParams(
            dimension_semantics=("arbitrary", "arbitrary", "arbitrary"),
            vmem_limit_bytes=VMEM_LIMIT),
    )(slopes, pa, k_cmp, vct, ovt, spread, bias)


def _sel_win_kernel(ids_ref, cnt_ref, slopes_ref, qz_ref, kv_ref, sel_ref, ocmp_ref, gate_ref,
                    bias_ref, y_ref, *scratch, n_chunks, nq):
    for sub in range(SEL_Q_PER_STEP):
        _sel_win_block(sub, ids_ref, cnt_ref, slopes_ref, qz_ref, kv_ref, sel_ref, ocmp_ref,
                       gate_ref, bias_ref, y_ref, *(ref.at[sub] for ref in scratch),
                       n_chunks=n_chunks, nq=nq)


def _sel_win_block(sub, ids_ref, cnt_ref, slopes_ref, qz_ref, kv_ref, sel_ref, ocmp_ref, gate_ref,
                   bias_ref, y_ref, m_ref, l_ref, acc_ref, s_scr, w_scr, win_ref, *, n_chunks, nq):
    b = pl.program_id(0)
    g = pl.program_id(1)
    qi = pl.program_id(2) * SEL_Q_PER_STEP + sub
    qb = Q_BLOCK
    q0 = qi * qb
    tok = slice(sub * qb, (sub + 1) * qb)
    per_piece = SEL_PIECE // SLC_BLOCK
    pieces_per_chunk = SEL_CHUNK // SEL_PIECE

    gw = NSA_REP * NSA_DH
    q = _stack_heads(qz_ref[0, tok, :gw])
    slopes = [slopes_ref[g * NSA_REP + r] for r in range(NSA_REP)]
    ks_cols, vs_cols, kw_cols, vw_cols = (slice(i * NSA_DH, (i + 1) * NSA_DH) for i in range(4))

    m_ref[...] = jnp.full_like(m_ref, M_FLOOR)
    l_ref[...] = jnp.zeros_like(l_ref)
    acc_ref[...] = jnp.zeros_like(acc_ref)
    step = (b * NSA_GROUPS + g) * nq + qi
    n_active = cnt_ref[3 * step]
    n_big = cnt_ref[3 * step + 1]
    n_small = cnt_ref[3 * step + 2]
    id_base = step * n_chunks

    def stage_scores(row0, st, mask, scr=s_scr):
        n = st.shape[0]
        tops = []
        for r in range(NSA_REP):
            s = jnp.where(mask, st[:, _head_cols(r)] + bias_ref[0, :n, _head_cols(r)], MASKED)
            scr[row0:row0 + n, _head_cols(r)] = s
            tops.append(jnp.max(s, axis=0, keepdims=True))
        return jnp.concatenate(tops, axis=1)

    def piece_rows(cols, c):
        return kv_ref[0, pl.ds(pl.multiple_of(c * SEL_PIECE, SEL_PIECE), SEL_PIECE), cols]

    def piece_mask(c, threshold):
        group = lax.shift_right_logical(c, int(np.log2(pieces_per_chunk)))
        rows8 = sel_ref[0, 0, sub, pl.ds(pl.multiple_of(group * 8, 8), 8), :]
        within = c & (pieces_per_chunk - 1)
        flags = []
        for i in range(per_piece):
            row = rows8[i:i + 1]
            for w in range(1, pieces_per_chunk):
                row = jnp.where(within == w, rows8[w * per_piece + i:w * per_piece + i + 1], row)
            flags.append(jnp.broadcast_to(row, (SLC_BLOCK, qb)))
        chosen = jnp.concatenate(flags, axis=0)
        rel = _iota((SEL_PIECE, qb), 1) - _iota((SEL_PIECE, qb), 0)
        return (chosen > threshold) & (rel >= c * SEL_PIECE - q0)

    def stage_pair(slot, ca, cb, thr_a, thr_b):
        keys = jnp.concatenate([piece_rows(ks_cols, ca), piece_rows(ks_cols, cb)], axis=0)
        st = lax.dot_general(keys, q, _NT, preferred_element_type=F32)
        out = []
        for h, (c, thr) in enumerate(((ca, thr_a), (cb, thr_b))):
            rows = slice(h * SEL_PIECE, (h + 1) * SEL_PIECE)
            top = stage_scores((2 * slot + h) * SEL_PIECE, st[rows], piece_mask(c, thr))
            shift = _chunk_shift(slopes, c * SEL_PIECE - q0)
            out.append((top + shift, shift))
        return out

    def update(pieces, thresholds):
        pairs = [(pieces[2 * k], pieces[2 * k + 1]) for k in range(len(pieces) // 2)]
        staged = [stage_pair(k, ca, cb, thresholds[2 * k], thresholds[2 * k + 1])
                  for k, (ca, cb) in enumerate(pairs)]
        m_old = m_ref[...]
        m_new = functools.reduce(jnp.maximum, [top for pair in staged for top, _ in pair], m_old)
        alpha = jnp.exp2(m_old - m_new)
        l = alpha * l_ref[...]
        acc = alpha * acc_ref[...]
        for k, ((ca, cb), pair) in enumerate(zip(pairs, staged)):
            halves = []
            for h, (_, shift) in enumerate(pair):
                row0 = (2 * k + h) * SEL_PIECE
                p = jnp.exp2(s_scr[row0:row0 + SEL_PIECE, :] - (m_new - shift))
                l = l + jnp.sum(p, axis=0, keepdims=True)
                halves.append(p.astype(BF16))
            values = jnp.concatenate([piece_rows(vs_cols, ca), piece_rows(vs_cols, cb)], axis=0)
            acc = acc + lax.dot_general(values, jnp.concatenate(halves, axis=0), _TN,
                                        preferred_element_type=F32)
        m_ref[...] = m_new
        l_ref[...] = l
        acc_ref[...] = acc

    def run_step(first, n):
        pieces = [ids_ref[id_base + jnp.minimum(first + k, n_chunks - 1)] for k in range(n)]
        update(pieces, [jnp.where(first + k < n_active, 0.5, 2.0) for k in range(n)])

    def big_step(i, carry):
        run_step(BIG_STEP * (i + 1), BIG_STEP)
        return carry

    def small_step(i, carry):
        run_step(BIG_STEP * (n_big + 1) + SMALL_STEP * i, SMALL_STEP)
        return carry

    wk = WINDOW + qb
    start = pl.multiple_of(jnp.maximum(q0 - WINDOW, 0), qb)
    st = lax.dot_general(kv_ref[0, pl.ds(start, wk), kw_cols], q, _NT, preferred_element_type=F32)
    rel = _iota((wk, qb), 1) - _iota((wk, qb), 0)
    m_win = stage_scores(0, st, (rel >= start - q0) & (rel < WINDOW + start - q0), w_scr)
    p = jnp.exp2(w_scr[...] - m_win)
    l_win = jnp.sum(p, axis=0, keepdims=True)
    win_ref[...] = lax.dot_general(kv_ref[0, pl.ds(start, wk), vw_cols], p.astype(BF16), _TN,
                                   preferred_element_type=F32) * (1.0 / l_win)

    run_step(0, BIG_STEP)
    lax.fori_loop(0, n_big, big_step, 0)
    lax.fori_loop(0, n_small, small_step, 0)

    gates = _sigmoid(gate_ref[0, 0, :, tok])
    gate_row = lambda br: jnp.concatenate(
        [gates[3 * r + br:3 * r + br + 1] for r in range(NSA_REP)], axis=1)
    l_sel = l_ref[...]
    comb = (gate_row(0) * ocmp_ref[0, 0, sub]
            + gate_row(1) * (acc_ref[...] * jnp.where(l_sel > 0.0, 1.0 / l_sel, 0.0))
            + gate_row(2) * win_ref[...])
    for r in range(NSA_REP):
        cols = slice(r * NSA_DH, (r + 1) * NSA_DH)
        z = qz_ref[0, tok, gw + r * NSA_DH:gw + (r + 1) * NSA_DH].astype(F32)
        y_ref[0, tok, cols] = (z * _sigmoid(z) * comb[:, _head_cols(r)].T).astype(y_ref.dtype)


def _sel_win(ids, cnt, slopes, pa, sel, ocmp, gates_t):
    b, s, _ = pa.shape
    nq = s // Q_BLOCK
    gw = NSA_REP * NSA_DH
    kv_col = 2 * NSA_GROUPS
    per_step = SEL_Q_PER_STEP
    assert nq % per_step == 0
    per_q = lambda arr: pl.BlockSpec((1, 1, per_step) + arr.shape[3:],
                                     lambda bi, gi, qi, *_: (bi, gi, qi, 0, 0))
    grid_spec = pltpu.PrefetchScalarGridSpec(
        num_scalar_prefetch=3,
        grid=(b, NSA_GROUPS, nq // per_step),
        in_specs=[
            pl.BlockSpec((1, per_step * Q_BLOCK, 2 * gw), lambda bi, gi, qi, *_: (bi, qi, gi)),
            pl.BlockSpec((1, s, gw), lambda bi, gi, qi, *_: (bi, 0, kv_col + gi)),
            per_q(sel), per_q(ocmp),
            pl.BlockSpec((1, 1, NSA_REP * 3, per_step * Q_BLOCK),
                         lambda bi, gi, qi, *_: (bi, gi, 0, qi)),
            pl.BlockSpec((1, WINDOW + Q_BLOCK, gw), lambda bi, gi, qi, *_: (gi, 0, 0)),
        ],
        out_specs=pl.BlockSpec((1, per_step * Q_BLOCK, gw), lambda bi, gi, qi, *_: (bi, qi, gi)),
        scratch_shapes=[
            pltpu.VMEM((per_step, 1, gw), F32),
            pltpu.VMEM((per_step, 1, gw), F32),
            pltpu.VMEM((per_step, NSA_DH, gw), F32),
            pltpu.VMEM((per_step, BIG_STEP * SEL_PIECE, gw), F32),
            pltpu.VMEM((per_step, WINDOW + Q_BLOCK, gw), F32),
            pltpu.VMEM((per_step, NSA_DH, gw), F32),
        ],
    )
    return pl.pallas_call(
        functools.partial(_sel_win_kernel, n_chunks=s // SEL_PIECE, nq=nq),
        grid_spec=grid_spec,
        out_shape=jax.ShapeDtypeStruct((b, s, NSA_GROUPS * gw), BF16),
        compiler_params=pltpu.CompilerParams(
            dimension_semantics=("arbitrary", "arbitrary", "arbitrary"),
            vmem_limit_bytes=VMEM_LIMIT),
    )(ids, cnt, slopes, pa, pa, sel, ocmp, gates_t, _alibi_table(slopes, WINDOW + Q_BLOCK, 1))


def _active_chunk_lists(counts):
    b, g, nq, _, n_slc = counts.shape
    per_chunk = SEL_PIECE // SLC_BLOCK
    n_chunks = n_slc // per_chunk
    chunk = jnp.arange(n_chunks, dtype=jnp.int32)
    causal = chunk[None, :] <= (jnp.arange(nq, dtype=jnp.int32)[:, None] * Q_BLOCK) // SEL_PIECE
    active = counts[:, :, :, 0].reshape(b, g, nq, n_chunks, per_chunk).max(axis=-1) > 0.5
    active = active & causal
    ids = jnp.sort(jnp.where(active, chunk, chunk + n_chunks), axis=-1) % n_chunks
    n = active.sum(axis=-1, dtype=jnp.int32)
    left = jnp.maximum(n - BIG_STEP, 0)
    full, rest = left // BIG_STEP, left % BIG_STEP
    one_more = rest > 2 * SMALL_STEP
    n_big = full + one_more
    n_small = jnp.where(one_more, 0, (rest + SMALL_STEP - 1) // SMALL_STEP)
    return ids.reshape(-1), jnp.stack([n, n_big, n_small], axis=-1).astype(jnp.int32).reshape(-1)


def kernel(x, norm_g, ret_w_in, ret_w_out, nsa_w_in, nsa_cmp_pos_k, nsa_cmp_w1_k, nsa_cmp_w2_k,
           nsa_cmp_pos_v, nsa_cmp_w1_v, nsa_cmp_w2_v, nsa_w_out, final_g):
    b, s, d = x.shape
    m = b * s
    x2 = x.reshape(m, d)

    qk_w = 2 * RET_HEADS * RET_DK
    v_w = RET_HEADS * RET_DV
    w_in = ret_w_in[0].astype(BF16)
    scale = jnp.concatenate([jnp.ones((qk_w // 2,), F32), jnp.full((qk_w // 2,), RET_DK ** -0.5, F32),
                             jnp.ones((2 * v_w,), F32)])
    proj = _norm_proj(x2, norm_g[0], w_in, scale, BF16, tm=1024, tn=2048)
    y = _retention(proj.reshape(b, s, qk_w + 2 * v_w))
    h1, hn1 = _out_proj(y.reshape(m, v_w), ret_w_out[0].astype(BF16), x2, norm_g[1],
                        last_layer=False)

    width = NSA_HEADS * NSA_DH
    kvw = NSA_GROUPS * NSA_DH
    n_gate = NSA_HEADS * 3
    qkv_w = width + 6 * kvw
    w_n = nsa_w_in[0]
    gw = NSA_REP * NSA_DH
    kv_start = lambda slot: width + slot * kvw
    z_start = qkv_w + n_gate
    runs = []
    for gi in range(NSA_GROUPS):
        runs += [(gi * gw, gw), (z_start + gi * gw, gw)]
    for gi in range(NSA_GROUPS):
        runs += [(kv_start(slot) + gi * NSA_DH, NSA_DH) for slot in (2, 3, 4, 5)]
    runs += [(kv_start(0), kvw), (kv_start(1), kvw)]
    w_a = jnp.concatenate([w_n[:, c0:c0 + cw] for c0, cw in runs], axis=1).astype(BF16)
    scale_a = jnp.concatenate([jnp.full((cw,), NSA_DH ** -0.5 * LOG2E if c0 < width else 1.0, F32)
                               for c0, cw in runs])
    w_g = jnp.concatenate([w_n[:, qkv_w:qkv_w + n_gate], jnp.zeros((d, 128 - n_gate), F32)],
                          axis=1).astype(BF16)
    pa = _proj(hn1, w_a, scale_a, BF16, tm=min(2048, m), tn=1024)
    pa = pa.reshape(b, s, qkv_w + width)
    pg = _proj(hn1, w_g, jnp.ones((128,), F32), F32, tm=min(2048, m), tn=128)

    raw_k = slice(qkv_w + width - 2 * kvw, qkv_w + width - kvw)
    raw_v = slice(qkv_w + width - kvw, qkv_w + width)
    k_cmp = _compress(pa[:, :, raw_k], nsa_cmp_pos_k[0], nsa_cmp_w1_k[0], nsa_cmp_w2_k[0])
    v_cmp = _compress(pa[:, :, raw_v], nsa_cmp_pos_v[0], nsa_cmp_w1_v[0], nsa_cmp_w2_v[0])
    slopes = jnp.exp2(-8.0 * jnp.arange(1, NSA_HEADS + 1, dtype=F32) / NSA_HEADS) * LOG2E
    ocmp, sel, counts = _cmp_select(pa, k_cmp, v_cmp, slopes)
    ids, cnt = _active_chunk_lists(counts)
    gates_t = pg[:, :n_gate].reshape(b, s, NSA_GROUPS, NSA_REP * 3)
    gates_t = gates_t.transpose(0, 2, 3, 1)
    y = _sel_win(ids, cnt, slopes, pa, sel, ocmp, gates_t)
    out, = _out_proj(y.reshape(m, width), nsa_w_out[0].astype(BF16), h1, final_g, last_layer=True)
    return out.reshape(b, s, d)
```

```python
import functools

import jax
import jax.numpy as jnp
import numpy as np
from jax import lax
from jax.experimental import pallas as pl
from jax.experimental.pallas import tpu as pltpu

F32 = jnp.float32
BF16 = jnp.bfloat16

EPS = 1e-6
LOG2E = 1.4426950408889634
MASKED = -2e30
M_FLOOR = -1e30
TAKEN = -3e38
N_FORCED = 3

RET_HEADS = 8
RET_DK = 256
RET_DV = 512
RET_CHUNK = 256

NSA_HEADS = 16
NSA_GROUPS = 4
NSA_REP = 4
NSA_DH = 128
CMP_LEN = 32
CMP_STRIDE = 16
SLC_BLOCK = 64
TOP_N = 16
WINDOW = 512
Q_BLOCK = 128
CMP_Q_PER_STEP = 4
SEL_Q_PER_STEP = 2
SEL_CHUNK = 256
SEL_PIECE = 128
BIG_STEP = 10
SMALL_STEP = 2

VMEM_LIMIT = 56 * 1024 * 1024

_NT = (((1,), (1,)), ((), ()))
_TN = (((0,), (0,)), ((), ()))


def _sigmoid(x):
    return 1.0 / (1.0 + jnp.exp2(-LOG2E * x))


def _iota(shape, dim):
    return lax.broadcasted_iota(jnp.int32, shape, dim)


def _norm_proj_kernel(x_ref, g_ref, w_ref, s_ref, o_ref, xn_ref):
    @pl.when(pl.program_id(1) == 0)
    def _():
        x = x_ref[...]
        ms = jnp.mean(x * x, axis=-1, keepdims=True)
        xn_ref[...] = (x * lax.rsqrt(ms + EPS) * g_ref[...]).astype(xn_ref.dtype)

    acc = jnp.dot(xn_ref[...], w_ref[...], preferred_element_type=F32)
    o_ref[...] = (acc * s_ref[...]).astype(o_ref.dtype)


def _norm_proj(x2, g, w, col_scale, out_dtype, tm, tn):
    m, k = x2.shape
    n = w.shape[1]
    return pl.pallas_call(
        _norm_proj_kernel,
        grid=(m // tm, n // tn),
        in_specs=[
            pl.BlockSpec((tm, k), lambda i, j: (i, 0)),
            pl.BlockSpec((1, k), lambda i, j: (0, 0)),
            pl.BlockSpec((k, tn), lambda i, j: (0, j)),
            pl.BlockSpec((1, tn), lambda i, j: (0, j)),
        ],
        out_specs=pl.BlockSpec((tm, tn), lambda i, j: (i, j)),
        out_shape=jax.ShapeDtypeStruct((m, n), out_dtype),
        scratch_shapes=[pltpu.VMEM((tm, k), BF16)],
        compiler_params=pltpu.CompilerParams(
            dimension_semantics=("arbitrary", "arbitrary"), vmem_limit_bytes=VMEM_LIMIT),
    )(x2, g.reshape(1, k), w, col_scale.reshape(1, n))


def _proj_kernel(x_ref, w_ref, s_ref, o_ref):
    acc = jnp.dot(x_ref[...], w_ref[...], preferred_element_type=F32)
    o_ref[...] = (acc * s_ref[...]).astype(o_ref.dtype)


def _proj(x2, w, col_scale, out_dtype, tm, tn):
    m, k = x2.shape
    n = w.shape[1]
    return pl.pallas_call(
        _proj_kernel,
        grid=(m // tm, n // tn),
        in_specs=[
            pl.BlockSpec((tm, k), lambda i, j: (i, 0)),
            pl.BlockSpec((k, tn), lambda i, j: (0, j)),
            pl.BlockSpec((1, tn), lambda i, j: (0, j)),
        ],
        out_specs=pl.BlockSpec((tm, tn), lambda i, j: (i, j)),
        out_shape=jax.ShapeDtypeStruct((m, n), out_dtype),
        compiler_params=pltpu.CompilerParams(
            dimension_semantics=("arbitrary", "arbitrary"), vmem_limit_bytes=VMEM_LIMIT),
    )(x2, w, col_scale.reshape(1, n))


def _out_proj_kernel(y_ref, w_ref, res_ref, g_ref, *o_refs, last_layer):
    h = res_ref[...] + jnp.dot(y_ref[...], w_ref[...], preferred_element_type=F32)
    ms = jnp.mean(h * h, axis=-1, keepdims=True)
    hn = h * lax.rsqrt(ms + EPS) * g_ref[...]
    if last_layer:
        o_refs[0][...] = hn
    else:
        o_refs[0][...] = h
        o_refs[1][...] = hn.astype(o_refs[1].dtype)


def _out_proj(y2, w, res2, g, last_layer, tm=512):
    m, kd = y2.shape
    n = w.shape[1]
    row_blk = pl.BlockSpec((tm, n), lambda i: (i, 0))
    out_shape = [jax.ShapeDtypeStruct((m, n), F32)]
    if not last_layer:
        out_shape.append(jax.ShapeDtypeStruct((m, n), BF16))
    return pl.pallas_call(
        functools.partial(_out_proj_kernel, last_layer=last_layer),
        grid=(m // tm,),
        in_specs=[
            pl.BlockSpec((tm, kd), lambda i: (i, 0)),
            pl.BlockSpec((kd, n), lambda i: (0, 0), pipeline_mode=pl.Buffered(1)),
            row_blk,
            pl.BlockSpec((1, n), lambda i: (0, 0)),
        ],
        out_specs=[row_blk] * len(out_shape),
        out_shape=out_shape,
        compiler_params=pltpu.CompilerParams(
            dimension_semantics=("arbitrary",), vmem_limit_bytes=VMEM_LIMIT),
    )(y2, w, res2, g.reshape(1, n))


def _retention_kernel(q_ref, k_ref, v_ref, gate_ref, di_ref, qd_ref, kd_ref, cd_ref, y_ref,
                      state_ref, *, chunk, n_sub):
    @pl.when(pl.program_id(2) == 0)
    def _():
        state_ref[...] = jnp.zeros_like(state_ref)

    di = di_ref[0]
    qd = qd_ref[0]
    kd = kd_ref[0]
    cd = cd_ref[0]
    for i in range(n_sub):
        rows = pl.ds(i * chunk, chunk)
        q = q_ref[0, rows, :]
        k = k_ref[0, rows, :]
        v = v_ref[0, rows, :]
        state = state_ref[...]
        s = lax.dot_general(q, k, _NT, preferred_element_type=F32) * di
        o = jnp.dot(s.astype(BF16), v, preferred_element_type=F32)
        o = o + jnp.dot(q, state.astype(BF16), preferred_element_type=F32) * qd
        k_dec = (k.astype(F32) * kd).astype(BF16)
        state_ref[...] = state * cd + lax.dot_general(k_dec, v, _TN, preferred_element_type=F32)
        mu = jnp.mean(o, axis=-1, keepdims=True)
        oc = o - mu
        var = jnp.mean(oc * oc, axis=-1, keepdims=True)
        on = oc * lax.rsqrt(var + EPS)
        gt = gate_ref[0, rows, :].astype(F32)
        y_ref[0, rows, :] = (gt * _sigmoid(gt) * on).astype(y_ref.dtype)


def _retention(proj, tokens_per_step=2048):
    b, s, _ = proj.shape
    h, c = RET_HEADS, RET_CHUNK
    t = min(tokens_per_step, s)
    log_g = jnp.log1p(-jnp.exp2(-5.0 - jnp.arange(h, dtype=F32)))
    idx = jnp.arange(c, dtype=F32)
    diff = idx[:, None] - idx[None, :]
    decay_intra = jnp.where(diff >= 0, jnp.exp(diff[None] * log_g[:, None, None]), 0.0)
    q_decay = jnp.exp((idx[None, :] + 1.0) * log_g[:, None])[:, :, None]
    k_decay = jnp.exp((c - 1.0 - idx[None, :]) * log_g[:, None])[:, :, None]
    chunk_decay = jnp.exp(c * log_g)[:, None, None]
    nk = RET_HEADS * RET_DK // RET_DK
    nv = 2 * RET_HEADS * RET_DK // RET_DV
    return pl.pallas_call(
        functools.partial(_retention_kernel, chunk=c, n_sub=t // c),
        grid=(b, h, s // t),
        in_specs=[
            pl.BlockSpec((1, t, RET_DK), lambda bi, hi, ti: (bi, ti, hi)),
            pl.BlockSpec((1, t, RET_DK), lambda bi, hi, ti: (bi, ti, nk + hi)),
            pl.BlockSpec((1, t, RET_DV), lambda bi, hi, ti: (bi, ti, nv + hi)),
            pl.BlockSpec((1, t, RET_DV), lambda bi, hi, ti: (bi, ti, nv + h + hi)),
            pl.BlockSpec((1, c, c), lambda bi, hi, ti: (hi, 0, 0)),
            pl.BlockSpec((1, c, 1), lambda bi, hi, ti: (hi, 0, 0)),
            pl.BlockSpec((1, c, 1), lambda bi, hi, ti: (hi, 0, 0)),
            pl.BlockSpec((1, 1, 1), lambda bi, hi, ti: (hi, 0, 0)),
        ],
        out_specs=pl.BlockSpec((1, t, RET_DV), lambda bi, hi, ti: (bi, ti, hi)),
        out_shape=jax.ShapeDtypeStruct((b, s, h * RET_DV), BF16),
        scratch_shapes=[pltpu.VMEM((RET_DK, RET_DV), F32)],
        compiler_params=pltpu.CompilerParams(
            dimension_semantics=("arbitrary", "arbitrary", "arbitrary"),
            vmem_limit_bytes=VMEM_LIMIT),
    )(proj, proj, proj, proj, decay_intra, q_decay, k_decay, chunk_decay)


def _compress_kernel(x_ref, xn_ref, pos_ref, w1_ref, w2_ref, o_ref):
    half = CMP_LEN // 2
    width = NSA_GROUPS * NSA_DH
    for g in range(NSA_GROUPS):
        acc = jnp.zeros((x_ref.shape[1], NSA_DH), F32)
        for l in range(half):
            cols = slice(l * width + g * NSA_DH, l * width + (g + 1) * NSA_DH)
            xa = (x_ref[0, :, cols].astype(F32) + pos_ref[l:l + 1, :]).astype(BF16)
            acc += jnp.dot(xa, w1_ref[l * NSA_DH:(l + 1) * NSA_DH, :], preferred_element_type=F32)
            lb = half + l
            xb = (xn_ref[0, :, cols].astype(F32) + pos_ref[lb:lb + 1, :]).astype(BF16)
            acc += jnp.dot(xb, w1_ref[lb * NSA_DH:(lb + 1) * NSA_DH, :], preferred_element_type=F32)
        hid = (acc * _sigmoid(acc)).astype(BF16)
        out = jnp.dot(hid, w2_ref[...], preferred_element_type=F32)
        o_ref[0, :, g * NSA_DH:(g + 1) * NSA_DH] = out.astype(o_ref.dtype)


def _compress(raw, pos, w1, w2):
    b, s, width = raw.shape
    rows = s // CMP_STRIDE
    x = raw.reshape(b, rows, CMP_STRIDE * width)
    xn = jnp.concatenate([x[:, 1:], jnp.zeros_like(x[:, :1])], axis=1)
    tr = min(256, rows)
    blk = pl.BlockSpec((1, tr, CMP_STRIDE * width), lambda bi, ri: (bi, ri, 0))
    return pl.pallas_call(
        _compress_kernel,
        grid=(b, rows // tr),
        in_specs=[
            blk, blk,
            pl.BlockSpec((CMP_LEN, NSA_DH), lambda bi, ri: (0, 0)),
            pl.BlockSpec((CMP_LEN * NSA_DH, NSA_DH), lambda bi, ri: (0, 0)),
            pl.BlockSpec((NSA_DH, NSA_DH), lambda bi, ri: (0, 0)),
        ],
        out_specs=pl.BlockSpec((1, tr, width), lambda bi, ri: (bi, ri, 0)),
        out_shape=jax.ShapeDtypeStruct((b, rows, width), BF16),
        compiler_params=pltpu.CompilerParams(
            dimension_semantics=("arbitrary", "arbitrary"), vmem_limit_bytes=VMEM_LIMIT),
    )(x, xn, pos, w1.astype(BF16), w2.astype(BF16))


def _stack_heads(qblk):
    return jnp.concatenate(
        [qblk[:, r * NSA_DH:(r + 1) * NSA_DH] for r in range(NSA_REP)], axis=0)


def _head_cols(r):
    return slice(r * Q_BLOCK, (r + 1) * Q_BLOCK)


def _alibi_table(slopes, n_keys, key_stride):
    j = jnp.arange(n_keys, dtype=F32)[:, None] * key_stride
    i = jnp.arange(Q_BLOCK, dtype=F32)[None, :]
    table = slopes.reshape(NSA_GROUPS, 1, NSA_REP, 1) * (j - i)[None, :, None, :]
    return table.reshape(NSA_GROUPS, n_keys, NSA_REP * Q_BLOCK)


def _chunk_shift(slopes, delta):
    row = jnp.full((1, Q_BLOCK), delta, jnp.int32).astype(F32)
    return jnp.concatenate([s * row for s in slopes], axis=1)


def _cmp_select_kernel(slopes_ref, q_ref, kc_ref, vct_ref, ovt_ref, spread_ref, bias_ref, ocmp_ref,
                       sel_ref, cnt_ref, s_scrs, *, ck, nq):
    g = pl.program_id(1)
    qi_first = pl.program_id(2) * CMP_Q_PER_STEP
    qb = Q_BLOCK
    n_ck, n_slc = ovt_ref.shape[0], ovt_ref.shape[1]
    slopes = [slopes_ref[g * NSA_REP + r] for r in range(NSA_REP)]
    rel = _iota((ck, qb), 1) - CMP_STRIDE * _iota((ck, qb), 0)

    def attend_and_select(chunks, rows, sub):
        q0 = (qi_first + sub) * qb
        q = _stack_heads(q_ref[0, sub * qb:(sub + 1) * qb, :])
        chunk_delta = lambda off: CMP_STRIDE * off + (CMP_LEN - 1) - q0
        s_scr = s_scrs.at[sub]
        ocmp_out, sel_out, cnt_out = ocmp_ref.at[0, 0, sub], sel_ref.at[0, 0, sub], cnt_ref.at[0, 0, sub]
        tops = []
        for c in range(chunks):
            off = c * ck
            st = lax.dot_general(kc_ref[0, off:off + ck, :], q, _NT, preferred_element_type=F32)
            valid = rel >= chunk_delta(off)
            top_c = []
            for r in range(NSA_REP):
                s = jnp.where(valid, st[:, _head_cols(r)] + bias_ref[0, :, _head_cols(r)], MASKED)
                s_scr[off:off + ck, _head_cols(r)] = s
                top_c.append(jnp.max(s, axis=0, keepdims=True))
            tops.append(jnp.concatenate(top_c, axis=1) + _chunk_shift(slopes, chunk_delta(off)))
        m_all = functools.reduce(jnp.maximum, tops, jnp.full((1, NSA_REP * qb), M_FLOOR, F32))

        l = jnp.zeros((1, NSA_REP * qb), F32)
        o = jnp.zeros((NSA_DH, NSA_REP * qb), F32)
        imp = jnp.zeros((rows, NSA_REP * qb), F32)
        for c in range(chunks):
            off = c * ck
            shifted_m = m_all - _chunk_shift(slopes, chunk_delta(off))
            p = jnp.exp2(s_scr[off:off + ck, :] - shifted_m)
            pb = p.astype(BF16)
            l = l + jnp.sum(p, axis=0, keepdims=True)
            o = o + jnp.dot(vct_ref[0, 0, c], pb, preferred_element_type=F32)
            imp = imp + jnp.dot(ovt_ref[c, :rows, :], pb, preferred_element_type=F32)
        inv = jnp.where(l > 0.0, 1.0 / l, 0.0)
        ocmp_out[...] = o * inv
        imp_w = imp * inv
        imp_t = imp_w[:, _head_cols(0)]
        for r in range(1, NSA_REP):
            imp_t = imp_t + imp_w[:, _head_cols(r)]

        jrow = _iota((rows, qb), 0)
        cur = lax.shift_right_logical(q0 + _iota((rows, qb), 1), int(np.log2(SLC_BLOCK)))
        forced = (jrow == 0) | (jrow == cur) | (jrow == cur - 1)
        score = jnp.where(forced, TAKEN, jnp.where(jrow <= cur, imp_t, -1.0))
        for _ in range(min(TOP_N, rows) - N_FORCED):
            best = jnp.max(score, axis=0, keepdims=True)
            first = jnp.min(jnp.where(score == best, jrow, rows), axis=0, keepdims=True)
            score = jnp.where(jrow == first, TAKEN, score)
        chosen = jnp.where(score == TAKEN, 1.0, 0.0).astype(BF16)
        sel_out[...] = jnp.dot(spread_ref[:, :rows], chosen, preferred_element_type=F32)
        cnt_out[:, :rows] = lax.dot_general(jnp.ones((8, qb), BF16), chosen, _NT,
                                            preferred_element_type=F32)
        if rows < n_slc:
            cnt_out[:, rows:] = jnp.zeros((8, n_slc - rows), F32)

    if nq % n_ck == 0 and n_slc % n_ck == 0:
        for k in range(1, n_ck + 1):
            in_share = (qi_first >= (k - 1) * (nq // n_ck)) & (qi_first < k * (nq // n_ck))

            @pl.when(in_share)
            def _(k=k):
                for sub in range(CMP_Q_PER_STEP):
                    attend_and_select(k, k * (n_slc // n_ck), sub)
    else:
        for sub in range(CMP_Q_PER_STEP):
            attend_and_select(n_ck, n_slc, sub)


def _cmp_select(pa, k_cmp, v_cmp, slopes):
    b, s, _ = pa.shape
    nq = s // Q_BLOCK
    n_cmp = k_cmp.shape[1]
    n_slc = s // SLC_BLOCK
    ck = min(256, n_cmp)
    n_ck = n_cmp // ck
    gw = NSA_REP * NSA_DH
    cstart = np.arange(n_cmp)[None, :] * CMP_STRIDE
    sstart = np.arange(n_slc)[:, None] * SLC_BLOCK
    real = np.arange(n_cmp)[None, :] < (s - CMP_LEN) // CMP_STRIDE + 1
    ovt = ((cstart < sstart + SLC_BLOCK) & (cstart + CMP_LEN > sstart) & real).astype(np.float32)
    ovt = jnp.asarray(ovt.reshape(n_slc, n_ck, ck).transpose(1, 0, 2), BF16)
    per_chunk = SEL_CHUNK // SLC_BLOCK
    n_sel_chunks = s // SEL_CHUNK
    spread = np.zeros((8 * n_sel_chunks, n_slc), np.float32)
    for j in range(n_slc):
        spread[8 * (j // per_chunk) + j % per_chunk, j] = 1.0
    spread = jnp.asarray(spread, BF16)
    vct = v_cmp.reshape(b, n_ck, ck, NSA_GROUPS, NSA_DH).transpose(0, 3, 1, 4, 2)
    bias = _alibi_table(slopes, ck, CMP_STRIDE)
    per_step = CMP_Q_PER_STEP
    assert nq % per_step == 0 and (nq // n_ck) % per_step == 0
    return pl.pallas_call(
        functools.partial(_cmp_select_kernel, ck=ck, nq=nq),
        grid=(b, NSA_GROUPS, nq // per_step),
        in_specs=[
            pl.BlockSpec(memory_space=pltpu.SMEM),
            pl.BlockSpec((1, per_step * Q_BLOCK, gw), lambda bi, gi, qi: (bi, qi, 2 * gi)),
            pl.BlockSpec((1, n_cmp, NSA_DH), lambda bi, gi, qi: (bi, 0, gi)),
            pl.BlockSpec((1, 1, n_ck, NSA_DH, ck), lambda bi, gi, qi: (bi, gi, 0, 0, 0)),
            pl.BlockSpec((n_ck, n_slc, ck), lambda bi, gi, qi: (0, 0, 0)),
            pl.BlockSpec((8 * n_sel_chunks, n_slc), lambda bi, gi, qi: (0, 0)),
            pl.BlockSpec((1, ck, gw), lambda bi, gi, qi: (gi, 0, 0)),
        ],
        out_specs=[
            pl.BlockSpec((1, 1, per_step, NSA_DH, gw), lambda bi, gi, qi: (bi, gi, qi, 0, 0)),
            pl.BlockSpec((1, 1, per_step, 8 * n_sel_chunks, Q_BLOCK),
                         lambda bi, gi, qi: (bi, gi, qi, 0, 0)),
            pl.BlockSpec((1, 1, per_step, 8, n_slc), lambda bi, gi, qi: (bi, gi, qi, 0, 0)),
        ],
        out_shape=[
            jax.ShapeDtypeStruct((b, NSA_GROUPS, nq, NSA_DH, gw), F32),
            jax.ShapeDtypeStruct((b, NSA_GROUPS, nq, 8 * n_sel_chunks, Q_BLOCK), F32),
            jax.ShapeDtypeStruct((b, NSA_GROUPS, nq, 8, n_slc), F32),
        ],
        scratch_shapes=[pltpu.VMEM((per_step, n_cmp, gw), F32)],
        compiler_params=pltpu.CompilerParams(
            dimension_semantics=("arbitrary", "arbitrary", "arbitrary"),
            vmem_limit_bytes=VMEM_LIMIT),
    )(slopes, pa, k_cmp, vct, ovt, spread, bias)


def _sel_win_kernel(ids_ref, cnt_ref, slopes_ref, qz_ref, kv_ref, sel_ref, ocmp_ref, gate_ref,
                    bias_ref, y_ref, *scratch, n_chunks, nq):
    for sub in range(SEL_Q_PER_STEP):
        _sel_win_block(sub, ids_ref, cnt_ref, slopes_ref, qz_ref, kv_ref, sel_ref, ocmp_ref,
                       gate_ref, bias_ref, y_ref, *(ref.at[sub] for ref in scratch),
                       n_chunks=n_chunks, nq=nq)


def _sel_win_block(sub, ids_ref, cnt_ref, slopes_ref, qz_ref, kv_ref, sel_ref, ocmp_ref, gate_ref,
                   bias_ref, y_ref, m_ref, l_ref, acc_ref, s_scr, w_scr, win_ref, *, n_chunks, nq):
    b = pl.program_id(0)
    g = pl.program_id(1)
    qi = pl.program_id(2) * SEL_Q_PER_STEP + sub
    qb = Q_BLOCK
    q0 = qi * qb
    tok = slice(sub * qb, (sub + 1) * qb)
    per_piece = SEL_PIECE // SLC_BLOCK
    pieces_per_chunk = SEL_CHUNK // SEL_PIECE

    gw = NSA_REP * NSA_DH
    q = _stack_heads(qz_ref[0, tok, :gw])
    slopes = [slopes_ref[g * NSA_REP + r] for r in range(NSA_REP)]
    ks_cols, vs_cols, kw_cols, vw_cols = (slice(i * NSA_DH, (i + 1) * NSA_DH) for i in range(4))

    m_ref[...] = jnp.full_like(m_ref, M_FLOOR)
    l_ref[...] = jnp.zeros_like(l_ref)
    acc_ref[...] = jnp.zeros_like(acc_ref)
    step = (b * NSA_GROUPS + g) * nq + qi
    n_active = cnt_ref[3 * step]
    n_big = cnt_ref[3 * step + 1]
    n_small = cnt_ref[3 * step + 2]
    id_base = step * n_chunks

    def stage_scores(row0, st, mask, scr=s_scr):
        n = st.shape[0]
        tops = []
        for r in range(NSA_REP):
            s = jnp.where(mask, st[:, _head_cols(r)] + bias_ref[0, :n, _head_cols(r)], MASKED)
            scr[row0:row0 + n, _head_cols(r)] = s
            tops.append(jnp.max(s, axis=0, keepdims=True))
        return jnp.concatenate(tops, axis=1)

    def piece_rows(cols, c):
        return kv_ref[0, pl.ds(pl.multiple_of(c * SEL_PIECE, SEL_PIECE), SEL_PIECE), cols]

    def piece_mask(c, threshold):
        group = lax.shift_right_logical(c, int(np.log2(pieces_per_chunk)))
        rows8 = sel_ref[0, 0, sub, pl.ds(pl.multiple_of(group * 8, 8), 8), :]
        within = c & (pieces_per_chunk - 1)
        flags = []
        for i in range(per_piece):
            row = rows8[i:i + 1]
            for w in range(1, pieces_per_chunk):
                row = jnp.where(within == w, rows8[w * per_piece + i:w * per_piece + i + 1], row)
            flags.append(jnp.broadcast_to(row, (SLC_BLOCK, qb)))
        chosen = jnp.concatenate(flags, axis=0)
        rel = _iota((SEL_PIECE, qb), 1) - _iota((SEL_PIECE, qb), 0)
        return (chosen > threshold) & (rel >= c * SEL_PIECE - q0)

    def stage_pair(slot, ca, cb, thr_a, thr_b):
        keys = jnp.concatenate([piece_rows(ks_cols, ca), piece_rows(ks_cols, cb)], axis=0)
        st = lax.dot_general(keys, q, _NT, preferred_element_type=F32)
        out = []
        for h, (c, thr) in enumerate(((ca, thr_a), (cb, thr_b))):
            rows = slice(h * SEL_PIECE, (h + 1) * SEL_PIECE)
            top = stage_scores((2 * slot + h) * SEL_PIECE, st[rows], piece_mask(c, thr))
            shift = _chunk_shift(slopes, c * SEL_PIECE - q0)
            out.append((top + shift, shift))
        return out

    def update(pieces, thresholds):
        pairs = [(pieces[2 * k], pieces[2 * k + 1]) for k in range(len(pieces) // 2)]
        staged = [stage_pair(k, ca, cb, thresholds[2 * k], thresholds[2 * k + 1])
                  for k, (ca, cb) in enumerate(pairs)]
        m_old = m_ref[...]
        m_new = functools.reduce(jnp.maximum, [top for pair in staged for top, _ in pair], m_old)
        alpha = jnp.exp2(m_old - m_new)
        l = alpha * l_ref[...]
        acc = alpha * acc_ref[...]
        for k, ((ca, cb), pair) in enumerate(zip(pairs, staged)):
            halves = []
            for h, (_, shift) in enumerate(pair):
                row0 = (2 * k + h) * SEL_PIECE
                p = jnp.exp2(s_scr[row0:row0 + SEL_PIECE, :] - (m_new - shift))
                l = l + jnp.sum(p, axis=0, keepdims=True)
                halves.append(p.astype(BF16))
            values = jnp.concatenate([piece_rows(vs_cols, ca), piece_rows(vs_cols, cb)], axis=0)
            acc = acc + lax.dot_general(values, jnp.concatenate(halves, axis=0), _TN,
                                        preferred_element_type=F32)
        m_ref[...] = m_new
        l_ref[...] = l
        acc_ref[...] = acc

    def run_step(first, n):
        pieces = [ids_ref[id_base + jnp.minimum(first + k, n_chunks - 1)] for k in range(n)]
        update(pieces, [jnp.where(first + k < n_active, 0.5, 2.0) for k in range(n)])

    def big_step(i, carry):
        run_step(BIG_STEP * (i + 1), BIG_STEP)
        return carry

    def small_step(i, carry):
        run_step(BIG_STEP * (n_big + 1) + SMALL_STEP * i, SMALL_STEP)
        return carry

    wk = WINDOW + qb
    start = pl.multiple_of(jnp.maximum(q0 - WINDOW, 0), qb)
    st = lax.dot_general(kv_ref[0, pl.ds(start, wk), kw_cols], q, _NT, preferred_element_type=F32)
    rel = _iota((wk, qb), 1) - _iota((wk, qb), 0)
    m_win = stage_scores(0, st, (rel >= start - q0) & (rel < WINDOW + start - q0), w_scr)
    p = jnp.exp2(w_scr[...] - m_win)
    l_win = jnp.sum(p, axis=0, keepdims=True)
    win_ref[...] = lax.dot_general(kv_ref[0, pl.ds(start, wk), vw_cols], p.astype(BF16), _TN,
                                   preferred_element_type=F32) * (1.0 / l_win)

    run_step(0, BIG_STEP)
    lax.fori_loop(0, n_big, big_step, 0)
    lax.fori_loop(0, n_small, small_step, 0)

    gates = _sigmoid(gate_ref[0, 0, :, tok])
    gate_row = lambda br: jnp.concatenate(
        [gates[3 * r + br:3 * r + br + 1] for r in range(NSA_REP)], axis=1)
    l_sel = l_ref[...]
    comb = (gate_row(0) * ocmp_ref[0, 0, sub]
            + gate_row(1) * (acc_ref[...] * jnp.where(l_sel > 0.0, 1.0 / l_sel, 0.0))
            + gate_row(2) * win_ref[...])
    for r in range(NSA_REP):
        cols = slice(r * NSA_DH, (r + 1) * NSA_DH)
        z = qz_ref[0, tok, gw + r * NSA_DH:gw + (r + 1) * NSA_DH].astype(F32)
        y_ref[0, tok, cols] = (z * _sigmoid(z) * comb[:, _head_cols(r)].T).astype(y_ref.dtype)


def _sel_win(ids, cnt, slopes, pa, sel, ocmp, gates_t):
    b, s, _ = pa.shape
    nq = s // Q_BLOCK
    gw = NSA_REP * NSA_DH
    kv_col = 2 * NSA_GROUPS
    per_step = SEL_Q_PER_STEP
    assert nq % per_step == 0
    per_q = lambda arr: pl.BlockSpec((1, 1, per_step) + arr.shape[3:],
                                     lambda bi, gi, qi, *_: (bi, gi, qi, 0, 0))
    grid_spec = pltpu.PrefetchScalarGridSpec(
        num_scalar_prefetch=3,
        grid=(b, NSA_GROUPS, nq // per_step),
        in_specs=[
            pl.BlockSpec((1, per_step * Q_BLOCK, 2 * gw), lambda bi, gi, qi, *_: (bi, qi, gi)),
            pl.BlockSpec((1, s, gw), lambda bi, gi, qi, *_: (bi, 0, kv_col + gi)),
            per_q(sel), per_q(ocmp),
            pl.BlockSpec((1, 1, NSA_REP * 3, per_step * Q_BLOCK),
                         lambda bi, gi, qi, *_: (bi, gi, 0, qi)),
            pl.BlockSpec((1, WINDOW + Q_BLOCK, gw), lambda bi, gi, qi, *_: (gi, 0, 0)),
        ],
        out_specs=pl.BlockSpec((1, per_step * Q_BLOCK, gw), lambda bi, gi, qi, *_: (bi, qi, gi)),
        scratch_shapes=[
            pltpu.VMEM((per_step, 1, gw), F32),
            pltpu.VMEM((per_step, 1, gw), F32),
            pltpu.VMEM((per_step, NSA_DH, gw), F32),
            pltpu.VMEM((per_step, BIG_STEP * SEL_PIECE, gw), F32),
            pltpu.VMEM((per_step, WINDOW + Q_BLOCK, gw), F32),
            pltpu.VMEM((per_step, NSA_DH, gw), F32),
        ],
    )
    return pl.pallas_call(
        functools.partial(_sel_win_kernel, n_chunks=s // SEL_PIECE, nq=nq),
        grid_spec=grid_spec,
        out_shape=jax.ShapeDtypeStruct((b, s, NSA_GROUPS * gw), BF16),
        compiler_params=pltpu.CompilerParams(
            dimension_semantics=("arbitrary", "arbitrary", "arbitrary"),
            vmem_limit_bytes=VMEM_LIMIT),
    )(ids, cnt, slopes, pa, pa, sel, ocmp, gates_t, _alibi_table(slopes, WINDOW + Q_BLOCK, 1))


def _active_chunk_lists(counts):
    b, g, nq, _, n_slc = counts.shape
    per_chunk = SEL_PIECE // SLC_BLOCK
    n_chunks = n_slc // per_chunk
    chunk = jnp.arange(n_chunks, dtype=jnp.int32)
    causal = chunk[None, :] <= (jnp.arange(nq, dtype=jnp.int32)[:, None] * Q_BLOCK) // SEL_PIECE
    active = counts[:, :, :, 0].reshape(b, g, nq, n_chunks, per_chunk).max(axis=-1) > 0.5
    active = active & causal
    ids = jnp.sort(jnp.where(active, chunk, chunk + n_chunks), axis=-1) % n_chunks
    n = active.sum(axis=-1, dtype=jnp.int32)
    left = jnp.maximum(n - BIG_STEP, 0)
    full, rest = left // BIG_STEP, left % BIG_STEP
    one_more = rest > 2 * SMALL_STEP
    n_big = full + one_more
    n_small = jnp.where(one_more, 0, (rest + SMALL_STEP - 1) // SMALL_STEP)
    return ids.reshape(-1), jnp.stack([n, n_big, n_small], axis=-1).astype(jnp.int32).reshape(-1)


def kernel(x, norm_g, ret_w_in, ret_w_out, nsa_w_in, nsa_cmp_pos_k, nsa_cmp_w1_k, nsa_cmp_w2_k,
           nsa_cmp_pos_v, nsa_cmp_w1_v, nsa_cmp_w2_v, nsa_w_out, final_g):
    b, s, d = x.shape
    m = b * s
    x2 = x.reshape(m, d)

    qk_w = 2 * RET_HEADS * RET_DK
    v_w = RET_HEADS * RET_DV
    w_in = ret_w_in[0].astype(BF16)
    scale = jnp.concatenate([jnp.ones((qk_w // 2,), F32), jnp.full((qk_w // 2,), RET_DK ** -0.5, F32),
                             jnp.ones((2 * v_w,), F32)])
    proj = _norm_proj(x2, norm_g[0], w_in, scale, BF16, tm=1024, tn=2048)
    y = _retention(proj.reshape(b, s, qk_w + 2 * v_w))
    h1, hn1 = _out_proj(y.reshape(m, v_w), ret_w_out[0].astype(BF16), x2, norm_g[1],
                        last_layer=False)

    width = NSA_HEADS * NSA_DH
    kvw = NSA_GROUPS * NSA_DH
    n_gate = NSA_HEADS * 3
    qkv_w = width + 6 * kvw
    w_n = nsa_w_in[0]
    gw = NSA_REP * NSA_DH
    kv_start = lambda slot: width + slot * kvw
    z_start = qkv_w + n_gate
    runs = []
    for gi in range(NSA_GROUPS):
        runs += [(gi * gw, gw), (z_start + gi * gw, gw)]
    for gi in range(NSA_GROUPS):
        runs += [(kv_start(slot) + gi * NSA_DH, NSA_DH) for slot in (2, 3, 4, 5)]
    runs += [(kv_start(0), kvw), (kv_start(1), kvw)]
    w_a = jnp.concatenate([w_n[:, c0:c0 + cw] for c0, cw in runs], axis=1).astype(BF16)
    scale_a = jnp.concatenate([jnp.full((cw,), NSA_DH ** -0.5 * LOG2E if c0 < width else 1.0, F32)
                               for c0, cw in runs])
    w_g = jnp.concatenate([w_n[:, qkv_w:qkv_w + n_gate], jnp.zeros((d, 128 - n_gate), F32)],
                          axis=1).astype(BF16)
    pa = _proj(hn1, w_a, scale_a, BF16, tm=min(2048, m), tn=1024)
    pa = pa.reshape(b, s, qkv_w + width)
    pg = _proj(hn1, w_g, jnp.ones((128,), F32), F32, tm=min(2048, m), tn=128)

    raw_k = slice(qkv_w + width - 2 * kvw, qkv_w + width - kvw)
    raw_v = slice(qkv_w + width - kvw, qkv_w + width)
    k_cmp = _compress(pa[:, :, raw_k], nsa_cmp_pos_k[0], nsa_cmp_w1_k[0], nsa_cmp_w2_k[0])
    v_cmp = _compress(pa[:, :, raw_v], nsa_cmp_pos_v[0], nsa_cmp_w1_v[0], nsa_cmp_w2_v[0])
    slopes = jnp.exp2(-8.0 * jnp.arange(1, NSA_HEADS + 1, dtype=F32) / NSA_HEADS) * LOG2E
    ocmp, sel, counts = _cmp_select(pa, k_cmp, v_cmp, slopes)
    ids, cnt = _active_chunk_lists(counts)
    gates_t = pg[:, :n_gate].reshape(b, s, NSA_GROUPS, NSA_REP * 3)
    gates_t = gates_t.transpose(0, 2, 3, 1)
    y = _sel_win(ids, cnt, slopes, pa, sel, ocmp, gates_t)
    out, = _out_proj(y.reshape(m, width), nsa_w_out[0].astype(BF16), h1, final_g, last_layer=True)
    return out.reshape(b, s, d)
```

```python
import functools

import jax
import jax.numpy as jnp
import numpy as np
from jax import lax
from jax.experimental import pallas as pl
from jax.experimental.pallas import tpu as pltpu

F32 = jnp.float32
BF16 = jnp.bfloat16

EPS = 1e-6
LOG2E = 1.4426950408889634
MASKED = -2e30
M_FLOOR = -1e30
TAKEN = -3e38
N_FORCED = 3

RET_HEADS = 8
RET_DK = 256
RET_DV = 512
RET_CHUNK = 256

NSA_HEADS = 16
NSA_GROUPS = 4
NSA_REP = 4
NSA_DH = 128
CMP_LEN = 32
CMP_STRIDE = 16
SLC_BLOCK = 64
TOP_N = 16
WINDOW = 512
Q_BLOCK = 128
CMP_Q_PER_STEP = 4
SEL_Q_PER_STEP = 2
SEL_CHUNK = 256
SEL_PIECE = 128
BIG_STEP = 10
SMALL_STEP = 2

LANES = 128
VMEM_LIMIT = 56 * 1024 * 1024

NORM_PROJ_TILE = (1024, 2048)
PROJ_ROWS = 2048
PROJ_COLS = 1024
OUT_PROJ_ROWS = 512
RET_TOKENS = 2048
CMP_ROWS = 256
CMP_KEY_CHUNK = 256

_NT = (((1,), (1,)), ((), ()))
_TN = (((0,), (0,)), ((), ()))


def _sigmoid(x):
    return 1.0 / (1.0 + jnp.exp2(-LOG2E * x))


def _iota(shape, dim):
    return lax.broadcasted_iota(jnp.int32, shape, dim)


def _norm_proj_kernel(x_ref, g_ref, w_ref, s_ref, o_ref, xn_ref):
    @pl.when(pl.program_id(1) == 0)
    def _():
        x = x_ref[...]
        ms = jnp.mean(x * x, axis=-1, keepdims=True)
        xn_ref[...] = (x * lax.rsqrt(ms + EPS) * g_ref[...]).astype(xn_ref.dtype)

    acc = jnp.dot(xn_ref[...], w_ref[...], preferred_element_type=F32)
    o_ref[...] = (acc * s_ref[...]).astype(o_ref.dtype)


def _norm_proj(x2, g, w, col_scale, out_dtype, tm, tn):
    m, k = x2.shape
    n = w.shape[1]
    return pl.pallas_call(
        _norm_proj_kernel,
        grid=(m // tm, n // tn),
        in_specs=[
            pl.BlockSpec((tm, k), lambda i, j: (i, 0)),
            pl.BlockSpec((1, k), lambda i, j: (0, 0)),
            pl.BlockSpec((k, tn), lambda i, j: (0, j)),
            pl.BlockSpec((1, tn), lambda i, j: (0, j)),
        ],
        out_specs=pl.BlockSpec((tm, tn), lambda i, j: (i, j)),
        out_shape=jax.ShapeDtypeStruct((m, n), out_dtype),
        scratch_shapes=[pltpu.VMEM((tm, k), BF16)],
        compiler_params=pltpu.CompilerParams(
            dimension_semantics=("arbitrary", "arbitrary"), vmem_limit_bytes=VMEM_LIMIT),
    )(x2, g.reshape(1, k), w, col_scale.reshape(1, n))


def _proj_kernel(x_ref, w_ref, s_ref, o_ref):
    acc = jnp.dot(x_ref[...], w_ref[...], preferred_element_type=F32)
    o_ref[...] = (acc * s_ref[...]).astype(o_ref.dtype)


def _proj(x2, w, col_scale, out_dtype, tm, tn):
    m, k = x2.shape
    n = w.shape[1]
    return pl.pallas_call(
        _proj_kernel,
        grid=(m // tm, n // tn),
        in_specs=[
            pl.BlockSpec((tm, k), lambda i, j: (i, 0)),
            pl.BlockSpec((k, tn), lambda i, j: (0, j)),
            pl.BlockSpec((1, tn), lambda i, j: (0, j)),
        ],
        out_specs=pl.BlockSpec((tm, tn), lambda i, j: (i, j)),
        out_shape=jax.ShapeDtypeStruct((m, n), out_dtype),
        compiler_params=pltpu.CompilerParams(
            dimension_semantics=("arbitrary", "arbitrary"), vmem_limit_bytes=VMEM_LIMIT),
    )(x2, w, col_scale.reshape(1, n))


def _out_proj_kernel(y_ref, w_ref, res_ref, g_ref, *o_refs, last_layer):
    h = res_ref[...] + jnp.dot(y_ref[...], w_ref[...], preferred_element_type=F32)
    ms = jnp.mean(h * h, axis=-1, keepdims=True)
    hn = h * lax.rsqrt(ms + EPS) * g_ref[...]
    if last_layer:
        o_refs[0][...] = hn
    else:
        o_refs[0][...] = h
        o_refs[1][...] = hn.astype(o_refs[1].dtype)


def _out_proj(y2, w, res2, g, last_layer, tm=OUT_PROJ_ROWS):
    m, kd = y2.shape
    n = w.shape[1]
    row_blk = pl.BlockSpec((tm, n), lambda i: (i, 0))
    out_shape = [jax.ShapeDtypeStruct((m, n), F32)]
    if not last_layer:
        out_shape.append(jax.ShapeDtypeStruct((m, n), BF16))
    return pl.pallas_call(
        functools.partial(_out_proj_kernel, last_layer=last_layer),
        grid=(m // tm,),
        in_specs=[
            pl.BlockSpec((tm, kd), lambda i: (i, 0)),
            pl.BlockSpec((kd, n), lambda i: (0, 0), pipeline_mode=pl.Buffered(1)),
            row_blk,
            pl.BlockSpec((1, n), lambda i: (0, 0)),
        ],
        out_specs=[row_blk] * len(out_shape),
        out_shape=out_shape,
        compiler_params=pltpu.CompilerParams(
            dimension_semantics=("arbitrary",), vmem_limit_bytes=VMEM_LIMIT),
    )(y2, w, res2, g.reshape(1, n))


def _retention_kernel(q_ref, k_ref, v_ref, gate_ref, di_ref, qd_ref, kd_ref, cd_ref, y_ref,
                      state_ref, *, chunk, n_sub):
    @pl.when(pl.program_id(2) == 0)
    def _():
        state_ref[...] = jnp.zeros_like(state_ref)

    di = di_ref[0]
    qd = qd_ref[0]
    kd = kd_ref[0]
    cd = cd_ref[0]
    for i in range(n_sub):
        rows = pl.ds(i * chunk, chunk)
        q = q_ref[0, rows, :]
        k = k_ref[0, rows, :]
        v = v_ref[0, rows, :]
        state = state_ref[...]
        s = lax.dot_general(q, k, _NT, preferred_element_type=F32) * di
        o = jnp.dot(s.astype(BF16), v, preferred_element_type=F32)
        o = o + jnp.dot(q, state.astype(BF16), preferred_element_type=F32) * qd
        k_dec = (k.astype(F32) * kd).astype(BF16)
        state_ref[...] = state * cd + lax.dot_general(k_dec, v, _TN, preferred_element_type=F32)
        mu = jnp.mean(o, axis=-1, keepdims=True)
        oc = o - mu
        var = jnp.mean(oc * oc, axis=-1, keepdims=True)
        on = oc * lax.rsqrt(var + EPS)
        gt = gate_ref[0, rows, :].astype(F32)
        y_ref[0, rows, :] = (gt * _sigmoid(gt) * on).astype(y_ref.dtype)


def _retention(proj, tokens_per_step=RET_TOKENS):
    b, s, _ = proj.shape
    h, c = RET_HEADS, RET_CHUNK
    t = min(tokens_per_step, s)
    log_g = jnp.log1p(-jnp.exp2(-5.0 - jnp.arange(h, dtype=F32)))
    idx = jnp.arange(c, dtype=F32)
    diff = idx[:, None] - idx[None, :]
    decay_intra = jnp.where(diff >= 0, jnp.exp(diff[None] * log_g[:, None, None]), 0.0)
    q_decay = jnp.exp((idx[None, :] + 1.0) * log_g[:, None])[:, :, None]
    k_decay = jnp.exp((c - 1.0 - idx[None, :]) * log_g[:, None])[:, :, None]
    chunk_decay = jnp.exp(c * log_g)[:, None, None]
    nk = RET_HEADS * RET_DK // RET_DK
    nv = 2 * RET_HEADS * RET_DK // RET_DV
    return pl.pallas_call(
        functools.partial(_retention_kernel, chunk=c, n_sub=t // c),
        grid=(b, h, s // t),
        in_specs=[
            pl.BlockSpec((1, t, RET_DK), lambda bi, hi, ti: (bi, ti, hi)),
            pl.BlockSpec((1, t, RET_DK), lambda bi, hi, ti: (bi, ti, nk + hi)),
            pl.BlockSpec((1, t, RET_DV), lambda bi, hi, ti: (bi, ti, nv + hi)),
            pl.BlockSpec((1, t, RET_DV), lambda bi, hi, ti: (bi, ti, nv + h + hi)),
            pl.BlockSpec((1, c, c), lambda bi, hi, ti: (hi, 0, 0)),
            pl.BlockSpec((1, c, 1), lambda bi, hi, ti: (hi, 0, 0)),
            pl.BlockSpec((1, c, 1), lambda bi, hi, ti: (hi, 0, 0)),
            pl.BlockSpec((1, 1, 1), lambda bi, hi, ti: (hi, 0, 0)),
        ],
        out_specs=pl.BlockSpec((1, t, RET_DV), lambda bi, hi, ti: (bi, ti, hi)),
        out_shape=jax.ShapeDtypeStruct((b, s, h * RET_DV), BF16),
        scratch_shapes=[pltpu.VMEM((RET_DK, RET_DV), F32)],
        compiler_params=pltpu.CompilerParams(
            dimension_semantics=("arbitrary", "arbitrary", "arbitrary"),
            vmem_limit_bytes=VMEM_LIMIT),
    )(proj, proj, proj, proj, decay_intra, q_decay, k_decay, chunk_decay)


def _compress_kernel(x_ref, xn_ref, pos_ref, w1_ref, w2_ref, o_ref):
    half = CMP_LEN // 2
    width = NSA_GROUPS * NSA_DH
    for g in range(NSA_GROUPS):
        acc = jnp.zeros((x_ref.shape[1], NSA_DH), F32)
        for l in range(half):
            cols = slice(l * width + g * NSA_DH, l * width + (g + 1) * NSA_DH)
            xa = (x_ref[0, :, cols].astype(F32) + pos_ref[l:l + 1, :]).astype(BF16)
            acc += jnp.dot(xa, w1_ref[l * NSA_DH:(l + 1) * NSA_DH, :], preferred_element_type=F32)
            lb = half + l
            xb = (xn_ref[0, :, cols].astype(F32) + pos_ref[lb:lb + 1, :]).astype(BF16)
            acc += jnp.dot(xb, w1_ref[lb * NSA_DH:(lb + 1) * NSA_DH, :], preferred_element_type=F32)
        hid = (acc * _sigmoid(acc)).astype(BF16)
        out = jnp.dot(hid, w2_ref[...], preferred_element_type=F32)
        o_ref[0, :, g * NSA_DH:(g + 1) * NSA_DH] = out.astype(o_ref.dtype)


def _compress(raw, pos, w1, w2):
    b, s, width = raw.shape
    rows = s // CMP_STRIDE
    x = raw.reshape(b, rows, CMP_STRIDE * width)
    xn = jnp.concatenate([x[:, 1:], jnp.zeros_like(x[:, :1])], axis=1)
    tr = min(CMP_ROWS, rows)
    blk = pl.BlockSpec((1, tr, CMP_STRIDE * width), lambda bi, ri: (bi, ri, 0))
    return pl.pallas_call(
        _compress_kernel,
        grid=(b, rows // tr),
        in_specs=[
            blk, blk,
            pl.BlockSpec((CMP_LEN, NSA_DH), lambda bi, ri: (0, 0)),
            pl.BlockSpec((CMP_LEN * NSA_DH, NSA_DH), lambda bi, ri: (0, 0)),
            pl.BlockSpec((NSA_DH, NSA_DH), lambda bi, ri: (0, 0)),
        ],
        out_specs=pl.BlockSpec((1, tr, width), lambda bi, ri: (bi, ri, 0)),
        out_shape=jax.ShapeDtypeStruct((b, rows, width), BF16),
        compiler_params=pltpu.CompilerParams(
            dimension_semantics=("arbitrary", "arbitrary"), vmem_limit_bytes=VMEM_LIMIT),
    )(x, xn, pos, w1.astype(BF16), w2.astype(BF16))


def _stack_heads(qblk):
    return jnp.concatenate(
        [qblk[:, r * NSA_DH:(r + 1) * NSA_DH] for r in range(NSA_REP)], axis=0)


def _head_cols(r):
    return slice(r * Q_BLOCK, (r + 1) * Q_BLOCK)


def _alibi_table(slopes, n_keys, key_stride):
    j = jnp.arange(n_keys, dtype=F32)[:, None] * key_stride
    i = jnp.arange(Q_BLOCK, dtype=F32)[None, :]
    table = slopes.reshape(NSA_GROUPS, 1, NSA_REP, 1) * (j - i)[None, :, None, :]
    return table.reshape(NSA_GROUPS, n_keys, NSA_REP * Q_BLOCK)


def _chunk_shift(slopes, delta):
    row = jnp.full((1, Q_BLOCK), delta, jnp.int32).astype(F32)
    return jnp.concatenate([s * row for s in slopes], axis=1)


def _cmp_select_kernel(slopes_ref, q_ref, kc_ref, vct_ref, ovt_ref, spread_ref, bias_ref, ocmp_ref,
                       sel_ref, cnt_ref, s_scrs, *, ck, nq):
    g = pl.program_id(1)
    qi_first = pl.program_id(2) * CMP_Q_PER_STEP
    qb = Q_BLOCK
    n_ck, n_slc = ovt_ref.shape[0], ovt_ref.shape[1]
    slopes = [slopes_ref[g * NSA_REP + r] for r in range(NSA_REP)]
    rel = _iota((ck, qb), 1) - CMP_STRIDE * _iota((ck, qb), 0)

    def attend_and_select(chunks, rows, sub):
        q0 = (qi_first + sub) * qb
        q = _stack_heads(q_ref[0, sub * qb:(sub + 1) * qb, :])
        chunk_delta = lambda off: CMP_STRIDE * off + (CMP_LEN - 1) - q0
        s_scr = s_scrs.at[sub]
        ocmp_out, sel_out, cnt_out = ocmp_ref.at[0, 0, sub], sel_ref.at[0, 0, sub], cnt_ref.at[0, 0, sub]
        tops = []
        for c in range(chunks):
            off = c * ck
            st = lax.dot_general(kc_ref[0, off:off + ck, :], q, _NT, preferred_element_type=F32)
            valid = rel >= chunk_delta(off)
            top_c = []
            for r in range(NSA_REP):
                s = jnp.where(valid, st[:, _head_cols(r)] + bias_ref[0, :, _head_cols(r)], MASKED)
                s_scr[off:off + ck, _head_cols(r)] = s
                top_c.append(jnp.max(s, axis=0, keepdims=True))
            tops.append(jnp.concatenate(top_c, axis=1) + _chunk_shift(slopes, chunk_delta(off)))
        m_all = functools.reduce(jnp.maximum, tops, jnp.full((1, NSA_REP * qb), M_FLOOR, F32))

        l = jnp.zeros((1, NSA_REP * qb), F32)
        o = jnp.zeros((NSA_DH, NSA_REP * qb), F32)
        imp = jnp.zeros((rows, NSA_REP * qb), F32)
        for c in range(chunks):
            off = c * ck
            shifted_m = m_all - _chunk_shift(slopes, chunk_delta(off))
            p = jnp.exp2(s_scr[off:off + ck, :] - shifted_m)
            pb = p.astype(BF16)
            l = l + jnp.sum(p, axis=0, keepdims=True)
            o = o + jnp.dot(vct_ref[0, 0, c], pb, preferred_element_type=F32)
            imp = imp + jnp.dot(ovt_ref[c, :rows, :], pb, preferred_element_type=F32)
        inv = jnp.where(l > 0.0, 1.0 / l, 0.0)
        ocmp_out[...] = o * inv
        imp_w = imp * inv
        imp_t = imp_w[:, _head_cols(0)]
        for r in range(1, NSA_REP):
            imp_t = imp_t + imp_w[:, _head_cols(r)]

        jrow = _iota((rows, qb), 0)
        cur = lax.shift_right_logical(q0 + _iota((rows, qb), 1), int(np.log2(SLC_BLOCK)))
        forced = (jrow == 0) | (jrow == cur) | (jrow == cur - 1)
        score = jnp.where(forced, TAKEN, jnp.where(jrow <= cur, imp_t, -1.0))
        for _ in range(min(TOP_N, rows) - N_FORCED):
            best = jnp.max(score, axis=0, keepdims=True)
            first = jnp.min(jnp.where(score == best, jrow, rows), axis=0, keepdims=True)
            score = jnp.where(jrow == first, TAKEN, score)
        chosen = jnp.where(score == TAKEN, 1.0, 0.0).astype(BF16)
        sel_out[...] = jnp.dot(spread_ref[:, :rows], chosen, preferred_element_type=F32)
        cnt_out[:, :rows] = lax.dot_general(jnp.ones((8, qb), BF16), chosen, _NT,
                                            preferred_element_type=F32)
        if rows < n_slc:
            cnt_out[:, rows:] = jnp.zeros((8, n_slc - rows), F32)

    if nq % n_ck == 0 and n_slc % n_ck == 0:
        for k in range(1, n_ck + 1):
            in_share = (qi_first >= (k - 1) * (nq // n_ck)) & (qi_first < k * (nq // n_ck))

            @pl.when(in_share)
            def _(k=k):
                for sub in range(CMP_Q_PER_STEP):
                    attend_and_select(k, k * (n_slc // n_ck), sub)
    else:
        for sub in range(CMP_Q_PER_STEP):
            attend_and_select(n_ck, n_slc, sub)


def _cmp_select(pa, k_cmp, v_cmp, slopes):
    b, s, _ = pa.shape
    nq = s // Q_BLOCK
    n_cmp = k_cmp.shape[1]
    n_slc = s // SLC_BLOCK
    ck = min(CMP_KEY_CHUNK, n_cmp)
    n_ck = n_cmp // ck
    gw = NSA_REP * NSA_DH
    cstart = np.arange(n_cmp)[None, :] * CMP_STRIDE
    sstart = np.arange(n_slc)[:, None] * SLC_BLOCK
    real = np.arange(n_cmp)[None, :] < (s - CMP_LEN) // CMP_STRIDE + 1
    ovt = ((cstart < sstart + SLC_BLOCK) & (cstart + CMP_LEN > sstart) & real).astype(np.float32)
    ovt = jnp.asarray(ovt.reshape(n_slc, n_ck, ck).transpose(1, 0, 2), BF16)
    per_chunk = SEL_CHUNK // SLC_BLOCK
    n_sel_chunks = s // SEL_CHUNK
    spread = np.zeros((8 * n_sel_chunks, n_slc), np.float32)
    for j in range(n_slc):
        spread[8 * (j // per_chunk) + j % per_chunk, j] = 1.0
    spread = jnp.asarray(spread, BF16)
    vct = v_cmp.reshape(b, n_ck, ck, NSA_GROUPS, NSA_DH).transpose(0, 3, 1, 4, 2)
    bias = _alibi_table(slopes, ck, CMP_STRIDE)
    per_step = CMP_Q_PER_STEP
    assert nq % per_step == 0 and (nq // n_ck) % per_step == 0
    return pl.pallas_call(
        functools.partial(_cmp_select_kernel, ck=ck, nq=nq),
        grid=(b, NSA_GROUPS, nq // per_step),
        in_specs=[
            pl.BlockSpec(memory_space=pltpu.SMEM),
            pl.BlockSpec((1, per_step * Q_BLOCK, gw), lambda bi, gi, qi: (bi, qi, 2 * gi)),
            pl.BlockSpec((1, n_cmp, NSA_DH), lambda bi, gi, qi: (bi, 0, gi)),
            pl.BlockSpec((1, 1, n_ck, NSA_DH, ck), lambda bi, gi, qi: (bi, gi, 0, 0, 0)),
            pl.BlockSpec((n_ck, n_slc, ck), lambda bi, gi, qi: (0, 0, 0)),
            pl.BlockSpec((8 * n_sel_chunks, n_slc), lambda bi, gi, qi: (0, 0)),
            pl.BlockSpec((1, ck, gw), lambda bi, gi, qi: (gi, 0, 0)),
        ],
        out_specs=[
            pl.BlockSpec((1, 1, per_step, NSA_DH, gw), lambda bi, gi, qi: (bi, gi, qi, 0, 0)),
            pl.BlockSpec((1, 1, per_step, 8 * n_sel_chunks, Q_BLOCK),
                         lambda bi, gi, qi: (bi, gi, qi, 0, 0)),
            pl.BlockSpec((1, 1, per_step, 8, n_slc), lambda bi, gi, qi: (bi, gi, qi, 0, 0)),
        ],
        out_shape=[
            jax.ShapeDtypeStruct((b, NSA_GROUPS, nq, NSA_DH, gw), F32),
            jax.ShapeDtypeStruct((b, NSA_GROUPS, nq, 8 * n_sel_chunks, Q_BLOCK), F32),
            jax.ShapeDtypeStruct((b, NSA_GROUPS, nq, 8, n_slc), F32),
        ],
        scratch_shapes=[pltpu.VMEM((per_step, n_cmp, gw), F32)],
        compiler_params=pltpu.CompilerParams(
            dimension_semantics=("arbitrary", "arbitrary", "arbitrary"),
            vmem_limit_bytes=VMEM_LIMIT),
    )(slopes, pa, k_cmp, vct, ovt, spread, bias)


def _sel_win_kernel(ids_ref, cnt_ref, slopes_ref, qz_ref, kv_ref, sel_ref, ocmp_ref, gate_ref,
                    bias_ref, y_ref, *scratch, n_chunks, nq):
    for sub in range(SEL_Q_PER_STEP):
        _sel_win_block(sub, ids_ref, cnt_ref, slopes_ref, qz_ref, kv_ref, sel_ref, ocmp_ref,
                       gate_ref, bias_ref, y_ref, *(ref.at[sub] for ref in scratch),
                       n_chunks=n_chunks, nq=nq)


def _sel_win_block(sub, ids_ref, cnt_ref, slopes_ref, qz_ref, kv_ref, sel_ref, ocmp_ref, gate_ref,
                   bias_ref, y_ref, m_ref, l_ref, acc_ref, s_scr, w_scr, win_ref, *, n_chunks, nq):
    b = pl.program_id(0)
    g = pl.program_id(1)
    qi = pl.program_id(2) * SEL_Q_PER_STEP + sub
    qb = Q_BLOCK
    q0 = qi * qb
    tok = slice(sub * qb, (sub + 1) * qb)
    per_piece = SEL_PIECE // SLC_BLOCK
    pieces_per_chunk = SEL_CHUNK // SEL_PIECE

    gw = NSA_REP * NSA_DH
    q = _stack_heads(qz_ref[0, tok, :gw])
    slopes = [slopes_ref[g * NSA_REP + r] for r in range(NSA_REP)]
    ks_cols, vs_cols, kw_cols, vw_cols = (slice(i * NSA_DH, (i + 1) * NSA_DH) for i in range(4))

    m_ref[...] = jnp.full_like(m_ref, M_FLOOR)
    l_ref[...] = jnp.zeros_like(l_ref)
    acc_ref[...] = jnp.zeros_like(acc_ref)
    step = (b * NSA_GROUPS + g) * nq + qi
    n_active = cnt_ref[3 * step]
    n_big = cnt_ref[3 * step + 1]
    n_small = cnt_ref[3 * step + 2]
    id_base = step * n_chunks

    def stage_scores(row0, st, mask, scr=s_scr):
        n = st.shape[0]
        tops = []
        for r in range(NSA_REP):
            s = jnp.where(mask, st[:, _head_cols(r)] + bias_ref[0, :n, _head_cols(r)], MASKED)
            scr[row0:row0 + n, _head_cols(r)] = s
            tops.append(jnp.max(s, axis=0, keepdims=True))
        return jnp.concatenate(tops, axis=1)

    def piece_rows(cols, c):
        return kv_ref[0, pl.ds(pl.multiple_of(c * SEL_PIECE, SEL_PIECE), SEL_PIECE), cols]

    def piece_mask(c, threshold):
        group = lax.shift_right_logical(c, int(np.log2(pieces_per_chunk)))
        rows8 = sel_ref[0, 0, sub, pl.ds(pl.multiple_of(group * 8, 8), 8), :]
        within = c & (pieces_per_chunk - 1)
        flags = []
        for i in range(per_piece):
            row = rows8[i:i + 1]
            for w in range(1, pieces_per_chunk):
                row = jnp.where(within == w, rows8[w * per_piece + i:w * per_piece + i + 1], row)
            flags.append(jnp.broadcast_to(row, (SLC_BLOCK, qb)))
        chosen = jnp.concatenate(flags, axis=0)
        rel = _iota((SEL_PIECE, qb), 1) - _iota((SEL_PIECE, qb), 0)
        return (chosen > threshold) & (rel >= c * SEL_PIECE - q0)

    def stage_pair(slot, ca, cb, thr_a, thr_b):
        keys = jnp.concatenate([piece_rows(ks_cols, ca), piece_rows(ks_cols, cb)], axis=0)
        st = lax.dot_general(keys, q, _NT, preferred_element_type=F32)
        out = []
        for h, (c, thr) in enumerate(((ca, thr_a), (cb, thr_b))):
            rows = slice(h * SEL_PIECE, (h + 1) * SEL_PIECE)
            top = stage_scores((2 * slot + h) * SEL_PIECE, st[rows], piece_mask(c, thr))
            shift = _chunk_shift(slopes, c * SEL_PIECE - q0)
            out.append((top + shift, shift))
        return out

    def update(pieces, thresholds):
        pairs = [(pieces[2 * k], pieces[2 * k + 1]) for k in range(len(pieces) // 2)]
        staged = [stage_pair(k, ca, cb, thresholds[2 * k], thresholds[2 * k + 1])
                  for k, (ca, cb) in enumerate(pairs)]
        m_old = m_ref[...]
        m_new = functools.reduce(jnp.maximum, [top for pair in staged for top, _ in pair], m_old)
        alpha = jnp.exp2(m_old - m_new)
        l = alpha * l_ref[...]
        acc = alpha * acc_ref[...]
        for k, ((ca, cb), pair) in enumerate(zip(pairs, staged)):
            halves = []
            for h, (_, shift) in enumerate(pair):
                row0 = (2 * k + h) * SEL_PIECE
                p = jnp.exp2(s_scr[row0:row0 + SEL_PIECE, :] - (m_new - shift))
                l = l + jnp.sum(p, axis=0, keepdims=True)
                halves.append(p.astype(BF16))
            values = jnp.concatenate([piece_rows(vs_cols, ca), piece_rows(vs_cols, cb)], axis=0)
            acc = acc + lax.dot_general(values, jnp.concatenate(halves, axis=0), _TN,
                                        preferred_element_type=F32)
        m_ref[...] = m_new
        l_ref[...] = l
        acc_ref[...] = acc

    def run_step(first, n):
        pieces = [ids_ref[id_base + jnp.minimum(first + k, n_chunks - 1)] for k in range(n)]
        update(pieces, [jnp.where(first + k < n_active, 0.5, 2.0) for k in range(n)])

    def big_step(i, carry):
        run_step(BIG_STEP * (i + 1), BIG_STEP)
        return carry

    def small_step(i, carry):
        run_step(BIG_STEP * (n_big + 1) + SMALL_STEP * i, SMALL_STEP)
        return carry

    wk = WINDOW + qb
    start = pl.multiple_of(jnp.maximum(q0 - WINDOW, 0), qb)
    st = lax.dot_general(kv_ref[0, pl.ds(start, wk), kw_cols], q, _NT, preferred_element_type=F32)
    rel = _iota((wk, qb), 1) - _iota((wk, qb), 0)
    m_win = stage_scores(0, st, (rel >= start - q0) & (rel < WINDOW + start - q0), w_scr)
    p = jnp.exp2(w_scr[...] - m_win)
    l_win = jnp.sum(p, axis=0, keepdims=True)
    win_ref[...] = lax.dot_general(kv_ref[0, pl.ds(start, wk), vw_cols], p.astype(BF16), _TN,
                                   preferred_element_type=F32) * (1.0 / l_win)

    run_step(0, BIG_STEP)
    lax.fori_loop(0, n_big, big_step, 0)
    lax.fori_loop(0, n_small, small_step, 0)

    gates = _sigmoid(gate_ref[0, 0, :, tok])
    gate_row = lambda br: jnp.concatenate(
        [gates[3 * r + br:3 * r + br + 1] for r in range(NSA_REP)], axis=1)
    l_sel = l_ref[...]
    comb = (gate_row(0) * ocmp_ref[0, 0, sub]
            + gate_row(1) * (acc_ref[...] * jnp.where(l_sel > 0.0, 1.0 / l_sel, 0.0))
            + gate_row(2) * win_ref[...])
    for r in range(NSA_REP):
        cols = slice(r * NSA_DH, (r + 1) * NSA_DH)
        z = qz_ref[0, tok, gw + r * NSA_DH:gw + (r + 1) * NSA_DH].astype(F32)
        y_ref[0, tok, cols] = (z * _sigmoid(z) * comb[:, _head_cols(r)].T).astype(y_ref.dtype)


def _sel_win(ids, cnt, slopes, pa, sel, ocmp, gates_t):
    b, s, _ = pa.shape
    nq = s // Q_BLOCK
    gw = NSA_REP * NSA_DH
    kv_col = 2 * NSA_GROUPS
    per_step = SEL_Q_PER_STEP
    assert nq % per_step == 0
    per_q = lambda arr: pl.BlockSpec((1, 1, per_step) + arr.shape[3:],
                                     lambda bi, gi, qi, *_: (bi, gi, qi, 0, 0))
    grid_spec = pltpu.PrefetchScalarGridSpec(
        num_scalar_prefetch=3,
        grid=(b, NSA_GROUPS, nq // per_step),
        in_specs=[
            pl.BlockSpec((1, per_step * Q_BLOCK, 2 * gw), lambda bi, gi, qi, *_: (bi, qi, gi)),
            pl.BlockSpec((1, s, gw), lambda bi, gi, qi, *_: (bi, 0, kv_col + gi)),
            per_q(sel), per_q(ocmp),
            pl.BlockSpec((1, 1, NSA_REP * 3, per_step * Q_BLOCK),
                         lambda bi, gi, qi, *_: (bi, gi, 0, qi)),
            pl.BlockSpec((1, WINDOW + Q_BLOCK, gw), lambda bi, gi, qi, *_: (gi, 0, 0)),
        ],
        out_specs=pl.BlockSpec((1, per_step * Q_BLOCK, gw), lambda bi, gi, qi, *_: (bi, qi, gi)),
        scratch_shapes=[
            pltpu.VMEM((per_step, 1, gw), F32),
            pltpu.VMEM((per_step, 1, gw), F32),
            pltpu.VMEM((per_step, NSA_DH, gw), F32),
            pltpu.VMEM((per_step, BIG_STEP * SEL_PIECE, gw), F32),
            pltpu.VMEM((per_step, WINDOW + Q_BLOCK, gw), F32),
            pltpu.VMEM((per_step, NSA_DH, gw), F32),
        ],
    )
    return pl.pallas_call(
        functools.partial(_sel_win_kernel, n_chunks=s // SEL_PIECE, nq=nq),
        grid_spec=grid_spec,
        out_shape=jax.ShapeDtypeStruct((b, s, NSA_GROUPS * gw), BF16),
        compiler_params=pltpu.CompilerParams(
            dimension_semantics=("arbitrary", "arbitrary", "arbitrary"),
            vmem_limit_bytes=VMEM_LIMIT),
    )(ids, cnt, slopes, pa, pa, sel, ocmp, gates_t, _alibi_table(slopes, WINDOW + Q_BLOCK, 1))


def _active_chunk_lists(counts):
    b, g, nq, _, n_slc = counts.shape
    per_chunk = SEL_PIECE // SLC_BLOCK
    n_chunks = n_slc // per_chunk
    chunk = jnp.arange(n_chunks, dtype=jnp.int32)
    causal = chunk[None, :] <= (jnp.arange(nq, dtype=jnp.int32)[:, None] * Q_BLOCK) // SEL_PIECE
    active = counts[:, :, :, 0].reshape(b, g, nq, n_chunks, per_chunk).max(axis=-1) > 0.5
    active = active & causal
    ids = jnp.sort(jnp.where(active, chunk, chunk + n_chunks), axis=-1) % n_chunks
    n = active.sum(axis=-1, dtype=jnp.int32)
    left = jnp.maximum(n - BIG_STEP, 0)
    full, rest = left // BIG_STEP, left % BIG_STEP
    one_more = rest > 2 * SMALL_STEP
    n_big = full + one_more
    n_small = jnp.where(one_more, 0, (rest + SMALL_STEP - 1) // SMALL_STEP)
    return ids.reshape(-1), jnp.stack([n, n_big, n_small], axis=-1).astype(jnp.int32).reshape(-1)


def kernel(x, norm_g, ret_w_in, ret_w_out, nsa_w_in, nsa_cmp_pos_k, nsa_cmp_w1_k, nsa_cmp_w2_k,
           nsa_cmp_pos_v, nsa_cmp_w1_v, nsa_cmp_w2_v, nsa_w_out, final_g):
    b, s, d = x.shape
    m = b * s
    x2 = x.reshape(m, d)

    qk_w = 2 * RET_HEADS * RET_DK
    v_w = RET_HEADS * RET_DV
    w_in = ret_w_in[0].astype(BF16)
    scale = jnp.concatenate([jnp.ones((qk_w // 2,), F32), jnp.full((qk_w // 2,), RET_DK ** -0.5, F32),
                             jnp.ones((2 * v_w,), F32)])
    proj = _norm_proj(x2, norm_g[0], w_in, scale, BF16, *NORM_PROJ_TILE)
    y = _retention(proj.reshape(b, s, qk_w + 2 * v_w))
    h1, hn1 = _out_proj(y.reshape(m, v_w), ret_w_out[0].astype(BF16), x2, norm_g[1],
                        last_layer=False)

    width = NSA_HEADS * NSA_DH
    kvw = NSA_GROUPS * NSA_DH
    n_gate = NSA_HEADS * 3
    qkv_w = width + 6 * kvw
    w_n = nsa_w_in[0]
    gw = NSA_REP * NSA_DH
    kv_start = lambda slot: width + slot * kvw
    z_start = qkv_w + n_gate
    runs = []
    for gi in range(NSA_GROUPS):
        runs += [(gi * gw, gw), (z_start + gi * gw, gw)]
    for gi in range(NSA_GROUPS):
        runs += [(kv_start(slot) + gi * NSA_DH, NSA_DH) for slot in (2, 3, 4, 5)]
    runs += [(kv_start(0), kvw), (kv_start(1), kvw)]
    w_a = jnp.concatenate([w_n[:, c0:c0 + cw] for c0, cw in runs], axis=1).astype(BF16)
    scale_a = jnp.concatenate([jnp.full((cw,), NSA_DH ** -0.5 * LOG2E if c0 < width else 1.0, F32)
                               for c0, cw in runs])
    w_g = jnp.concatenate([w_n[:, qkv_w:qkv_w + n_gate], jnp.zeros((d, LANES - n_gate), F32)],
                          axis=1).astype(BF16)
    pa = _proj(hn1, w_a, scale_a, BF16, tm=min(PROJ_ROWS, m), tn=PROJ_COLS)
    pa = pa.reshape(b, s, qkv_w + width)
    pg = _proj(hn1, w_g, jnp.ones((LANES,), F32), F32, tm=min(PROJ_ROWS, m), tn=LANES)

    raw_k = slice(qkv_w + width - 2 * kvw, qkv_w + width - kvw)
    raw_v = slice(qkv_w + width - kvw, qkv_w + width)
    k_cmp = _compress(pa[:, :, raw_k], nsa_cmp_pos_k[0], nsa_cmp_w1_k[0], nsa_cmp_w2_k[0])
    v_cmp = _compress(pa[:, :, raw_v], nsa_cmp_pos_v[0], nsa_cmp_w1_v[0], nsa_cmp_w2_v[0])
    slopes = jnp.exp2(-8.0 * jnp.arange(1, NSA_HEADS + 1, dtype=F32) / NSA_HEADS) * LOG2E
    ocmp, sel, counts = _cmp_select(pa, k_cmp, v_cmp, slopes)
    ids, cnt = _active_chunk_lists(counts)
    gates_t = pg[:, :n_gate].reshape(b, s, NSA_GROUPS, NSA_REP * 3)
    gates_t = gates_t.transpose(0, 2, 3, 1)
    y = _sel_win(ids, cnt, slopes, pa, sel, ocmp, gates_t)
    out, = _out_proj(y.reshape(m, width), nsa_w_out[0].astype(BF16), h1, final_g, last_layer=True)
    return out.reshape(b, s, d)
```
